```python
import jax, jax.numpy as jnp
from jax import lax
import numpy as np

D_MODEL = 1024
BATCH = 16
SEQ = 2048
DEPTH = 2

GRID_W = 64
CTX_LEN = 256
ROPE_THETA = 10000.0
NORM_EPS = 1e-6
NEG = -1e30

BRANCH_WIDTH = D_MODEL // 2
N_BRANCH = 3

A_HEAD = 64
A_HEADS = BRANCH_WIDTH // A_HEAD
A_DECAY_LORA = 64
A_ICLR_LORA = 64
A_GN_EPS = 64e-5
A_SHIFT_COLS = 3 * BRANCH_WIDTH + 2 * A_DECAY_LORA + 2 * A_ICLR_LORA

B_HEADS = 8
B_NOPE = 64
B_ROPE = 32
B_QK = B_NOPE + B_ROPE
B_V = BRANCH_WIDTH // B_HEADS
B_Q_LORA = 256
B_KV_LORA = 128
B_PROJ_COLS = B_Q_LORA + B_KV_LORA + B_ROPE
Q_BLOCK = 128

C_HEAD = 64
C_HEADS = BRANCH_WIDTH // C_HEAD
C_KV_HEADS = 2
C_GROUP = C_HEADS // C_KV_HEADS
C_KV_WIDTH = C_KV_HEADS * C_HEAD
C_PROJ_COLS = BRANCH_WIDTH + 2 * C_KV_WIDTH
WINDOW = 128
W_BLOCK = 128

IN_CUTS = (
    A_SHIFT_COLS,
    A_SHIFT_COLS + BRANCH_WIDTH,
    A_SHIFT_COLS + BRANCH_WIDTH + B_PROJ_COLS,
    A_SHIFT_COLS + 2 * BRANCH_WIDTH + B_PROJ_COLS,
    A_SHIFT_COLS + 2 * BRANCH_WIDTH + B_PROJ_COLS + C_PROJ_COLS,
    A_SHIFT_COLS + 3 * BRANCH_WIDTH + B_PROJ_COLS + C_PROJ_COLS,
)
N_IN = IN_CUTS[-1] + N_BRANCH * D_MODEL

kernel_name = "hybrid_rwkv7_mla_swa_diffusion_block"


def rms_norm(x, gain, eps=NORM_EPS):
    xf = x.astype(jnp.float32)
    y = xf * lax.rsqrt(jnp.mean(xf * xf, axis=-1, keepdims=True) + eps)
    return (y * gain.astype(jnp.float32)).astype(x.dtype)


def axial_rope_tables(n_tokens, rot_dim):
    rows = n_tokens // GRID_W
    row = jnp.repeat(jnp.arange(rows), GRID_W).astype(jnp.float32)
    col = jnp.tile(jnp.arange(GRID_W), rows).astype(jnp.float32)
    axis_dim = rot_dim // 2
    inv = ROPE_THETA ** (-(2.0 * jnp.arange(axis_dim // 2, dtype=jnp.float32)) / axis_dim)
    ang = jnp.concatenate([row[:, None] * inv, col[:, None] * inv], axis=-1)
    return jnp.cos(ang), jnp.sin(ang)


def apply_axial_rope(x, cos, sin):
    a = x.shape[-1] // 2
    q = a // 2
    c = cos[:, None, :].astype(x.dtype)
    s = sin[:, None, :].astype(x.dtype)

    def rot(xa, ca, sa):
        x1, x2 = xa[..., :q], xa[..., q:]
        return jnp.concatenate([x1 * ca - x2 * sa, x1 * sa + x2 * ca], axis=-1)

    return jnp.concatenate([rot(x[..., :a], c[..., :q], s[..., :q]),
                            rot(x[..., a:], c[..., q:], s[..., q:])], axis=-1)


def token_shift(z, mu_prev, mu_next):
    prev = jnp.pad(z[:, :-1], ((0, 0), (1, 0), (0, 0)))
    nxt = jnp.pad(z[:, 1:], ((0, 0), (0, 1), (0, 0)))
    return z + mu_prev * (prev - z) + mu_next * (nxt - z)


def rwkv7_prepare(z, mu_prev, mu_next, w0, w_up, a0, a_up, k_k, k_a):
    B, T, _ = z.shape
    z = token_shift(z, mu_prev, mu_next).astype(jnp.float32)
    r, k, v, wd, ad = jnp.split(
        z, [BRANCH_WIDTH, 2 * BRANCH_WIDTH, 3 * BRANCH_WIDTH, 3 * BRANCH_WIDTH + 2 * A_DECAY_LORA], axis=-1)
    wd = wd.reshape(B, T, 2, A_DECAY_LORA)
    ad = ad.reshape(B, T, 2, A_ICLR_LORA)
    w_log = -jax.nn.softplus(-(w0 + jnp.einsum('btdr,drc->btdc', jnp.tanh(wd), w_up))) - 0.5
    decay = jnp.exp(-jnp.exp(w_log))
    a = jax.nn.sigmoid(a0 + jnp.einsum('btdr,drc->btdc', ad, a_up))
    kk = (k * k_k).reshape(B, T, A_HEADS, A_HEAD)
    kk = kk * lax.rsqrt(jnp.maximum(jnp.sum(kk * kk, axis=-1, keepdims=True), 1e-24))
    k_dir = k[:, :, None, :] * (1.0 + (a - 1.0) * k_a)
    heads = lambda t: t.reshape(B, T, 2, A_HEADS, A_HEAD)
    return (r.reshape(B, T, A_HEADS, A_HEAD), v.reshape(B, T, A_HEADS, A_HEAD), kk,
            heads(decay), heads(k_dir), heads(a))


def rwkv7_scan(r, decay, k, v, kk, a, s0, reverse):
    def step(S, inp):
        r_t, w_t, k_t, v_t, kk_t, a_t = inp
        sa = jnp.einsum('bhvk,bhk->bhv', S, kk_t)
        S = (S * w_t[:, :, None, :] - sa[..., None] * (kk_t * a_t)[:, :, None, :]
             + v_t[..., None] * k_t[:, :, None, :])
        return S, jnp.einsum('bhvk,bhk->bhv', S, r_t)

    xs = tuple(jnp.swapaxes(t, 0, 1) for t in (r, decay, k, v, kk, a))
    s_final, ys = lax.scan(step, s0, xs, reverse=reverse)
    return s_final, jnp.swapaxes(ys, 0, 1)


def rwkv7_readout(y, r, v, k_dir, r_k, gn_g, gn_b, dtype):
    B, T = y.shape[:2]
    mu = jnp.mean(y, axis=-1, keepdims=True)
    var = jnp.mean(jnp.square(y - mu), axis=-1, keepdims=True)
    yn = ((y - mu) * lax.rsqrt(var + A_GN_EPS)).reshape(B, T, BRANCH_WIDTH) * gn_g + gn_b
    bonus = jnp.sum(r[:, :, None] * k_dir * r_k, axis=-1, keepdims=True) * v[:, :, None]
    return (yn + jnp.sum(bonus, axis=2).reshape(B, T, BRANCH_WIDTH)).astype(dtype)


def rwkv7_branch(za, zac, mu_prev, mu_next, w0, w_up, a0, a_up, k_k, k_a, r_k, gn_g, gn_b, ctx_out):
    lat = rwkv7_prepare(za, mu_prev, mu_next, w0, w_up, a0, a_up, k_k, k_a)
    cx = rwkv7_prepare(zac, mu_prev, mu_next, w0, w_up, a0, a_up, k_k, k_a)
    B = za.shape[0]
    s0 = jnp.zeros((B, A_HEADS, A_HEAD, A_HEAD), jnp.float32)
    y_lat = jnp.zeros(lat[0].shape, jnp.float32)
    y_ctx = jnp.zeros(cx[0].shape, jnp.float32)
    for d, rev in ((0, False), (1, True)):
        sel = lambda p: (p[0], p[3][:, :, d], p[4][:, :, d], p[1], p[2], p[5][:, :, d])
        s_ctx, yc = rwkv7_scan(*sel(cx), s0, rev)
        _, yl = rwkv7_scan(*sel(lat), s_ctx, rev)
        y_lat = y_lat + yl
        y_ctx = y_ctx + yc
    out_lat = rwkv7_readout(y_lat, lat[0], lat[1], lat[4], r_k, gn_g, gn_b, za.dtype)
    out_ctx = rwkv7_readout(y_ctx, cx[0], cx[1], cx[4], r_k, gn_g, gn_b, za.dtype) if ctx_out else None
    return out_lat, out_ctx


def mla_project(zb, q_ln, kv_ln, w_uq, w_ukv, qn_g, kn_g, rope):
    B, T, _ = zb.shape
    cq, ckv, kr = jnp.split(zb, [B_Q_LORA, B_Q_LORA + B_KV_LORA], axis=-1)
    q = (rms_norm(cq, q_ln) @ w_uq).reshape(B, T, B_HEADS, B_QK)
    kv = (rms_norm(ckv, kv_ln) @ w_ukv).reshape(B, T, B_HEADS, B_NOPE + B_V)
    k_nope, v = kv[..., :B_NOPE], kv[..., B_NOPE:]
    k = jnp.concatenate([k_nope, jnp.broadcast_to(kr[:, :, None, :], (B, T, B_HEADS, B_ROPE))], axis=-1)
    q = rms_norm(q, qn_g)
    k = rms_norm(k, kn_g)
    if rope is not None:
        cos, sin = rope
        q = jnp.concatenate([q[..., :B_NOPE], apply_axial_rope(q[..., B_NOPE:], cos, sin)], axis=-1)
        k = jnp.concatenate([k[..., :B_NOPE], apply_axial_rope(k[..., B_NOPE:], cos, sin)], axis=-1)
    return q, k, v


def dense_block_attention(q, k, v):
    B, T, H, Dq = q.shape
    nb = T // Q_BLOCK
    qb = jnp.swapaxes(q.reshape(B, nb, Q_BLOCK, H, Dq), 0, 1)

    def one(qx):
        s = jnp.einsum('bqhd,bkhd->bhqk', qx, k, preferred_element_type=jnp.float32) * (Dq ** -0.5)
        p = jax.nn.softmax(s, axis=-1).astype(v.dtype)
        return jnp.einsum('bhqk,bkhd->bqhd', p, v)

    o = lax.map(one, qb)
    return jnp.swapaxes(o, 0, 1).reshape(B, T, H * v.shape[-1])


def gqa_project(zc, qn_g, kn_g, rope):
    B, T, _ = zc.shape
    q, k, v = jnp.split(zc, [BRANCH_WIDTH, BRANCH_WIDTH + C_KV_WIDTH], axis=-1)
    q = rms_norm(q.reshape(B, T, C_HEADS, C_HEAD), qn_g)
    k = rms_norm(k.reshape(B, T, C_KV_HEADS, C_HEAD), kn_g)
    v = v.reshape(B, T, C_KV_HEADS, C_HEAD)
    if rope is not None:
        q = apply_axial_rope(q, *rope)
        k = apply_axial_rope(k, *rope)
    return q, k, v


def sink_gqa(q, k, v, sink, valid):
    B, Q, H, D = q.shape
    qg = q.reshape(B, Q, C_KV_HEADS, C_GROUP, D)
    s = jnp.einsum('bqkgd,bskd->bkgqs', qg, k, preferred_element_type=jnp.float32) * (D ** -0.5)
    s = jnp.where(valid, s, NEG)
    sk = jnp.broadcast_to(sink.astype(jnp.float32).reshape(C_KV_HEADS, C_GROUP, 1, 1), s.shape[:-1] + (1,))
    p = jax.nn.softmax(jnp.concatenate([s, sk], axis=-1), axis=-1)[..., :-1].astype(v.dtype)
    return jnp.einsum('bkgqs,bskd->bqkgd', p, v).reshape(B, Q, H * D)


def window_attention(q, k, v, kc, vc, sink):
    B, T, H, D = q.shape
    L = kc.shape[1]
    nb = T // W_BLOCK

    def band(t):
        tp = jnp.pad(t, ((0, 0), (W_BLOCK, W_BLOCK), (0, 0), (0, 0)))
        views = [tp[:, j * W_BLOCK: j * W_BLOCK + T].reshape(B, nb, W_BLOCK, C_KV_HEADS, D) for j in range(3)]
        return jnp.swapaxes(jnp.concatenate(views, axis=2), 0, 1)

    kb, vb = band(k), band(v)
    qb = jnp.swapaxes(q.reshape(B, nb, W_BLOCK, H, D), 0, 1)
    qi = jnp.arange(W_BLOCK)[:, None]
    kj = jnp.arange(3 * W_BLOCK)[None, :]
    kpos = jnp.arange(nb)[:, None, None] * W_BLOCK + kj - W_BLOCK
    valid = (jnp.abs(kj - W_BLOCK - qi) <= WINDOW)[None] & (kpos >= 0) & (kpos < T)
    valid = jnp.concatenate([valid, jnp.ones((nb, W_BLOCK, L), bool)], axis=-1)

    def one(args):
        qx, kx, vx, m = args
        return sink_gqa(qx, jnp.concatenate([kx, kc], axis=1), jnp.concatenate([vx, vc], axis=1), sink, m)

    o = lax.map(one, (qb, kb, vb, valid))
    return jnp.swapaxes(o, 0, 1).reshape(B, T, H * D)


def merge_branches(ys, gs, zg, w_branch_out, w_out):
    gates = jax.nn.sigmoid(zg).reshape(zg.shape[:-1] + (N_BRANCH, D_MODEL))
    m = gates[..., 0, :] * ((ys[0] * jax.nn.silu(gs[0])) @ w_branch_out[0])
    for n in range(1, N_BRANCH):
        m = m + gates[..., n, :] * ((ys[n] * jax.nn.silu(gs[n])) @ w_branch_out[n])
    return m @ w_out


def trunk_layer(x, xc, c, c_ctx, ada_w, ada_b, norm_g, w_in,
                a_mu_prev, a_mu_next, a_w0, a_w_up, a_a0, a_a_up, a_k_k, a_k_a, a_r_k, a_gn_g, a_gn_b,
                b_q_ln, b_kv_ln, b_w_uq, b_w_ukv, b_qn_g, b_kn_g,
                c_qn_g, c_kn_g, c_sink, w_branch_out, w_out, rope_b, rope_c, ctx_out):
    mod = jax.nn.silu(c) @ ada_w + ada_b
    mod_c = jax.nn.silu(c_ctx) @ ada_w + ada_b
    shift, scale, gate = jnp.split(mod[:, None, :], 3, axis=-1)
    shift_c, scale_c, gate_c = jnp.split(mod_c, 3, axis=-1)
    h = rms_norm(x, norm_g) * (1.0 + scale) + shift
    hc = rms_norm(xc, norm_g) * (1.0 + scale_c) + shift_c

    za, ga, zb, gb, zc, gc, zg = jnp.split(h @ w_in, list(IN_CUTS), axis=-1)
    zac, gac, zbc, gbc, zcc, gcc, zgc = jnp.split(hc @ w_in, list(IN_CUTS), axis=-1)

    ya, yac = rwkv7_branch(za, zac, a_mu_prev, a_mu_next, a_w0, a_w_up, a_a0, a_a_up,
                           a_k_k, a_k_a, a_r_k, a_gn_g, a_gn_b, ctx_out)

    qb_, kb_, vb_ = mla_project(zb, b_q_ln, b_kv_ln, b_w_uq, b_w_ukv, b_qn_g, b_kn_g, rope_b)
    qbc, kbc, vbc = mla_project(zbc, b_q_ln, b_kv_ln, b_w_uq, b_w_ukv, b_qn_g, b_kn_g, None)
    yb = dense_block_attention(qb_, jnp.concatenate([kbc, kb_], axis=1), jnp.concatenate([vbc, vb_], axis=1))

    qc_, kc_, vc_ = gqa_project(zc, c_qn_g, c_kn_g, rope_c)
    qcc, kcc, vcc = gqa_project(zcc, c_qn_g, c_kn_g, None)
    yc = window_attention(qc_, kc_, vc_, kcc, vcc, c_sink)

    x_new = x + gate * merge_branches((ya, yb, yc), (ga, gb, gc), zg, w_branch_out, w_out)
    if ctx_out:
        ybc = dense_block_attention(qbc, kbc, vbc)
        L = xc.shape[1]
        ycc = sink_gqa(qcc, kcc, vcc, c_sink, jnp.ones((L, L), bool))
        xc_new = xc + gate_c * merge_branches((yac, ybc, ycc), (gac, gbc, gcc), zgc, w_branch_out, w_out)
    else:
        xc_new = xc
    return x_new, xc_new


def setup_inputs(seed: int = 0) -> dict:
    key = jax.random.key(seed)
    ks = iter(jax.random.split(key, 48))
    nrm = lambda shape, s: jax.random.normal(next(ks), shape, jnp.float32) * s
    uni = lambda shape, lo, hi: jax.random.uniform(next(ks), shape, jnp.float32, lo, hi)
    L = DEPTH
    return {
        "x": nrm((BATCH, SEQ, D_MODEL), 1.0),
        "c": nrm((BATCH, D_MODEL), 1.0),
        "ctx": nrm((BATCH, CTX_LEN, D_MODEL), 1.0),
        "c_ctx": nrm((D_MODEL,), 1.0),
        "ada_w": nrm((L, D_MODEL, 3 * D_MODEL), 0.5 * D_MODEL ** -0.5),
        "ada_b": nrm((L, 3 * D_MODEL), 0.02),
        "norm_g": 1.0 + nrm((L, D_MODEL), 0.02),
        "w_in": nrm((L, D_MODEL, N_IN), D_MODEL ** -0.5),
        "a_mu_prev": uni((L, A_SHIFT_COLS), 0.0, 0.5),
        "a_mu_next": uni((L, A_SHIFT_COLS), 0.0, 0.5),
        "a_w0": uni((L, 2, BRANCH_WIDTH), -6.0, 1.0),
        "a_w_up": nrm((L, 2, A_DECAY_LORA, BRANCH_WIDTH), 0.1),
        "a_a0": nrm((L, 2, BRANCH_WIDTH), 0.5),
        "a_a_up": nrm((L, 2, A_ICLR_LORA, BRANCH_WIDTH), 0.1),
        "a_k_k": 0.85 + nrm((L, BRANCH_WIDTH), 0.02),
        "a_k_a": 1.0 + nrm((L, BRANCH_WIDTH), 0.02),
        "a_r_k": nrm((L, A_HEADS, A_HEAD), 0.1),
        "a_gn_g": 1.0 + nrm((L, BRANCH_WIDTH), 0.02),
        "a_gn_b": nrm((L, BRANCH_WIDTH), 0.02),
        "b_q_ln": 1.0 + nrm((L, B_Q_LORA), 0.02),
        "b_kv_ln": 1.0 + nrm((L, B_KV_LORA), 0.02),
        "b_w_uq": nrm((L, B_Q_LORA, B_HEADS * B_QK), B_Q_LORA ** -0.5),
        "b_w_ukv": nrm((L, B_KV_LORA, B_HEADS * (B_NOPE + B_V)), B_KV_LORA ** -0.5),
        "b_qn_g": 1.0 + nrm((L, B_QK), 0.02),
        "b_kn_g": 1.0 + nrm((L, B_QK), 0.02),
        "c_qn_g": 1.0 + nrm((L, C_HEAD), 0.02),
        "c_kn_g": 1.0 + nrm((L, C_HEAD), 0.02),
        "c_sink": nrm((L, C_HEADS), 0.5),
        "w_branch_out": nrm((L, N_BRANCH, BRANCH_WIDTH, D_MODEL), BRANCH_WIDTH ** -0.5),
        "w_out": nrm((L, D_MODEL, D_MODEL), D_MODEL ** -0.5),
    }


def reference(x, c, ctx, c_ctx, ada_w, ada_b, norm_g, w_in,
              a_mu_prev, a_mu_next, a_w0, a_w_up, a_a0, a_a_up, a_k_k, a_k_a, a_r_k, a_gn_g, a_gn_b,
              b_q_ln, b_kv_ln, b_w_uq, b_w_ukv, b_qn_g, b_kn_g,
              c_qn_g, c_kn_g, c_sink, w_branch_out, w_out):
    n_tok = x.shape[1]
    rope_b = axial_rope_tables(n_tok, B_ROPE)
    rope_c = axial_rope_tables(n_tok, C_HEAD)
    xc = ctx
    for i in range(DEPTH):
        x, xc = trunk_layer(
            x, xc, c, c_ctx, ada_w[i], ada_b[i], norm_g[i], w_in[i],
            a_mu_prev[i], a_mu_next[i], a_w0[i], a_w_up[i], a_a0[i], a_a_up[i],
            a_k_k[i], a_k_a[i], a_r_k[i], a_gn_g[i], a_gn_b[i],
            b_q_ln[i], b_kv_ln[i], b_w_uq[i], b_w_ukv[i], b_qn_g[i], b_kn_g[i],
            c_qn_g[i], c_kn_g[i], c_sink[i], w_branch_out[i], w_out[i],
            rope_b, rope_c, i < DEPTH - 1)
    return x
```

```python
import functools

import numpy as np
import jax
import jax.numpy as jnp
from jax import lax
from jax.experimental import pallas as pl
from jax.experimental.pallas import tpu as pltpu

F32 = jnp.float32
BF16 = jnp.bfloat16

D_MODEL = 1024
GRID_W = 64
ROPE_THETA = 10000.0
NORM_EPS = 1e-6
NEG = -1e30
BW = 512
N_BRANCH = 3

A_HEAD = 64
A_HEADS = BW // A_HEAD
A_LORA = 64
A_GN_EPS = 64e-5
A_COLS = 3 * BW + 4 * A_LORA

B_HEADS = 8
B_NOPE = 64
B_ROPE = 32
B_QK = B_NOPE + B_ROPE
B_V = BW // B_HEADS
B_Q_LORA = 256
B_KV_LORA = 128
B_COLS = B_Q_LORA + B_KV_LORA + B_ROPE
B_COLS_P = 512

C_HEAD = 64
C_HEADS = BW // C_HEAD
C_KV_HEADS = 2
C_GROUP = C_HEADS // C_KV_HEADS
C_KV_W = C_KV_HEADS * C_HEAD
C_COLS = BW + 2 * C_KV_W
WINDOW = 128

LANE = 128
CHUNK = 64
TOK_TILE = 256
VMEM_LIMIT = 56 * 1024 * 1024

IN_SEGS = (("za", A_COLS), ("ga", BW), ("zb", B_COLS_P), ("gb", BW),
           ("zc", C_COLS), ("gc", BW), ("zg", N_BRANCH * D_MODEL))
N_IN_P = sum(w for _, w in IN_SEGS)

NN = (((1,), (0,)), ((), ()))
NT = (((1,), (1,)), ((), ()))


def _dot(a, b, dims=NN):
    return lax.dot_general(a, b, dims, preferred_element_type=F32)


def _split(x):
    hi = x.astype(BF16)
    lo = (x - hi.astype(F32)).astype(BF16)
    return hi, lo


def _mm(a, b, dims=NN, passes=1):
    if passes == 1:
        return _dot(a.astype(BF16), b.astype(BF16), dims)
    ah, al = _split(a)
    bh, bl = _split(b)
    return _dot(ah, bh, dims) + (_dot(ah, bl, dims) + _dot(al, bh, dims))


def _mm_exact_rhs(a, b_bf16, parts=2):
    acc = None
    rem = a
    for _ in range(parts):
        p = rem.astype(BF16)
        t = _dot(p, b_bf16)
        acc = t if acc is None else acc + t
        rem = rem - p.astype(F32)
    return acc


def _sigmoid(x):
    return 1.0 / (1.0 + jnp.exp(-x))


def _silu(x):
    return x * _sigmoid(x)


def _params(sem):
    return pltpu.CompilerParams(dimension_semantics=sem, vmem_limit_bytes=VMEM_LIMIT)


def _mod_kernel(c_ref, w_ref, b_ref, o_ref):
    s = _silu(c_ref[...])
    o_ref[0] = _mm(s, w_ref[0], passes=3) + b_ref[0]


def _mod_call(cs, ada_w, ada_b):
    n_layers = ada_w.shape[0]
    rows = cs.shape[0]
    return pl.pallas_call(
        _mod_kernel,
        out_shape=jax.ShapeDtypeStruct((n_layers, rows, 3 * D_MODEL), F32),
        grid=(n_layers, 3),
        in_specs=[
            pl.BlockSpec((rows, D_MODEL), lambda l, n: (0, 0)),
            pl.BlockSpec((1, D_MODEL, D_MODEL), lambda l, n: (l, 0, n)),
            pl.BlockSpec((1, 1, D_MODEL), lambda l, n: (l, 0, n)),
        ],
        out_specs=pl.BlockSpec((1, rows, D_MODEL), lambda l, n: (l, 0, n)),
        compiler_params=_params(("arbitrary", "arbitrary")),
        name="mod",
    )(cs, ada_w, ada_b.reshape(n_layers, 1, 3 * D_MODEL))


def _inproj_kernel(x_ref, shift_ref, scale_ref, g_ref, w_ref, *o_refs):
    x = x_ref[0]
    ms = jnp.mean(x * x, axis=-1, keepdims=True)
    h = x * lax.rsqrt(ms + NORM_EPS) * g_ref[...]
    h = h * (1.0 + scale_ref[0]) + shift_ref[0]
    hb = h.astype(BF16)
    col = 0
    for (_, width), o_ref in zip(IN_SEGS, o_refs):
        step = 512 if width % 512 == 0 else 256
        for c0 in range(0, width, step):
            o_ref[0, :, c0:c0 + step] = _dot(hb, w_ref[:, col + c0:col + c0 + step])
        col += width


def _mod_row(ctx_tiles, n_batch_rows):
    return lambda b, t: jnp.where(t < ctx_tiles, n_batch_rows, b)


def _inproj_call(xall, modl, norm_g, w_in_p, ctx_len):
    B, S, _ = xall.shape
    tm = TOK_TILE
    row = _mod_row(ctx_len // tm, B)
    mod3 = modl.reshape(modl.shape[0], 1, 3 * D_MODEL)
    outs = tuple(jax.ShapeDtypeStruct((B, S, w), F32) for _, w in IN_SEGS)
    return pl.pallas_call(
        _inproj_kernel,
        out_shape=outs,
        grid=(B, S // tm),
        in_specs=[
            pl.BlockSpec((1, tm, D_MODEL), lambda b, t: (b, t, 0)),
            pl.BlockSpec((1, 1, D_MODEL), lambda b, t: (row(b, t), 0, 0)),
            pl.BlockSpec((1, 1, D_MODEL), lambda b, t: (row(b, t), 0, 1)),
            pl.BlockSpec((1, D_MODEL), lambda b, t: (0, 0)),
            pl.BlockSpec((D_MODEL, N_IN_P), lambda b, t: (0, 0),
                         pipeline_mode=pl.Buffered(1)),
        ],
        out_specs=tuple(pl.BlockSpec((1, tm, w), lambda b, t: (b, t, 0)) for _, w in IN_SEGS),
        compiler_params=_params(("parallel", "arbitrary")),
        name="inproj",
    )(xall, mod3, mod3, norm_g.reshape(1, D_MODEL), w_in_p)


def _prep_kernel(ctx_tiles, n_tiles,
                 z_ref, zp_ref, zn_ref, mup_ref, mun_ref, w0_ref, wup_ref, a0_ref, aup_ref,
                 kk_ref, ka_ref, rk_ref, ones_ref,
                 r_out, v_out, kk_out, lw_out, kd_out, nb_out, bonus_out, zs_ref):
    t = pl.program_id(1)
    tt = z_ref.shape[1]
    first = jnp.logical_or(t == 0, t == ctx_tiles)
    last = jnp.logical_or(t == ctx_tiles - 1, t == n_tiles - 1)
    z = z_ref[0]
    zs_ref[0:8, :] = jnp.where(first, 0.0, zp_ref[0])
    zs_ref[8:8 + tt, :] = z
    zs_ref[8 + tt:16 + tt, :] = jnp.where(last, 0.0, zn_ref[0])
    prev = zs_ref[7:7 + tt, :]
    nxt = zs_ref[9:9 + tt, :]
    zsh = z + mup_ref[...] * (prev - z) + mun_ref[...] * (nxt - z)

    r = zsh[:, 0:BW]
    k = zsh[:, BW:2 * BW]
    v = zsh[:, 2 * BW:3 * BW]
    ones = ones_ref[...]
    kk = k * kk_ref[...]
    ss = _mm_exact_rhs(kk * kk, ones)
    kk = kk * lax.rsqrt(jnp.maximum(ss, 1e-24))
    r_out[0] = r
    v_out[0] = v
    kk_out[0] = kk
    kd_sum = None
    for d in range(2):
        wd = zsh[:, 3 * BW + d * A_LORA:3 * BW + (d + 1) * A_LORA]
        ad = zsh[:, 3 * BW + 2 * A_LORA + d * A_LORA:3 * BW + 2 * A_LORA + (d + 1) * A_LORA]
        u = -(w0_ref[d:d + 1, :] + _mm(jnp.tanh(wd), wup_ref[d], passes=3))
        softplus = jnp.maximum(u, 0.0) + jnp.log(1.0 + jnp.exp(-jnp.abs(u)))
        w_log = -softplus - 0.5
        lw = -jnp.exp(w_log)
        a = _sigmoid(a0_ref[d:d + 1, :] + _mm(ad, aup_ref[d], passes=3))
        kd = k * (1.0 + (a - 1.0) * ka_ref[...])
        lw_out[d, 0] = lw
        kd_out[d, 0] = kd
        nb_out[d, 0] = -(kk * a)
        kd_sum = kd if kd_sum is None else kd_sum + kd
    bonus_out[0] = _mm_exact_rhs(r * kd_sum * rk_ref[...], ones) * v


def _prep_call(za, lp, ones_a, ctx_len):
    B, S, _ = za.shape
    tt = TOK_TILE
    n_tiles = S // tt
    hb = tt // 8
    n8 = S // 8
    full = lambda shape: pl.BlockSpec(shape, lambda b, t: (0,) * len(shape))
    o3 = jax.ShapeDtypeStruct((B, S, BW), F32)
    o4 = jax.ShapeDtypeStruct((2, B, S, BW), F32)
    s3 = pl.BlockSpec((1, tt, BW), lambda b, t: (b, t, 0))
    s4 = pl.BlockSpec((2, 1, tt, BW), lambda b, t: (0, b, t, 0))
    return pl.pallas_call(
        functools.partial(_prep_kernel, ctx_len // tt, n_tiles),
        out_shape=(o3, o3, o3, o4, o4, o4, o3),
        grid=(B, n_tiles),
        in_specs=[
            pl.BlockSpec((1, tt, A_COLS), lambda b, t: (b, t, 0)),
            pl.BlockSpec((1, 8, A_COLS), lambda b, t: (b, jnp.maximum(t * hb - 1, 0), 0)),
            pl.BlockSpec((1, 8, A_COLS), lambda b, t: (b, jnp.minimum((t + 1) * hb, n8 - 1), 0)),
            full((1, A_COLS)), full((1, A_COLS)),
            full((2, BW)), full((2, A_LORA, BW)), full((2, BW)), full((2, A_LORA, BW)),
            full((1, BW)), full((1, BW)), full((1, BW)), full((BW, BW)),
        ],
        out_specs=(s3, s3, s3, s4, s4, s4, s3),
        scratch_shapes=[pltpu.VMEM((tt + 16, A_COLS), F32)],
        compiler_params=_params(("parallel", "arbitrary")),
        name="prep",
    )(za, za, za, lp["mu_prev"], lp["mu_next"], lp["w0"], lp["w_up"], lp["a0"], lp["a_up"],
      lp["k_k"], lp["k_a"], lp["r_k"], ones_a)


def _scan_kernel(ctx_chunks, passes,
                 r_ref, v_ref, kk_ref, lw_ref, kd_ref, nb_ref, y_ref, state_ref):
    d = pl.program_id(1)
    c = pl.program_id(2)
    C = CHUNK
    fwd = d == 0

    @pl.when(c == 0)
    def _():
        state_ref[...] = jnp.zeros_like(state_ref)

    rows = lax.broadcasted_iota(jnp.int32, (C, C), 0)
    cols = lax.broadcasted_iota(jnp.int32, (C, C), 1)
    order = (1 - 2 * d) * (rows - cols)
    incl = order >= 0
    strict = order > 0
    eye = (rows == cols).astype(F32)

    lw = lw_ref[0, 0]
    tri = incl.astype(BF16)
    rem = lw
    cum = None
    for _ in range(3):
        part = rem.astype(BF16)
        term = _dot(tri, part)
        cum = term if cum is None else cum + term
        rem = rem - part.astype(F32)
    tot = jnp.where(fwd, cum[C - 1:C, :], cum[0:1, :])
    p_in = jnp.exp(cum)
    p_ex = jnp.exp(cum - lw)
    p_inv = jnp.exp(-cum)
    p_tot = jnp.exp(tot - cum)
    e_tot = jnp.exp(tot)

    r = r_ref[0]
    v = v_ref[0]
    nb = nb_ref[0, 0]
    kd = kd_ref[0, 0]
    rt = r * p_in
    kt = kk_ref[0] * p_ex
    nbt = nb * p_inv
    kdt = kd * p_inv
    nbp = nb * p_tot
    kdp = kd * p_tot
    v_t = v.T

    for h in range(A_HEADS):
        sl = slice(h * A_HEAD, (h + 1) * A_HEAD)
        lhs = jnp.concatenate([rt[:, sl], kt[:, sl]], axis=0)
        rhs = jnp.concatenate([nbt[:, sl], kdt[:, sl]], axis=0)
        qk = _mm(lhs, rhs, NT, passes)
        rb = jnp.where(incl, qk[0:C, 0:C], 0.0)
        rk = jnp.where(incl, qk[0:C, C:2 * C], 0.0)
        n = jnp.where(strict, qk[C:2 * C, 0:C], 0.0)
        a = jnp.where(strict, qk[C:2 * C, C:2 * C], 0.0)
        t_inv = eye + n
        pw = n
        for _ in range(5):
            pw = _mm(pw, pw, NN, passes)
            t_inv = t_inv + _mm(t_inv, pw, NN, passes)
        w = _mm(t_inv, kt[:, sl], NN, passes)
        ta = _mm(t_inv, a, NN, passes)
        s0 = state_ref[h]
        vt_h = v_t[sl, :]
        q = _mm(s0, w, NT, passes) + _mm(vt_h, ta, NT, passes)
        y = (_mm(rt[:, sl], s0, NT, passes) + _mm(rb, q, NT, passes)
             + _mm(rk, vt_h, NT, passes))
        state_ref[h] = (s0 * e_tot[:, sl] + _mm(q, nbp[:, sl], NN, passes)
                        + _mm(vt_h, kdp[:, sl], NN, passes))
        y_ref[0, 0, :, sl] = y


def _scan_call(r, v, kk, lw, kd, nb, ctx_len, passes):
    B, S, _ = r.shape
    C = CHUNK
    nc = S // C
    lc = ctx_len // C

    def ci(d, c):
        fwd_i = c
        rev_i = jnp.where(c < lc, lc - 1 - c, nc - 1 - (c - lc))
        return jnp.where(d == 0, fwd_i, rev_i)

    s3 = pl.BlockSpec((1, C, BW), lambda b, d, c: (b, ci(d, c), 0))
    s4 = pl.BlockSpec((1, 1, C, BW), lambda b, d, c: (d, b, ci(d, c), 0))
    return pl.pallas_call(
        functools.partial(_scan_kernel, lc, passes),
        out_shape=jax.ShapeDtypeStruct((2, B, S, BW), F32),
        grid=(B, 2, nc),
        in_specs=[s3, s3, s3, s4, s4, s4],
        out_specs=s4,
        scratch_shapes=[pltpu.VMEM((A_HEADS, A_HEAD, A_HEAD), F32)],
        compiler_params=_params(("parallel", "arbitrary", "arbitrary")),
        name="scan",
    )(r, v, kk, lw, kd, nb)


def _rope(x, cos, s_plus, s_minus, shift):
    return (x * cos + pltpu.roll(x, shift, 1) * s_plus
            + pltpu.roll(x, LANE - shift, 1) * s_minus)


def _bcproj_kernel(zb_ref, zc_ref, tb_ref, tc_ref, qln_ref, kvln_ref, wuq_ref, wuk_ref, wuv_ref,
                   bqg_ref, bkg_ref, cqg_ref, ckg_ref,
                   qb_out, kb_out, vb_out, qc_out, kc_out, vc_out):
    zb = zb_ref[0]
    cq = zb[:, 0:B_Q_LORA]
    ckv = zb[:, B_Q_LORA:B_Q_LORA + B_KV_LORA]
    kr_slab = zb[:, B_Q_LORA + B_KV_LORA:B_COLS_P]
    cqn = cq * lax.rsqrt(jnp.mean(cq * cq, axis=-1, keepdims=True) + NORM_EPS) * qln_ref[...]
    ckvn = ckv * lax.rsqrt(jnp.mean(ckv * ckv, axis=-1, keepdims=True) + NORM_EPS) * kvln_ref[...]
    q_all = _dot(cqn.astype(BF16), wuq_ref[...])
    k_all = _dot(ckvn.astype(BF16), wuk_ref[...])
    vb_out[0] = _dot(ckvn.astype(BF16), wuv_ref[...]).astype(BF16)
    cos_b, sp_b, sm_b = tb_ref[0], tb_ref[1], tb_ref[2]
    q_scale = B_QK ** -0.5
    for h in range(B_HEADS):
        sl = slice(h * LANE, (h + 1) * LANE)
        q = q_all[:, sl]
        q = q * lax.rsqrt(jnp.sum(q * q, axis=-1, keepdims=True) * (1.0 / B_QK) + NORM_EPS)
        q = _rope(q * bqg_ref[...], cos_b, sp_b, sm_b, B_ROPE // 4)
        qb_out[0, :, sl] = (q * q_scale).astype(BF16)
        k = k_all[:, sl] + kr_slab
        k = k * lax.rsqrt(jnp.sum(k * k, axis=-1, keepdims=True) * (1.0 / B_QK) + NORM_EPS)
        k = _rope(k * bkg_ref[...], cos_b, sp_b, sm_b, B_ROPE // 4)
        kb_out[0, :, sl] = k.astype(BF16)

    zc = zc_ref[0]
    cos_c, sp_c, sm_c = tc_ref[0], tc_ref[1], tc_ref[2]
    lane = lax.broadcasted_iota(jnp.int32, (1, LANE), 1)
    lo = lane < C_HEAD

    def head_norm(x, gain):
        x2 = x * x
        s_lo = jnp.sum(jnp.where(lo, x2, 0.0), axis=-1, keepdims=True)
        s_hi = jnp.sum(jnp.where(lo, 0.0, x2), axis=-1, keepdims=True)
        ms = jnp.where(lo, s_lo, s_hi) * (1.0 / C_HEAD)
        return _rope(x * lax.rsqrt(ms + NORM_EPS) * gain, cos_c, sp_c, sm_c, C_HEAD // 4)

    c_scale = C_HEAD ** -0.5
    for j in range(C_HEADS // 2):
        x = head_norm(zc[:, j * LANE:(j + 1) * LANE], cqg_ref[...]) * c_scale
        xr = pltpu.roll(x, C_HEAD, 1)
        g = (2 * j) // C_GROUP
        for half in range(2):
            h = 2 * j + half
            src = x if half == g else xr
            keep = lo if g == 0 else jnp.logical_not(lo)
            qc_out[0, :, h * LANE:(h + 1) * LANE] = jnp.where(keep, src, 0.0).astype(BF16)
    kc_out[0] = head_norm(zc[:, BW:BW + C_KV_W], ckg_ref[...]).astype(BF16)
    vv = zc[:, BW + C_KV_W:BW + 2 * C_KV_W]
    vr = pltpu.roll(vv, C_HEAD, 1)
    vc_out[0, :, 0:LANE] = jnp.where(lo, vv, vr).astype(BF16)
    vc_out[0, :, LANE:2 * LANE] = jnp.where(lo, vr, vv).astype(BF16)


def _bcproj_call(zb, zc, tab_b, tab_c, lp):
    B, S, _ = zb.shape
    tt = TOK_TILE
    full = lambda shape: pl.BlockSpec(shape, lambda b, t: (0,) * len(shape))
    tok = lambda w: pl.BlockSpec((1, tt, w), lambda b, t: (b, t, 0))
    tab = pl.BlockSpec((3, tt, LANE), lambda b, t: (0, t, 0))
    shp = lambda w: jax.ShapeDtypeStruct((B, S, w), BF16)
    return pl.pallas_call(
        _bcproj_kernel,
        out_shape=(shp(B_HEADS * LANE), shp(B_HEADS * LANE), shp(BW),
                   shp(C_HEADS * LANE), shp(C_KV_W), shp(2 * LANE)),
        grid=(B, S // tt),
        in_specs=[tok(B_COLS_P), tok(C_COLS), tab, tab,
                  full((1, B_Q_LORA)), full((1, B_KV_LORA)),
                  full((B_Q_LORA, B_HEADS * LANE)), full((B_KV_LORA, B_HEADS * LANE)),
                  full((B_KV_LORA, BW)),
                  full((1, LANE)), full((1, LANE)), full((1, LANE)), full((1, LANE))],
        out_specs=(tok(B_HEADS * LANE), tok(B_HEADS * LANE), tok(BW),
                   tok(C_HEADS * LANE), tok(C_KV_W), tok(2 * LANE)),
        compiler_params=_params(("parallel", "arbitrary")),
        name="bcproj",
    )(zb, zc, tab_b, tab_c, lp["q_ln"], lp["kv_ln"], lp["w_uq"], lp["w_uk"], lp["w_uv"],
      lp["b_qn"], lp["b_kn"], lp["c_qn"], lp["c_kn"])


def _attb_kernel(tile0, ctx_tiles, ctx_len, q_ref, k_ref, v_ref, o_ref):
    t = pl.program_id(1) + tile0
    lane = lax.broadcasted_iota(jnp.int32, (1, LANE), 1)
    lo = lane < B_V

    def run(n_keys):
        for j in range(B_HEADS // 2):
            outs = []
            for h in (2 * j, 2 * j + 1):
                sl = slice(h * LANE, (h + 1) * LANE)
                s = _dot(q_ref[0, :, sl], k_ref[0, 0:n_keys, sl], NT)
                m = jnp.max(s, axis=-1, keepdims=True)
                p = jnp.exp(s - m)
                den = jnp.sum(p, axis=-1, keepdims=True)
                o = _dot(p.astype(BF16), v_ref[0, 0:n_keys, j * LANE:(j + 1) * LANE])
                outs.append(o / den)
            o_ref[0, :, j * LANE:(j + 1) * LANE] = jnp.where(lo, outs[0], outs[1])

    if tile0 < ctx_tiles:
        @pl.when(t < ctx_tiles)
        def _():
            run(ctx_len)

        @pl.when(t >= ctx_tiles)
        def _():
            run(k_ref.shape[1])
    else:
        run(k_ref.shape[1])


def _attb_call(qb, kb, vb, ctx_len, with_ctx):
    B, S, _ = qb.shape
    tq = TOK_TILE
    ctx_tiles = ctx_len // tq
    tile0 = 0 if with_ctx else ctx_tiles
    n_tiles = S // tq - tile0
    return pl.pallas_call(
        functools.partial(_attb_kernel, tile0, ctx_tiles, ctx_len),
        out_shape=jax.ShapeDtypeStruct((B, n_tiles * tq, BW), F32),
        grid=(B, n_tiles),
        in_specs=[
            pl.BlockSpec((1, tq, B_HEADS * LANE), lambda b, t: (b, t + tile0, 0)),
            pl.BlockSpec((1, S, B_HEADS * LANE), lambda b, t: (b, 0, 0)),
            pl.BlockSpec((1, S, BW), lambda b, t: (b, 0, 0)),
        ],
        out_specs=pl.BlockSpec((1, tq, BW), lambda b, t: (b, t, 0)),
        compiler_params=_params(("parallel", "arbitrary")),
        name="attb",
    )(qb, kb, vb)


def _attc_kernel(tile0, ctx_len, n_lat, sink_ref, q_ref, k_ref, v_ref, o_ref):
    W = WINDOW
    ctx_tiles = ctx_len // W
    n_blk = n_lat // W
    t = pl.program_id(1) + tile0
    lane = lax.broadcasted_iota(jnp.int32, (1, LANE), 1)
    lo = lane < C_HEAD
    rows4 = lax.broadcasted_iota(jnp.int32, (C_GROUP * W, 1), 0)

    def attend(k_cat, v_cat, mask):
        for g in range(C_KV_HEADS):
            q_st = jnp.concatenate(
                [q_ref[0, :, h * LANE:(h + 1) * LANE] for h in range(g * C_GROUP, (g + 1) * C_GROUP)],
                axis=0)
            s = _dot(q_st, k_cat, NT)
            if mask is not None:
                s = jnp.where(mask, s, NEG)
            sk = jnp.zeros((C_GROUP * W, 1), F32)
            for i in range(C_GROUP):
                sk = jnp.where(rows4 // W == i, sink_ref[g * C_GROUP + i], sk)
            m = jnp.maximum(jnp.max(s, axis=-1, keepdims=True), sk)
            p = jnp.exp(s - m)
            den = jnp.sum(p, axis=-1, keepdims=True) + jnp.exp(sk - m)
            o = _dot(p.astype(BF16), v_cat[:, g * LANE:(g + 1) * LANE]) / den
            for jj in range(C_GROUP // 2):
                o_lo = o[(2 * jj) * W:(2 * jj + 1) * W]
                o_hi = o[(2 * jj + 1) * W:(2 * jj + 2) * W]
                col = (g * C_GROUP // 2 + jj) * LANE
                o_ref[0, :, col:col + LANE] = jnp.where(lo, o_lo, o_hi)

    def latent():
        j = t - ctx_tiles
        bm = jnp.maximum(j - 1, 0)
        bp = jnp.minimum(j + 1, n_blk - 1)

        def blk(ref, b):
            return ref[0, pl.ds(pl.multiple_of(ctx_len + b * W, W), W), :]

        k_cat = jnp.concatenate([k_ref[0, 0:ctx_len, :], blk(k_ref, bm), blk(k_ref, j), blk(k_ref, bp)], axis=0)
        v_cat = jnp.concatenate([v_ref[0, 0:ctx_len, :], blk(v_ref, bm), blk(v_ref, j), blk(v_ref, bp)], axis=0)
        n = ctx_len + 3 * W
        qi = lax.broadcasted_iota(jnp.int32, (C_GROUP * W, n), 0) % W
        col = lax.broadcasted_iota(jnp.int32, (C_GROUP * W, n), 1)
        cb = col - ctx_len
        kpos = (j - 1) * W + cb
        diff = cb - W - qi
        band = (jnp.abs(diff) <= WINDOW) & (kpos >= 0) & (kpos < n_lat)
        attend(k_cat, v_cat, (col < ctx_len) | band)

    def context():
        attend(k_ref[0, 0:ctx_len, :], v_ref[0, 0:ctx_len, :], None)

    if tile0 < ctx_tiles:
        pl.when(t < ctx_tiles)(context)
        pl.when(t >= ctx_tiles)(latent)
    else:
        latent()


def _attc_call(qc, kc, vc, sink, ctx_len, with_ctx):
    B, S, _ = qc.shape
    W = WINDOW
    ctx_tiles = ctx_len // W
    tile0 = 0 if with_ctx else ctx_tiles
    n_tiles = S // W - tile0
    return pl.pallas_call(
        functools.partial(_attc_kernel, tile0, ctx_len, S - ctx_len),
        out_shape=jax.ShapeDtypeStruct((B, n_tiles * W, BW), F32),
        grid=(B, n_tiles),
        in_specs=[
            pl.BlockSpec(memory_space=pltpu.SMEM),
            pl.BlockSpec((1, W, C_HEADS * LANE), lambda b, t: (b, t + tile0, 0)),
            pl.BlockSpec((1, S, C_KV_W), lambda b, t: (b, 0, 0)),
            pl.BlockSpec((1, S, 2 * LANE), lambda b, t: (b, 0, 0)),
        ],
        out_specs=pl.BlockSpec((1, W, BW), lambda b, t: (b, t, 0)),
        compiler_params=_params(("parallel", "arbitrary")),
        name="attc",
    )(sink, qc, kc, vc)


def _merge_kernel(y0_ref, y1_ref, bonus_ref, ga_ref, yb_ref, gb_ref, yc_ref, gc_ref, zg_ref,
                  x_ref, gate_ref, gng_ref, gnb_ref, ones_ref, wbo_ref, wout_ref, o_ref):
    ones = ones_ref[...]
    y = y0_ref[0, 0] + y1_ref[0, 0]
    mu = _mm_exact_rhs(y, ones) * (1.0 / A_HEAD)
    dlt = y - mu
    var = _mm_exact_rhs(dlt * dlt, ones) * (1.0 / A_HEAD)
    ya = dlt * lax.rsqrt(var + A_GN_EPS) * gng_ref[...] + gnb_ref[...] + bonus_ref[0]
    m = None
    for n, (y_n, g_ref) in enumerate(((ya, ga_ref), (yb_ref[0], gb_ref), (yc_ref[0], gc_ref))):
        u = (y_n * _silu(g_ref[0])).astype(BF16)
        gate_n = _sigmoid(zg_ref[0, :, n * D_MODEL:(n + 1) * D_MODEL])
        term = gate_n * _dot(u, wbo_ref[n])
        m = term if m is None else m + term
    o_ref[0] = x_ref[0] + gate_ref[0] * _dot(m.astype(BF16), wout_ref[...])


def _merge_call(y, bonus, ga, yb, gb, yc, gc, zg, xall, modl, lp, ones_a, ctx_len, with_ctx):
    B, S, _ = xall.shape
    tt = TOK_TILE
    ctx_tiles = ctx_len // tt
    tile0 = 0 if with_ctx else ctx_tiles
    n_tiles = S // tt - tile0
    row = _mod_row(ctx_tiles, B)
    mod3 = modl.reshape(modl.shape[0], 1, 3 * D_MODEL)
    full = lambda shape: pl.BlockSpec(shape, lambda b, t: (0,) * len(shape))
    tok = lambda w: pl.BlockSpec((1, tt, w), lambda b, t: (b, t + tile0, 0))
    ydir = lambda d: pl.BlockSpec((1, 1, tt, BW), lambda b, t: (d, b, t + tile0, 0))
    att = lambda w: pl.BlockSpec((1, tt, w), lambda b, t: (b, t, 0))
    return pl.pallas_call(
        _merge_kernel,
        out_shape=jax.ShapeDtypeStruct((B, n_tiles * tt, D_MODEL), F32),
        grid=(B, n_tiles),
        in_specs=[ydir(0), ydir(1), tok(BW), tok(BW), att(BW), tok(BW), att(BW), tok(BW),
                  tok(N_BRANCH * D_MODEL), tok(D_MODEL),
                  pl.BlockSpec((1, 1, D_MODEL), lambda b, t: (row(b, t + tile0), 0, 2)),
                  full((1, BW)), full((1, BW)), full((BW, BW)),
                  full((N_BRANCH, BW, D_MODEL)), full((D_MODEL, D_MODEL))],
        out_specs=pl.BlockSpec((1, tt, D_MODEL), lambda b, t: (b, t, 0)),
        compiler_params=_params(("parallel", "arbitrary")),
        name="merge",
    )(y, y, bonus, ga, yb, gb, yc, gc, zg, xall, mod3, lp["gn_g"], lp["gn_b"], ones_a,
      lp["w_bo"], lp["w_out"])


def _rope_tables(n_ctx, n_lat, rot_dim, head_w, lane0):
    rows = n_lat // GRID_W
    row = jnp.repeat(jnp.arange(rows), GRID_W).astype(F32)
    colp = jnp.tile(jnp.arange(GRID_W), rows).astype(F32)
    q = rot_dim // 4
    inv = ROPE_THETA ** (-(2.0 * jnp.arange(q, dtype=F32)) / (rot_dim // 2))
    ang = jnp.concatenate([row[:, None] * inv, colp[:, None] * inv], axis=-1)
    cos, sin = jnp.cos(ang), jnp.sin(ang)
    zero = jnp.zeros_like(sin[:, :q])
    cos_g = jnp.concatenate([cos[:, :q], cos[:, :q], cos[:, q:], cos[:, q:]], axis=-1)
    sp_g = jnp.concatenate([zero, sin[:, :q], zero, sin[:, q:]], axis=-1)
    sm_g = jnp.concatenate([-sin[:, :q], zero, -sin[:, q:], zero], axis=-1)

    def place(t, fill):
        pad_l = jnp.full((n_lat, lane0), fill, F32)
        pad_r = jnp.full((n_lat, head_w - lane0 - rot_dim), fill, F32)
        grp = jnp.concatenate([pad_l, t, pad_r], axis=-1)
        lat = jnp.tile(grp, (1, LANE // head_w))
        return jnp.concatenate([jnp.full((n_ctx, LANE), fill, F32), lat], axis=0)

    return jnp.stack([place(cos_g, 1.0), place(sp_g, 0.0), place(sm_g, 0.0)])


def _block_ones(n, blk):
    i = np.arange(n) // blk
    return jnp.asarray(i[:, None] == i[None, :], dtype=BF16)


def _pad_lanes(g, width):
    return jnp.pad(g, (0, width - g.shape[0])).reshape(1, width)


def _layer_params(i, w_in, a_mu_prev, a_mu_next, a_w0, a_w_up, a_a0, a_a_up, a_k_k, a_k_a, a_r_k,
                  a_gn_g, a_gn_b, b_q_ln, b_kv_ln, b_w_uq, b_w_ukv, b_qn_g, b_kn_g,
                  c_qn_g, c_kn_g, c_sink, w_branch_out, w_out):
    w = w_in[i]
    c0 = 0
    segs = {}
    for name, width in (("za", A_COLS), ("ga", BW), ("zb", B_COLS), ("gb", BW),
                        ("zc", C_COLS), ("gc", BW), ("zg", N_BRANCH * D_MODEL)):
        segs[name] = w[:, c0:c0 + width]
        c0 += width
    zb = segs["zb"]
    zpad = lambda n: jnp.zeros((D_MODEL, n), w.dtype)
    segs["zb"] = jnp.concatenate(
        [zb[:, :B_Q_LORA + B_KV_LORA], zpad(B_NOPE), zb[:, B_Q_LORA + B_KV_LORA:], zpad(LANE - B_NOPE - B_ROPE)],
        axis=-1)
    w_in_p = jnp.concatenate([segs[n] for n, _ in IN_SEGS], axis=-1).astype(BF16)

    uq = b_w_uq[i].reshape(B_Q_LORA, B_HEADS, B_QK)
    uq = jnp.pad(uq, ((0, 0), (0, 0), (0, LANE - B_QK))).reshape(B_Q_LORA, B_HEADS * LANE)
    ukv = b_w_ukv[i].reshape(B_KV_LORA, B_HEADS, B_NOPE + B_V)
    uk = jnp.pad(ukv[:, :, :B_NOPE], ((0, 0), (0, 0), (0, LANE - B_NOPE))).reshape(B_KV_LORA, B_HEADS * LANE)
    uv = ukv[:, :, B_NOPE:].reshape(B_KV_LORA, BW)
    row = lambda t: t.reshape(1, -1)
    return dict(
        w_in=w_in_p,
        mu_prev=row(a_mu_prev[i]), mu_next=row(a_mu_next[i]),
        w0=a_w0[i], w_up=a_w_up[i], a0=a_a0[i], a_up=a_a_up[i],
        k_k=row(a_k_k[i]), k_a=row(a_k_a[i]), r_k=row(a_r_k[i]),
        gn_g=row(a_gn_g[i]), gn_b=row(a_gn_b[i]),
        q_ln=row(b_q_ln[i]), kv_ln=row(b_kv_ln[i]),
        w_uq=uq.astype(BF16), w_uk=uk.astype(BF16), w_uv=uv.astype(BF16),
        b_qn=_pad_lanes(b_qn_g[i], LANE), b_kn=_pad_lanes(b_kn_g[i], LANE),
        c_qn=jnp.tile(c_qn_g[i], 2).reshape(1, LANE), c_kn=jnp.tile(c_kn_g[i], 2).reshape(1, LANE),
        sink=c_sink[i],
        w_bo=w_branch_out[i].astype(BF16), w_out=w_out[i].astype(BF16),
    )


SCAN_PASSES = 3


def kernel(x, c, ctx, c_ctx, ada_w, ada_b, norm_g, w_in, a_mu_prev, a_mu_next, a_w0, a_w_up, a_a0, a_a_up, a_k_k, a_k_a, a_r_k, a_gn_g, a_gn_b, b_q_ln, b_kv_ln, b_w_uq, b_w_ukv, b_qn_g, b_kn_g, c_qn_g, c_kn_g, c_sink, w_branch_out, w_out):
    B, T, D = x.shape
    L = ctx.shape[1]
    depth = ada_w.shape[0]
    assert D == D_MODEL and L % TOK_TILE == 0 and T % TOK_TILE == 0 and T % GRID_W == 0

    mod_rows = -(-(B + 1) // 8) * 8
    cs = jnp.concatenate([c, c_ctx[None, :], jnp.zeros((mod_rows - B - 1, D), c.dtype)], axis=0)
    mod = _mod_call(cs, ada_w, ada_b)

    tab_b = _rope_tables(L, T, B_ROPE, LANE, B_NOPE)
    tab_c = _rope_tables(L, T, C_HEAD, C_HEAD, 0)
    ones_a = _block_ones(BW, A_HEAD)

    xall = jnp.concatenate([ctx, x], axis=1)
    for i in range(depth):
        with_ctx = i < depth - 1
        lp = _layer_params(i, w_in, a_mu_prev, a_mu_next, a_w0, a_w_up, a_a0, a_a_up, a_k_k, a_k_a,
                           a_r_k, a_gn_g, a_gn_b, b_q_ln, b_kv_ln, b_w_uq, b_w_ukv, b_qn_g, b_kn_g,
                           c_qn_g, c_kn_g, c_sink, w_branch_out, w_out)
        za, ga, zb, gb, zc, gc, zg = _inproj_call(xall, mod[i], norm_g[i], lp["w_in"], L)
        r, v, kk, lw, kd, nb, bonus = _prep_call(za, lp, ones_a, L)
        y = _scan_call(r, v, kk, lw, kd, nb, L, SCAN_PASSES)
        qb, kb, vb, qc, kc, vc = _bcproj_call(zb, zc, tab_b, tab_c, lp)
        yb = _attb_call(qb, kb, vb, L, with_ctx)
        yc = _attc_call(qc, kc, vc, lp["sink"], L, with_ctx)
        xall = _merge_call(y, bonus, ga, yb, gb, yc, gc, zg, xall, mod[i], lp, ones_a, L, with_ctx)
    return xall
```

```python
import functools

import numpy as np
import jax
import jax.numpy as jnp
from jax import lax
from jax.experimental import pallas as pl
from jax.experimental.pallas import tpu as pltpu

F32 = jnp.float32
BF16 = jnp.bfloat16

D_MODEL = 1024
GRID_W = 64
ROPE_THETA = 10000.0
NORM_EPS = 1e-6
NEG = -1e30
BW = 512
N_BRANCH = 3

A_HEAD = 64
A_HEADS = BW // A_HEAD
A_LORA = 64
A_GN_EPS = 64e-5
A_COLS = 3 * BW + 4 * A_LORA

B_HEADS = 8
B_NOPE = 64
B_ROPE = 32
B_QK = B_NOPE + B_ROPE
B_V = BW // B_HEADS
B_Q_LORA = 256
B_KV_LORA = 128
B_COLS = B_Q_LORA + B_KV_LORA + B_ROPE
B_COLS_P = 512

C_HEAD = 64
C_HEADS = BW // C_HEAD
C_KV_HEADS = 2
C_GROUP = C_HEADS // C_KV_HEADS
C_KV_W = C_KV_HEADS * C_HEAD
C_COLS = BW + 2 * C_KV_W
WINDOW = 128

LANE = 128
CHUNK = 64
TOK_TILE = 256
VMEM_LIMIT = 56 * 1024 * 1024

IN_SEGS = (("za", A_COLS), ("ga", BW), ("zb", B_COLS_P), ("gb", BW),
           ("zc", C_COLS), ("gc", BW), ("zg", N_BRANCH * D_MODEL))
N_IN_P = sum(w for _, w in IN_SEGS)

NN = (((1,), (0,)), ((), ()))
NT = (((1,), (1,)), ((), ()))


def _dot(a, b, dims=NN):
    return lax.dot_general(a, b, dims, preferred_element_type=F32)


def _split(x):
    hi = x.astype(BF16)
    lo = (x - hi.astype(F32)).astype(BF16)
    return hi, lo


def _mm(a, b, dims=NN, passes=1):
    if passes == 1:
        return _dot(a.astype(BF16), b.astype(BF16), dims)
    ah, al = _split(a)
    bh, bl = _split(b)
    return _dot(ah, bh, dims) + (_dot(ah, bl, dims) + _dot(al, bh, dims))


def _mm_exact_rhs(a, b_bf16, parts=2):
    acc = None
    rem = a
    for _ in range(parts):
        p = rem.astype(BF16)
        t = _dot(p, b_bf16)
        acc = t if acc is None else acc + t
        rem = rem - p.astype(F32)
    return acc


def _mm_exact_lhs(a_bf16, b, parts=3):
    acc = None
    rem = b
    for _ in range(parts):
        p = rem.astype(BF16)
        t = _dot(a_bf16, p)
        acc = t if acc is None else acc + t
        rem = rem - p.astype(F32)
    return acc


def _sigmoid(x):
    return 1.0 / (1.0 + jnp.exp(-x))


def _silu(x):
    return x * _sigmoid(x)


def _params(sem):
    return pltpu.CompilerParams(dimension_semantics=sem, vmem_limit_bytes=VMEM_LIMIT)


def _mod_kernel(c_ref, w_ref, b_ref, o_ref):
    s = _silu(c_ref[...])
    o_ref[0] = _mm(s, w_ref[0], passes=3) + b_ref[0]


def _mod_call(cs, ada_w, ada_b):
    n_layers = ada_w.shape[0]
    rows = cs.shape[0]
    return pl.pallas_call(
        _mod_kernel,
        out_shape=jax.ShapeDtypeStruct((n_layers, rows, 3 * D_MODEL), F32),
        grid=(n_layers, 3),
        in_specs=[
            pl.BlockSpec((rows, D_MODEL), lambda l, n: (0, 0)),
            pl.BlockSpec((1, D_MODEL, D_MODEL), lambda l, n: (l, 0, n)),
            pl.BlockSpec((1, 1, D_MODEL), lambda l, n: (l, 0, n)),
        ],
        out_specs=pl.BlockSpec((1, rows, D_MODEL), lambda l, n: (l, 0, n)),
        compiler_params=_params(("arbitrary", "arbitrary")),
        name="mod",
    )(cs, ada_w, ada_b.reshape(n_layers, 1, 3 * D_MODEL))


def _inproj_kernel(x_ref, shift_ref, scale_ref, g_ref, w_ref, *o_refs):
    x = x_ref[0]
    ms = jnp.mean(x * x, axis=-1, keepdims=True)
    h = x * lax.rsqrt(ms + NORM_EPS) * g_ref[...]
    h = h * (1.0 + scale_ref[0]) + shift_ref[0]
    hb = h.astype(BF16)
    col = 0
    for (_, width), o_ref in zip(IN_SEGS, o_refs):
        step = 512 if width % 512 == 0 else 256
        for c0 in range(0, width, step):
            o_ref[0, :, c0:c0 + step] = _dot(hb, w_ref[:, col + c0:col + c0 + step])
        col += width


def _mod_row(ctx_tiles, n_batch_rows):
    return lambda b, t: jnp.where(t < ctx_tiles, n_batch_rows, b)


def _inproj_call(xall, modl, norm_g, w_in_p, ctx_len):
    B, S, _ = xall.shape
    tm = TOK_TILE
    row = _mod_row(ctx_len // tm, B)
    mod3 = modl.reshape(modl.shape[0], 1, 3 * D_MODEL)
    outs = tuple(jax.ShapeDtypeStruct((B, S, w), F32) for _, w in IN_SEGS)
    return pl.pallas_call(
        _inproj_kernel,
        out_shape=outs,
        grid=(B, S // tm),
        in_specs=[
            pl.BlockSpec((1, tm, D_MODEL), lambda b, t: (b, t, 0)),
            pl.BlockSpec((1, 1, D_MODEL), lambda b, t: (row(b, t), 0, 0)),
            pl.BlockSpec((1, 1, D_MODEL), lambda b, t: (row(b, t), 0, 1)),
            pl.BlockSpec((1, D_MODEL), lambda b, t: (0, 0)),
            pl.BlockSpec((D_MODEL, N_IN_P), lambda b, t: (0, 0),
                         pipeline_mode=pl.Buffered(1)),
        ],
        out_specs=tuple(pl.BlockSpec((1, tm, w), lambda b, t: (b, t, 0)) for _, w in IN_SEGS),
        compiler_params=_params(("parallel", "arbitrary")),
        name="inproj",
    )(xall, mod3, mod3, norm_g.reshape(1, D_MODEL), w_in_p)


def _prep_kernel(ctx_tiles, n_tiles,
                 z_ref, zp_ref, zn_ref, mup_ref, mun_ref, w0_ref, wup_ref, a0_ref, aup_ref,
                 kk_ref, ka_ref, rk_ref, ones_ref,
                 r_out, v_out, kk_out, lw_out, kd_out, nb_out, bonus_out, zs_ref):
    t = pl.program_id(1)
    tt = z_ref.shape[1]
    first = jnp.logical_or(t == 0, t == ctx_tiles)
    last = jnp.logical_or(t == ctx_tiles - 1, t == n_tiles - 1)
    z = z_ref[0]
    zs_ref[0:8, :] = jnp.where(first, 0.0, zp_ref[0])
    zs_ref[8:8 + tt, :] = z
    zs_ref[8 + tt:16 + tt, :] = jnp.where(last, 0.0, zn_ref[0])
    prev = zs_ref[7:7 + tt, :]
    nxt = zs_ref[9:9 + tt, :]
    zsh = z + mup_ref[...] * (prev - z) + mun_ref[...] * (nxt - z)

    r = zsh[:, 0:BW]
    k = zsh[:, BW:2 * BW]
    v = zsh[:, 2 * BW:3 * BW]
    ones = ones_ref[...]
    kk = k * kk_ref[...]
    ss = _mm_exact_rhs(kk * kk, ones)
    kk = kk * lax.rsqrt(jnp.maximum(ss, 1e-24))
    r_out[0] = r
    v_out[0] = v
    kk_out[0] = kk
    kd_sum = None
    for d in range(2):
        wd = zsh[:, 3 * BW + d * A_LORA:3 * BW + (d + 1) * A_LORA]
        ad = zsh[:, 3 * BW + 2 * A_LORA + d * A_LORA:3 * BW + 2 * A_LORA + (d + 1) * A_LORA]
        u = -(w0_ref[d:d + 1, :] + _mm(jnp.tanh(wd), wup_ref[d], passes=3))
        softplus = jnp.maximum(u, 0.0) + jnp.log(1.0 + jnp.exp(-jnp.abs(u)))
        w_log = -softplus - 0.5
        lw = -jnp.exp(w_log)
        a = _sigmoid(a0_ref[d:d + 1, :] + _mm(ad, aup_ref[d], passes=3))
        kd = k * (1.0 + (a - 1.0) * ka_ref[...])
        lw_out[d, 0] = lw
        kd_out[d, 0] = kd
        nb_out[d, 0] = -(kk * a)
        kd_sum = kd if kd_sum is None else kd_sum + kd
    bonus_out[0] = _mm_exact_rhs(r * kd_sum * rk_ref[...], ones) * v


def _prep_call(za, lp, ones_a, ctx_len):
    B, S, _ = za.shape
    tt = TOK_TILE
    n_tiles = S // tt
    hb = tt // 8
    n8 = S // 8
    full = lambda shape: pl.BlockSpec(shape, lambda b, t: (0,) * len(shape))
    o3 = jax.ShapeDtypeStruct((B, S, BW), F32)
    o4 = jax.ShapeDtypeStruct((2, B, S, BW), F32)
    s3 = pl.BlockSpec((1, tt, BW), lambda b, t: (b, t, 0))
    s4 = pl.BlockSpec((2, 1, tt, BW), lambda b, t: (0, b, t, 0))
    return pl.pallas_call(
        functools.partial(_prep_kernel, ctx_len // tt, n_tiles),
        out_shape=(o3, o3, o3, o4, o4, o4, o3),
        grid=(B, n_tiles),
        in_specs=[
            pl.BlockSpec((1, tt, A_COLS), lambda b, t: (b, t, 0)),
            pl.BlockSpec((1, 8, A_COLS), lambda b, t: (b, jnp.maximum(t * hb - 1, 0), 0)),
            pl.BlockSpec((1, 8, A_COLS), lambda b, t: (b, jnp.minimum((t + 1) * hb, n8 - 1), 0)),
            full((1, A_COLS)), full((1, A_COLS)),
            full((2, BW)), full((2, A_LORA, BW)), full((2, BW)), full((2, A_LORA, BW)),
            full((1, BW)), full((1, BW)), full((1, BW)), full((BW, BW)),
        ],
        out_specs=(s3, s3, s3, s4, s4, s4, s3),
        scratch_shapes=[pltpu.VMEM((tt + 16, A_COLS), F32)],
        compiler_params=_params(("parallel", "arbitrary")),
        name="prep",
    )(za, za, za, lp["mu_prev"], lp["mu_next"], lp["w0"], lp["w_up"], lp["a0"], lp["a_up"],
      lp["k_k"], lp["k_a"], lp["r_k"], ones_a)


def _scan_kernel(passes, fr, fv, fkk, flw, fkd, fnb, br, bv, bkk, blw, bkd, bnb,
                 yf_ref, yb_ref, state_ref):
    C = CHUNK

    @pl.when(pl.program_id(1) == 0)
    def _():
        state_ref[...] = jnp.zeros_like(state_ref)

    rows = lax.broadcasted_iota(jnp.int32, (C, C), 0)
    cols = lax.broadcasted_iota(jnp.int32, (C, C), 1)
    eye = (rows == cols).astype(F32)

    dirs = []
    for d, (r_ref, v_ref, kk_ref, lw_ref, kd_ref, nb_ref) in enumerate(
            ((fr, fv, fkk, flw, fkd, fnb), (br, bv, bkk, blw, bkd, bnb))):
        order = (rows - cols) if d == 0 else (cols - rows)
        incl = order >= 0
        strict = order > 0
        lw = lw_ref[0, 0]
        cum = _mm_exact_lhs(incl.astype(BF16), lw)
        tot = cum[C - 1:C, :] if d == 0 else cum[0:1, :]
        p_inv = jnp.exp(-cum)
        p_tot = jnp.exp(tot - cum)
        nb = nb_ref[0, 0]
        kd = kd_ref[0, 0]
        dirs.append(dict(
            incl=incl, strict=strict,
            rt=r_ref[0] * jnp.exp(cum), kt=kk_ref[0] * jnp.exp(cum - lw),
            nbt=nb * p_inv, kdt=kd * p_inv, nbp=nb * p_tot, kdp=kd * p_tot,
            e_tot=jnp.exp(tot), v_t=v_ref[0].T))

    units = [(d, h) for d in range(2) for h in range(A_HEADS)]
    hs = lambda h: slice(h * A_HEAD, (h + 1) * A_HEAD)

    qk = []
    for d, h in units:
        x = dirs[d]
        lhs = jnp.concatenate([x["rt"][:, hs(h)], x["kt"][:, hs(h)]], axis=0)
        rhs = jnp.concatenate([x["nbt"][:, hs(h)], x["kdt"][:, hs(h)]], axis=0)
        qk.append(_mm(lhs, rhs, NT, passes))
    rb, rk, pw, a, t_inv = [], [], [], [], []
    for (d, h), m in zip(units, qk):
        x = dirs[d]
        rb.append(jnp.where(x["incl"], m[0:C, 0:C], 0.0))
        rk.append(jnp.where(x["incl"], m[0:C, C:2 * C], 0.0))
        n = jnp.where(x["strict"], m[C:2 * C, 0:C], 0.0)
        a.append(jnp.where(x["strict"], m[C:2 * C, C:2 * C], 0.0))
        pw.append(n)
        t_inv.append(eye + n)
    for _ in range(5):
        pw = [_mm(p, p, NN, passes) for p in pw]
        t_inv = [t + _mm(t, p, NN, passes) for t, p in zip(t_inv, pw)]
    w = [_mm(t, dirs[d]["kt"][:, hs(h)], NN, passes) for (d, h), t in zip(units, t_inv)]
    ta = [_mm(t, m, NN, passes) for t, m in zip(t_inv, a)]
    s0 = [state_ref[i] for i in range(len(units))]
    vt = [dirs[d]["v_t"][hs(h), :] for d, h in units]
    q = [_mm(s, wi, NT, passes) + _mm(v, t, NT, passes) for s, wi, v, t in zip(s0, w, vt, ta)]
    for i, (d, h) in enumerate(units):
        x = dirs[d]
        state_ref[i] = (s0[i] * x["e_tot"][:, hs(h)] + _mm(q[i], x["nbp"][:, hs(h)], NN, passes)
                        + _mm(vt[i], x["kdp"][:, hs(h)], NN, passes))
    for i, (d, h) in enumerate(units):
        y = (_mm(dirs[d]["rt"][:, hs(h)], s0[i], NT, passes) + _mm(rb[i], q[i], NT, passes)
             + _mm(rk[i], vt[i], NT, passes))
        (yf_ref if d == 0 else yb_ref)[0, :, hs(h)] = y


def _scan_call(r, v, kk, lw, kd, nb, ctx_len, passes):
    B, S, _ = r.shape
    C = CHUNK
    nc = S // C
    lc = ctx_len // C
    rev = lambda c: jnp.where(c < lc, lc - 1 - c, nc - 1 - (c - lc))
    f3 = pl.BlockSpec((1, C, BW), lambda b, c: (b, c, 0))
    b3 = pl.BlockSpec((1, C, BW), lambda b, c: (b, rev(c), 0))
    f4 = pl.BlockSpec((1, 1, C, BW), lambda b, c: (0, b, c, 0))
    b4 = pl.BlockSpec((1, 1, C, BW), lambda b, c: (1, b, rev(c), 0))
    out = jax.ShapeDtypeStruct((B, S, BW), F32)
    return pl.pallas_call(
        functools.partial(_scan_kernel, passes),
        out_shape=(out, out),
        grid=(B, nc),
        in_specs=[f3, f3, f3, f4, f4, f4, b3, b3, b3, b4, b4, b4],
        out_specs=(f3, b3),
        scratch_shapes=[pltpu.VMEM((2 * A_HEADS, A_HEAD, A_HEAD), F32)],
        compiler_params=_params(("parallel", "arbitrary")),
        name="scan",
    )(r, v, kk, lw, kd, nb, r, v, kk, lw, kd, nb)


def _rope(x, cos, s_plus, s_minus, shift):
    return (x * cos + pltpu.roll(x, shift, 1) * s_plus
            + pltpu.roll(x, LANE - shift, 1) * s_minus)


def _bcproj_kernel(zb_ref, zc_ref, tb_ref, tc_ref, qln_ref, kvln_ref, wuq_ref, wuk_ref, wuv_ref,
                   bqg_ref, bkg_ref, cqg_ref, ckg_ref,
                   qb_out, kb_out, vb_out, qc_out, kc_out, vc_out):
    zb = zb_ref[0]
    cq = zb[:, 0:B_Q_LORA]
    ckv = zb[:, B_Q_LORA:B_Q_LORA + B_KV_LORA]
    kr_slab = zb[:, B_Q_LORA + B_KV_LORA:B_COLS_P]
    cqn = cq * lax.rsqrt(jnp.mean(cq * cq, axis=-1, keepdims=True) + NORM_EPS) * qln_ref[...]
    ckvn = ckv * lax.rsqrt(jnp.mean(ckv * ckv, axis=-1, keepdims=True) + NORM_EPS) * kvln_ref[...]
    q_all = _dot(cqn.astype(BF16), wuq_ref[...])
    k_all = _dot(ckvn.astype(BF16), wuk_ref[...])
    vb_out[0] = _dot(ckvn.astype(BF16), wuv_ref[...]).astype(BF16)
    cos_b, sp_b, sm_b = tb_ref[0], tb_ref[1], tb_ref[2]
    q_scale = B_QK ** -0.5
    for h in range(B_HEADS):
        sl = slice(h * LANE, (h + 1) * LANE)
        q = q_all[:, sl]
        q = q * lax.rsqrt(jnp.sum(q * q, axis=-1, keepdims=True) * (1.0 / B_QK) + NORM_EPS)
        q = _rope(q * bqg_ref[...], cos_b, sp_b, sm_b, B_ROPE // 4)
        qb_out[0, :, sl] = (q * q_scale).astype(BF16)
        k = k_all[:, sl] + kr_slab
        k = k * lax.rsqrt(jnp.sum(k * k, axis=-1, keepdims=True) * (1.0 / B_QK) + NORM_EPS)
        k = _rope(k * bkg_ref[...], cos_b, sp_b, sm_b, B_ROPE // 4)
        kb_out[0, :, sl] = k.astype(BF16)

    zc = zc_ref[0]
    cos_c, sp_c, sm_c = tc_ref[0], tc_ref[1], tc_ref[2]
    lane = lax.broadcasted_iota(jnp.int32, (1, LANE), 1)
    lo = lane < C_HEAD

    def head_norm(x, gain):
        x2 = x * x
        s_lo = jnp.sum(jnp.where(lo, x2, 0.0), axis=-1, keepdims=True)
        s_hi = jnp.sum(jnp.where(lo, 0.0, x2), axis=-1, keepdims=True)
        ms = jnp.where(lo, s_lo, s_hi) * (1.0 / C_HEAD)
        return _rope(x * lax.rsqrt(ms + NORM_EPS) * gain, cos_c, sp_c, sm_c, C_HEAD // 4)

    c_scale = C_HEAD ** -0.5
    for j in range(C_HEADS // 2):
        x = head_norm(zc[:, j * LANE:(j + 1) * LANE], cqg_ref[...]) * c_scale
        xr = pltpu.roll(x, C_HEAD, 1)
        g = (2 * j) // C_GROUP
        for half in range(2):
            h = 2 * j + half
            src = x if half == g else xr
            keep = lo if g == 0 else jnp.logical_not(lo)
            qc_out[0, :, h * LANE:(h + 1) * LANE] = jnp.where(keep, src, 0.0).astype(BF16)
    kc_out[0] = head_norm(zc[:, BW:BW + C_KV_W], ckg_ref[...]).astype(BF16)
    vv = zc[:, BW + C_KV_W:BW + 2 * C_KV_W]
    vr = pltpu.roll(vv, C_HEAD, 1)
    vc_out[0, :, 0:LANE] = jnp.where(lo, vv, vr).astype(BF16)
    vc_out[0, :, LANE:2 * LANE] = jnp.where(lo, vr, vv).astype(BF16)


def _bcproj_call(zb, zc, tab_b, tab_c, lp):
    B, S, _ = zb.shape
    tt = TOK_TILE
    full = lambda shape: pl.BlockSpec(shape, lambda b, t: (0,) * len(shape))
    tok = lambda w: pl.BlockSpec((1, tt, w), lambda b, t: (b, t, 0))
    tab = pl.BlockSpec((3, tt, LANE), lambda b, t: (0, t, 0))
    shp = lambda w: jax.ShapeDtypeStruct((B, S, w), BF16)
    return pl.pallas_call(
        _bcproj_kernel,
        out_shape=(shp(B_HEADS * LANE), shp(B_HEADS * LANE), shp(BW),
                   shp(C_HEADS * LANE), shp(C_KV_W), shp(2 * LANE)),
        grid=(B, S // tt),
        in_specs=[tok(B_COLS_P), tok(C_COLS), tab, tab,
                  full((1, B_Q_LORA)), full((1, B_KV_LORA)),
                  full((B_Q_LORA, B_HEADS * LANE)), full((B_KV_LORA, B_HEADS * LANE)),
                  full((B_KV_LORA, BW)),
                  full((1, LANE)), full((1, LANE)), full((1, LANE)), full((1, LANE))],
        out_specs=(tok(B_HEADS * LANE), tok(B_HEADS * LANE), tok(BW),
                   tok(C_HEADS * LANE), tok(C_KV_W), tok(2 * LANE)),
        compiler_params=_params(("parallel", "arbitrary")),
        name="bcproj",
    )(zb, zc, tab_b, tab_c, lp["q_ln"], lp["kv_ln"], lp["w_uq"], lp["w_uk"], lp["w_uv"],
      lp["b_qn"], lp["b_kn"], lp["c_qn"], lp["c_kn"])


def _attb_kernel(tile0, ctx_tiles, ctx_len, q_ref, k_ref, v_ref, o_ref):
    t = pl.program_id(1) + tile0
    lane = lax.broadcasted_iota(jnp.int32, (1, LANE), 1)
    lo = lane < B_V

    def run(n_keys):
        for j in range(B_HEADS // 2):
            outs = []
            for h in (2 * j, 2 * j + 1):
                sl = slice(h * LANE, (h + 1) * LANE)
                s = _dot(q_ref[0, :, sl], k_ref[0, 0:n_keys, sl], NT)
                m = jnp.max(s, axis=-1, keepdims=True)
                p = jnp.exp(s - m)
                den = jnp.sum(p, axis=-1, keepdims=True)
                o = _dot(p.astype(BF16), v_ref[0, 0:n_keys, j * LANE:(j + 1) * LANE])
                outs.append(o / den)
            o_ref[0, :, j * LANE:(j + 1) * LANE] = jnp.where(lo, outs[0], outs[1])

    if tile0 < ctx_tiles:
        @pl.when(t < ctx_tiles)
        def _():
            run(ctx_len)

        @pl.when(t >= ctx_tiles)
        def _():
            run(k_ref.shape[1])
    else:
        run(k_ref.shape[1])


def _attb_call(qb, kb, vb, ctx_len, with_ctx):
    B, S, _ = qb.shape
    tq = TOK_TILE
    ctx_tiles = ctx_len // tq
    tile0 = 0 if with_ctx else ctx_tiles
    n_tiles = S // tq - tile0
    return pl.pallas_call(
        functools.partial(_attb_kernel, tile0, ctx_tiles, ctx_len),
        out_shape=jax.ShapeDtypeStruct((B, n_tiles * tq, BW), F32),
        grid=(B, n_tiles),
        in_specs=[
            pl.BlockSpec((1, tq, B_HEADS * LANE), lambda b, t: (b, t + tile0, 0)),
            pl.BlockSpec((1, S, B_HEADS * LANE), lambda b, t: (b, 0, 0)),
            pl.BlockSpec((1, S, BW), lambda b, t: (b, 0, 0)),
        ],
        out_specs=pl.BlockSpec((1, tq, BW), lambda b, t: (b, t, 0)),
        compiler_params=_params(("parallel", "arbitrary")),
        name="attb",
    )(qb, kb, vb)


def _attc_kernel(tile0, ctx_len, n_lat, sink_ref, q_ref, k_ref, v_ref, o_ref):
    W = WINDOW
    ctx_tiles = ctx_len // W
    n_blk = n_lat // W
    t = pl.program_id(1) + tile0
    lane = lax.broadcasted_iota(jnp.int32, (1, LANE), 1)
    lo = lane < C_HEAD
    rows4 = lax.broadcasted_iota(jnp.int32, (C_GROUP * W, 1), 0)

    def attend(k_cat, v_cat, mask):
        for g in range(C_KV_HEADS):
            q_st = jnp.concatenate(
                [q_ref[0, :, h * LANE:(h + 1) * LANE] for h in range(g * C_GROUP, (g + 1) * C_GROUP)],
                axis=0)
            s = _dot(q_st, k_cat, NT)
            if mask is not None:
                s = jnp.where(mask, s, NEG)
            sk = jnp.zeros((C_GROUP * W, 1), F32)
            for i in range(C_GROUP):
                sk = jnp.where(rows4 // W == i, sink_ref[g * C_GROUP + i], sk)
            m = jnp.maximum(jnp.max(s, axis=-1, keepdims=True), sk)
            p = jnp.exp(s - m)
            den = jnp.sum(p, axis=-1, keepdims=True) + jnp.exp(sk - m)
            o = _dot(p.astype(BF16), v_cat[:, g * LANE:(g + 1) * LANE]) / den
            for jj in range(C_GROUP // 2):
                o_lo = o[(2 * jj) * W:(2 * jj + 1) * W]
                o_hi = o[(2 * jj + 1) * W:(2 * jj + 2) * W]
                col = (g * C_GROUP // 2 + jj) * LANE
                o_ref[0, :, col:col + LANE] = jnp.where(lo, o_lo, o_hi)

    def latent():
        j = t - ctx_tiles
        bm = jnp.maximum(j - 1, 0)
        bp = jnp.minimum(j + 1, n_blk - 1)

        def blk(ref, b):
            return ref[0, pl.ds(pl.multiple_of(ctx_len + b * W, W), W), :]

        k_cat = jnp.concatenate([k_ref[0, 0:ctx_len, :], blk(k_ref, bm), blk(k_ref, j), blk(k_ref, bp)], axis=0)
        v_cat = jnp.concatenate([v_ref[0, 0:ctx_len, :], blk(v_ref, bm), blk(v_ref, j), blk(v_ref, bp)], axis=0)
        n = ctx_len + 3 * W
        qi = lax.broadcasted_iota(jnp.int32, (C_GROUP * W, n), 0) % W
        col = lax.broadcasted_iota(jnp.int32, (C_GROUP * W, n), 1)
        cb = col - ctx_len
        kpos = (j - 1) * W + cb
        diff = cb - W - qi
        band = (jnp.abs(diff) <= WINDOW) & (kpos >= 0) & (kpos < n_lat)
        attend(k_cat, v_cat, (col < ctx_len) | band)

    def context():
        attend(k_ref[0, 0:ctx_len, :], v_ref[0, 0:ctx_len, :], None)

    if tile0 < ctx_tiles:
        pl.when(t < ctx_tiles)(context)
        pl.when(t >= ctx_tiles)(latent)
    else:
        latent()


def _attc_call(qc, kc, vc, sink, ctx_len, with_ctx):
    B, S, _ = qc.shape
    W = WINDOW
    ctx_tiles = ctx_len // W
    tile0 = 0 if with_ctx else ctx_tiles
    n_tiles = S // W - tile0
    return pl.pallas_call(
        functools.partial(_attc_kernel, tile0, ctx_len, S - ctx_len),
        out_shape=jax.ShapeDtypeStruct((B, n_tiles * W, BW), F32),
        grid=(B, n_tiles),
        in_specs=[
            pl.BlockSpec(memory_space=pltpu.SMEM),
            pl.BlockSpec((1, W, C_HEADS * LANE), lambda b, t: (b, t + tile0, 0)),
            pl.BlockSpec((1, S, C_KV_W), lambda b, t: (b, 0, 0)),
            pl.BlockSpec((1, S, 2 * LANE), lambda b, t: (b, 0, 0)),
        ],
        out_specs=pl.BlockSpec((1, W, BW), lambda b, t: (b, t, 0)),
        compiler_params=_params(("parallel", "arbitrary")),
        name="attc",
    )(sink, qc, kc, vc)


def _merge_kernel(y0_ref, y1_ref, bonus_ref, ga_ref, yb_ref, gb_ref, yc_ref, gc_ref, zg_ref,
                  x_ref, gate_ref, gng_ref, gnb_ref, ones_ref, wbo_ref, wout_ref, o_ref):
    ones = ones_ref[...]
    y = y0_ref[0] + y1_ref[0]
    mu = _mm_exact_rhs(y, ones) * (1.0 / A_HEAD)
    dlt = y - mu
    var = _mm_exact_rhs(dlt * dlt, ones) * (1.0 / A_HEAD)
    ya = dlt * lax.rsqrt(var + A_GN_EPS) * gng_ref[...] + gnb_ref[...] + bonus_ref[0]
    m = None
    for n, (y_n, g_ref) in enumerate(((ya, ga_ref), (yb_ref[0], gb_ref), (yc_ref[0], gc_ref))):
        u = (y_n * _silu(g_ref[0])).astype(BF16)
        gate_n = _sigmoid(zg_ref[0, :, n * D_MODEL:(n + 1) * D_MODEL])
        term = gate_n * _dot(u, wbo_ref[n])
        m = term if m is None else m + term
    o_ref[0] = x_ref[0] + gate_ref[0] * _dot(m.astype(BF16), wout_ref[...])


def _merge_call(y0, y1, bonus, ga, yb, gb, yc, gc, zg, xall, modl, lp, ones_a, ctx_len, with_ctx):
    B, S, _ = xall.shape
    tt = TOK_TILE
    ctx_tiles = ctx_len // tt
    tile0 = 0 if with_ctx else ctx_tiles
    n_tiles = S // tt - tile0
    row = _mod_row(ctx_tiles, B)
    mod3 = modl.reshape(modl.shape[0], 1, 3 * D_MODEL)
    full = lambda shape: pl.BlockSpec(shape, lambda b, t: (0,) * len(shape))
    tok = lambda w: pl.BlockSpec((1, tt, w), lambda b, t: (b, t + tile0, 0))
    att = lambda w: pl.BlockSpec((1, tt, w), lambda b, t: (b, t, 0))
    return pl.pallas_call(
        _merge_kernel,
        out_shape=jax.ShapeDtypeStruct((B, n_tiles * tt, D_MODEL), F32),
        grid=(B, n_tiles),
        in_specs=[tok(BW), tok(BW), tok(BW), tok(BW), att(BW), tok(BW), att(BW), tok(BW),
                  tok(N_BRANCH * D_MODEL), tok(D_MODEL),
                  pl.BlockSpec((1, 1, D_MODEL), lambda b, t: (row(b, t + tile0), 0, 2)),
                  full((1, BW)), full((1, BW)), full((BW, BW)),
                  full((N_BRANCH, BW, D_MODEL)), full((D_MODEL, D_MODEL))],
        out_specs=pl.BlockSpec((1, tt, D_MODEL), lambda b, t: (b, t, 0)),
        compiler_params=_params(("parallel", "arbitrary")),
        name="merge",
    )(y0, y1, bonus, ga, yb, gb, yc, gc, zg, xall, mod3, lp["gn_g"], lp["gn_b"], ones_a,
      lp["w_bo"], lp["w_out"])


def _rope_tables(n_ctx, n_lat, rot_dim, head_w, lane0):
    rows = n_lat // GRID_W
    row = jnp.repeat(jnp.arange(rows), GRID_W).astype(F32)
    colp = jnp.tile(jnp.arange(GRID_W), rows).astype(F32)
    q = rot_dim // 4
    inv = ROPE_THETA ** (-(2.0 * jnp.arange(q, dtype=F32)) / (rot_dim // 2))
    ang = jnp.concatenate([row[:, None] * inv, colp[:, None] * inv], axis=-1)
    cos, sin = jnp.cos(ang), jnp.sin(ang)
    zero = jnp.zeros_like(sin[:, :q])
    cos_g = jnp.concatenate([cos[:, :q], cos[:, :q], cos[:, q:], cos[:, q:]], axis=-1)
    sp_g = jnp.concatenate([zero, sin[:, :q], zero, sin[:, q:]], axis=-1)
    sm_g = jnp.concatenate([-sin[:, :q], zero, -sin[:, q:], zero], axis=-1)

    def place(t, fill):
        pad_l = jnp.full((n_lat, lane0), fill, F32)
        pad_r = jnp.full((n_lat, head_w - lane0 - rot_dim), fill, F32)
        grp = jnp.concatenate([pad_l, t, pad_r], axis=-1)
        lat = jnp.tile(grp, (1, LANE // head_w))
        return jnp.concatenate([jnp.full((n_ctx, LANE), fill, F32), lat], axis=0)

    return jnp.stack([place(cos_g, 1.0), place(sp_g, 0.0), place(sm_g, 0.0)])


def _block_ones(n, blk):
    i = np.arange(n) // blk
    return jnp.asarray(i[:, None] == i[None, :], dtype=BF16)


def _pad_lanes(g, width):
    return jnp.pad(g, (0, width - g.shape[0])).reshape(1, width)


def _layer_params(i, w_in, a_mu_prev, a_mu_next, a_w0, a_w_up, a_a0, a_a_up, a_k_k, a_k_a, a_r_k,
                  a_gn_g, a_gn_b, b_q_ln, b_kv_ln, b_w_uq, b_w_ukv, b_qn_g, b_kn_g,
                  c_qn_g, c_kn_g, c_sink, w_branch_out, w_out):
    w = w_in[i]
    c0 = 0
    segs = {}
    for name, width in (("za", A_COLS), ("ga", BW), ("zb", B_COLS), ("gb", BW),
                        ("zc", C_COLS), ("gc", BW), ("zg", N_BRANCH * D_MODEL)):
        segs[name] = w[:, c0:c0 + width]
        c0 += width
    zb = segs["zb"]
    zpad = lambda n: jnp.zeros((D_MODEL, n), w.dtype)
    segs["zb"] = jnp.concatenate(
        [zb[:, :B_Q_LORA + B_KV_LORA], zpad(B_NOPE), zb[:, B_Q_LORA + B_KV_LORA:], zpad(LANE - B_NOPE - B_ROPE)],
        axis=-1)
    w_in_p = jnp.concatenate([segs[n] for n, _ in IN_SEGS], axis=-1).astype(BF16)

    uq = b_w_uq[i].reshape(B_Q_LORA, B_HEADS, B_QK)
    uq = jnp.pad(uq, ((0, 0), (0, 0), (0, LANE - B_QK))).reshape(B_Q_LORA, B_HEADS * LANE)
    ukv = b_w_ukv[i].reshape(B_KV_LORA, B_HEADS, B_NOPE + B_V)
    uk = jnp.pad(ukv[:, :, :B_NOPE], ((0, 0), (0, 0), (0, LANE - B_NOPE))).reshape(B_KV_LORA, B_HEADS * LANE)
    uv = ukv[:, :, B_NOPE:].reshape(B_KV_LORA, BW)
    row = lambda t: t.reshape(1, -1)
    return dict(
        w_in=w_in_p,
        mu_prev=row(a_mu_prev[i]), mu_next=row(a_mu_next[i]),
        w0=a_w0[i], w_up=a_w_up[i], a0=a_a0[i], a_up=a_a_up[i],
        k_k=row(a_k_k[i]), k_a=row(a_k_a[i]), r_k=row(a_r_k[i]),
        gn_g=row(a_gn_g[i]), gn_b=row(a_gn_b[i]),
        q_ln=row(b_q_ln[i]), kv_ln=row(b_kv_ln[i]),
        w_uq=uq.astype(BF16), w_uk=uk.astype(BF16), w_uv=uv.astype(BF16),
        b_qn=_pad_lanes(b_qn_g[i], LANE), b_kn=_pad_lanes(b_kn_g[i], LANE),
        c_qn=jnp.tile(c_qn_g[i], 2).reshape(1, LANE), c_kn=jnp.tile(c_kn_g[i], 2).reshape(1, LANE),
        sink=c_sink[i],
        w_bo=w_branch_out[i].astype(BF16), w_out=w_out[i].astype(BF16),
    )


SCAN_PASSES = 1


def kernel(x, c, ctx, c_ctx, ada_w, ada_b, norm_g, w_in, a_mu_prev, a_mu_next, a_w0, a_w_up, a_a0, a_a_up, a_k_k, a_k_a, a_r_k, a_gn_g, a_gn_b, b_q_ln, b_kv_ln, b_w_uq, b_w_ukv, b_qn_g, b_kn_g, c_qn_g, c_kn_g, c_sink, w_branch_out, w_out):
    B, T, D = x.shape
    L = ctx.shape[1]
    depth = ada_w.shape[0]
    assert D == D_MODEL and L % TOK_TILE == 0 and T % TOK_TILE == 0 and T % GRID_W == 0

    mod_rows = -(-(B + 1) // 8) * 8
    cs = jnp.concatenate([c, c_ctx[None, :], jnp.zeros((mod_rows - B - 1, D), c.dtype)], axis=0)
    mod = _mod_call(cs, ada_w, ada_b)

    tab_b = _rope_tables(L, T, B_ROPE, LANE, B_NOPE)
    tab_c = _rope_tables(L, T, C_HEAD, C_HEAD, 0)
    ones_a = _block_ones(BW, A_HEAD)

    xall = jnp.concatenate([ctx, x], axis=1)
    for i in range(depth):
        with_ctx = i < depth - 1
        lp = _layer_params(i, w_in, a_mu_prev, a_mu_next, a_w0, a_w_up, a_a0, a_a_up, a_k_k, a_k_a,
                           a_r_k, a_gn_g, a_gn_b, b_q_ln, b_kv_ln, b_w_uq, b_w_ukv, b_qn_g, b_kn_g,
                           c_qn_g, c_kn_g, c_sink, w_branch_out, w_out)
        za, ga, zb, gb, zc, gc, zg = _inproj_call(xall, mod[i], norm_g[i], lp["w_in"], L)
        r, v, kk, lw, kd, nb, bonus = _prep_call(za, lp, ones_a, L)
        y0, y1 = _scan_call(r, v, kk, lw, kd, nb, L, SCAN_PASSES)
        qb, kb, vb, qc, kc, vc = _bcproj_call(zb, zc, tab_b, tab_c, lp)
        yb = _attb_call(qb, kb, vb, L, with_ctx)
        yc = _attc_call(qc, kc, vc, lp["sink"], L, with_ctx)
        xall = _merge_call(y0, y1, bonus, ga, yb, gb, yc, gc, zg, xall, mod[i], lp, ones_a, L, with_ctx)
    return xall
```

```python
import functools

import numpy as np
import jax
import jax.numpy as jnp
from jax import lax
from jax.experimental import pallas as pl
from jax.experimental.pallas import tpu as pltpu

F32 = jnp.float32
BF16 = jnp.bfloat16

D_MODEL = 1024
GRID_W = 64
ROPE_THETA = 10000.0
NORM_EPS = 1e-6
NEG = -1e30
BW = 512
N_BRANCH = 3

A_HEAD = 64
A_HEADS = BW // A_HEAD
A_LORA = 64
A_GN_EPS = 64e-5
A_COLS = 3 * BW + 4 * A_LORA

B_HEADS = 8
B_NOPE = 64
B_ROPE = 32
B_QK = B_NOPE + B_ROPE
B_V = BW // B_HEADS
B_Q_LORA = 256
B_KV_LORA = 128
B_COLS = B_Q_LORA + B_KV_LORA + B_ROPE
B_COLS_P = 512

C_HEAD = 64
C_HEADS = BW // C_HEAD
C_KV_HEADS = 2
C_GROUP = C_HEADS // C_KV_HEADS
C_KV_W = C_KV_HEADS * C_HEAD
C_COLS = BW + 2 * C_KV_W
WINDOW = 128

LANE = 128
CHUNK = 64
TOK_TILE = 256
VMEM_LIMIT = 56 * 1024 * 1024

IN_SEGS = (("za", A_COLS), ("ga", BW), ("zb", B_COLS_P), ("gb", BW),
           ("zc", C_COLS), ("gc", BW), ("zg", N_BRANCH * D_MODEL))
N_IN_P = sum(w for _, w in IN_SEGS)

NN = (((1,), (0,)), ((), ()))
NT = (((1,), (1,)), ((), ()))


def _dot(a, b, dims=NN):
    return lax.dot_general(a, b, dims, preferred_element_type=F32)


def _split(x):
    hi = x.astype(BF16)
    lo = (x - hi.astype(F32)).astype(BF16)
    return hi, lo


def _mm(a, b, dims=NN, passes=1):
    if passes == 1:
        return _dot(a.astype(BF16), b.astype(BF16), dims)
    ah, al = _split(a)
    bh, bl = _split(b)
    return _dot(ah, bh, dims) + (_dot(ah, bl, dims) + _dot(al, bh, dims))


def _mm_exact_rhs(a, b_bf16, parts=2):
    acc = None
    rem = a
    for _ in range(parts):
        p = rem.astype(BF16)
        t = _dot(p, b_bf16)
        acc = t if acc is None else acc + t
        rem = rem - p.astype(F32)
    return acc


def _mm_exact_lhs(a_bf16, b, parts=3):
    acc = None
    rem = b
    for _ in range(parts):
        p = rem.astype(BF16)
        t = _dot(a_bf16, p)
        acc = t if acc is None else acc + t
        rem = rem - p.astype(F32)
    return acc


def _sigmoid(x):
    return 1.0 / (1.0 + jnp.exp(-x))


def _silu(x):
    return x * _sigmoid(x)


def _params(sem):
    return pltpu.CompilerParams(dimension_semantics=sem, vmem_limit_bytes=VMEM_LIMIT)


def _mod_kernel(c_ref, w_ref, b_ref, o_ref):
    s = _silu(c_ref[...])
    o_ref[0] = _mm(s, w_ref[0], passes=3) + b_ref[0]


def _mod_call(cs, ada_w, ada_b):
    n_layers = ada_w.shape[0]
    rows = cs.shape[0]
    return pl.pallas_call(
        _mod_kernel,
        out_shape=jax.ShapeDtypeStruct((n_layers, rows, 3 * D_MODEL), F32),
        grid=(n_layers, 3),
        in_specs=[
            pl.BlockSpec((rows, D_MODEL), lambda l, n: (0, 0)),
            pl.BlockSpec((1, D_MODEL, D_MODEL), lambda l, n: (l, 0, n)),
            pl.BlockSpec((1, 1, D_MODEL), lambda l, n: (l, 0, n)),
        ],
        out_specs=pl.BlockSpec((1, rows, D_MODEL), lambda l, n: (l, 0, n)),
        compiler_params=_params(("arbitrary", "arbitrary")),
        name="mod",
    )(cs, ada_w, ada_b.reshape(n_layers, 1, 3 * D_MODEL))


def _inproj_kernel(x_ref, shift_ref, scale_ref, g_ref, w_ref, *o_refs):
    x = x_ref[0]
    ms = jnp.mean(x * x, axis=-1, keepdims=True)
    h = x * lax.rsqrt(ms + NORM_EPS) * g_ref[...]
    h = h * (1.0 + scale_ref[0]) + shift_ref[0]
    hb = h.astype(BF16)
    col = 0
    for (_, width), o_ref in zip(IN_SEGS, o_refs):
        step = 512 if width % 512 == 0 else 256
        for c0 in range(0, width, step):
            o_ref[0, :, c0:c0 + step] = _dot(hb, w_ref[:, col + c0:col + c0 + step])
        col += width


def _mod_row(ctx_tiles, n_batch_rows):
    return lambda b, t: jnp.where(t < ctx_tiles, n_batch_rows, b)


def _inproj_call(xall, modl, norm_g, w_in_p, ctx_len):
    B, S, _ = xall.shape
    tm = TOK_TILE
    row = _mod_row(ctx_len // tm, B)
    mod3 = modl.reshape(modl.shape[0], 1, 3 * D_MODEL)
    outs = tuple(jax.ShapeDtypeStruct((B, S, w), F32) for _, w in IN_SEGS)
    return pl.pallas_call(
        _inproj_kernel,
        out_shape=outs,
        grid=(B, S // tm),
        in_specs=[
            pl.BlockSpec((1, tm, D_MODEL), lambda b, t: (b, t, 0)),
            pl.BlockSpec((1, 1, D_MODEL), lambda b, t: (row(b, t), 0, 0)),
            pl.BlockSpec((1, 1, D_MODEL), lambda b, t: (row(b, t), 0, 1)),
            pl.BlockSpec((1, D_MODEL), lambda b, t: (0, 0)),
            pl.BlockSpec((D_MODEL, N_IN_P), lambda b, t: (0, 0),
                         pipeline_mode=pl.Buffered(1)),
        ],
        out_specs=tuple(pl.BlockSpec((1, tm, w), lambda b, t: (b, t, 0)) for _, w in IN_SEGS),
        compiler_params=_params(("parallel", "arbitrary")),
        name="inproj",
    )(xall, mod3, mod3, norm_g.reshape(1, D_MODEL), w_in_p)


def _prep_kernel(ctx_tiles, n_tiles,
                 z_ref, zp_ref, zn_ref, mup_ref, mun_ref, w0_ref, wup_ref, a0_ref, aup_ref,
                 kk_ref, ka_ref, rk_ref, ones_ref,
                 r_out, v_out, kk_out, lw_out, kd_out, nb_out, bonus_out, zs_ref):
    t = pl.program_id(1)
    tt = z_ref.shape[1]
    first = jnp.logical_or(t == 0, t == ctx_tiles)
    last = jnp.logical_or(t == ctx_tiles - 1, t == n_tiles - 1)
    z = z_ref[0]
    zs_ref[0:8, :] = jnp.where(first, 0.0, zp_ref[0])
    zs_ref[8:8 + tt, :] = z
    zs_ref[8 + tt:16 + tt, :] = jnp.where(last, 0.0, zn_ref[0])
    prev = zs_ref[7:7 + tt, :]
    nxt = zs_ref[9:9 + tt, :]
    zsh = z + mup_ref[...] * (prev - z) + mun_ref[...] * (nxt - z)

    r = zsh[:, 0:BW]
    k = zsh[:, BW:2 * BW]
    v = zsh[:, 2 * BW:3 * BW]
    ones = ones_ref[...]
    kk = k * kk_ref[...]
    ss = _mm_exact_rhs(kk * kk, ones)
    kk = kk * lax.rsqrt(jnp.maximum(ss, 1e-24))
    r_out[0] = r
    v_out[0] = v
    kk_out[0] = kk
    kd_sum = None
    for d in range(2):
        wd = zsh[:, 3 * BW + d * A_LORA:3 * BW + (d + 1) * A_LORA]
        ad = zsh[:, 3 * BW + 2 * A_LORA + d * A_LORA:3 * BW + 2 * A_LORA + (d + 1) * A_LORA]
        u = -(w0_ref[d:d + 1, :] + _mm(jnp.tanh(wd), wup_ref[d], passes=3))
        softplus = jnp.maximum(u, 0.0) + jnp.log(1.0 + jnp.exp(-jnp.abs(u)))
        w_log = -softplus - 0.5
        lw = -jnp.exp(w_log)
        a = _sigmoid(a0_ref[d:d + 1, :] + _mm(ad, aup_ref[d], passes=3))
        kd = k * (1.0 + (a - 1.0) * ka_ref[...])
        lw_out[d, 0] = lw
        kd_out[d, 0] = kd
        nb_out[d, 0] = -(kk * a)
        kd_sum = kd if kd_sum is None else kd_sum + kd
    bonus_out[0] = _mm_exact_rhs(r * kd_sum * rk_ref[...], ones) * v


def _prep_call(za, lp, ones_a, ctx_len):
    B, S, _ = za.shape
    tt = TOK_TILE
    n_tiles = S // tt
    hb = tt // 8
    n8 = S // 8
    full = lambda shape: pl.BlockSpec(shape, lambda b, t: (0,) * len(shape))
    o3 = jax.ShapeDtypeStruct((B, S, BW), F32)
    o4 = jax.ShapeDtypeStruct((2, B, S, BW), F32)
    s3 = pl.BlockSpec((1, tt, BW), lambda b, t: (b, t, 0))
    s4 = pl.BlockSpec((2, 1, tt, BW), lambda b, t: (0, b, t, 0))
    return pl.pallas_call(
        functools.partial(_prep_kernel, ctx_len // tt, n_tiles),
        out_shape=(o3, o3, o3, o4, o4, o4, o3),
        grid=(B, n_tiles),
        in_specs=[
            pl.BlockSpec((1, tt, A_COLS), lambda b, t: (b, t, 0)),
            pl.BlockSpec((1, 8, A_COLS), lambda b, t: (b, jnp.maximum(t * hb - 1, 0), 0)),
            pl.BlockSpec((1, 8, A_COLS), lambda b, t: (b, jnp.minimum((t + 1) * hb, n8 - 1), 0)),
            full((1, A_COLS)), full((1, A_COLS)),
            full((2, BW)), full((2, A_LORA, BW)), full((2, BW)), full((2, A_LORA, BW)),
            full((1, BW)), full((1, BW)), full((1, BW)), full((BW, BW)),
        ],
        out_specs=(s3, s3, s3, s4, s4, s4, s3),
        scratch_shapes=[pltpu.VMEM((tt + 16, A_COLS), F32)],
        compiler_params=_params(("parallel", "arbitrary")),
        name="prep",
    )(za, za, za, lp["mu_prev"], lp["mu_next"], lp["w0"], lp["w_up"], lp["a0"], lp["a_up"],
      lp["k_k"], lp["k_a"], lp["r_k"], ones_a)


def _scan_kernel(passes, fr, fv, fkk, flw, fkd, fnb, br, bv, bkk, blw, bkd, bnb,
                 yf_ref, yb_ref, state_ref):
    C = CHUNK

    @pl.when(pl.program_id(1) == 0)
    def _():
        state_ref[...] = jnp.zeros_like(state_ref)

    lane = lax.broadcasted_iota(jnp.int32, (1, LANE), 1)
    lo = lane < A_HEAD
    row2 = lax.broadcasted_iota(jnp.int32, (2 * C, 2 * C), 0)
    col2 = lax.broadcasted_iota(jnp.int32, (2 * C, 2 * C), 1)
    rows = lax.broadcasted_iota(jnp.int32, (C, C), 0)
    cols = lax.broadcasted_iota(jnp.int32, (C, C), 1)
    eye = (rows == cols).astype(F32)
    n_pairs = A_HEADS // 2

    dirs = []
    for d, (r_ref, v_ref, kk_ref, lw_ref, kd_ref, nb_ref) in enumerate(
            ((fr, fv, fkk, flw, fkd, fnb), (br, bv, bkk, blw, bkd, bnb))):
        order = (rows - cols) if d == 0 else (cols - rows)
        lw = lw_ref[0, 0]
        cum = _mm_exact_lhs((order >= 0).astype(BF16), lw)
        tot = cum[C - 1:C, :] if d == 0 else cum[0:1, :]
        p_inv = jnp.exp(-cum)
        p_tot = jnp.exp(tot - cum)
        nb = nb_ref[0, 0]
        kd = kd_ref[0, 0]
        kt = kk_ref[0] * jnp.exp(cum - lw)
        order2 = (row2 % C - col2 % C) if d == 0 else (col2 % C - row2 % C)
        keep = order2 >= jnp.where(row2 < C, 0, 1)
        e_rows = jnp.broadcast_to(jnp.exp(tot), (LANE, BW))
        dirs.append(dict(
            keep=keep, kt=kt, v=v_ref[0],
            lhs=jnp.concatenate([r_ref[0] * jnp.exp(cum), kt], axis=0),
            rhs=jnp.concatenate([nb * p_inv, kd * p_inv], axis=0),
            nkt=jnp.concatenate([nb * p_tot, kd * p_tot, e_rows], axis=0).T))

    units = [(d, h) for d in range(2) for h in range(A_HEADS)]
    slab = lambda h: slice((h // 2) * LANE, (h // 2 + 1) * LANE)
    own = lambda h: lo if h % 2 == 0 else jnp.logical_not(lo)

    lhs_m, top, bot = [], [], []
    for d, h in units:
        x = dirs[d]
        lm = jnp.where(own(h), x["lhs"][:, slab(h)], 0.0)
        qk = jnp.where(x["keep"], _mm(lm, x["rhs"][:, slab(h)], NT, passes), 0.0)
        lhs_m.append(lm)
        top.append(qk[0:C])
        bot.append(qk[C:2 * C])
    pw = [_mm(b[:, 0:C], b[:, 0:C], NN, passes) for b in bot]
    t_inv = [eye + b[:, 0:C] for b in bot]
    for _ in range(4):
        both = [_mm(jnp.concatenate([t, p], axis=0), p, NN, passes) for t, p in zip(t_inv, pw)]
        t_inv = [t + m[0:C] for t, m in zip(t_inv, both)]
        pw = [m[C:2 * C] for m in both]
    t_inv = [t + _mm(t, p, NN, passes) for t, p in zip(t_inv, pw)]
    wta = []
    for (d, h), t, b in zip(units, t_inv, bot):
        kt_s = dirs[d]["kt"][:, slab(h)]
        if h % 2:
            kt_s = pltpu.roll(kt_s, A_HEAD, 1)
        wta.append(_mm(t, jnp.where(lo, kt_s, b), NN, passes))
    pairs = [(d, j) for d in range(2) for j in range(n_pairs)]
    st = [state_ref[i] for i in range(len(pairs))]
    vp = [dirs[d]["v"][:, j * LANE:(j + 1) * LANE] for d, j in pairs]
    sv = [jnp.concatenate([s, v], axis=0) for s, v in zip(st, vp)]
    qt_u = [_mm(m, sv[i // 2], NN, passes) for i, m in enumerate(wta)]
    qt = [jnp.where(lo, qt_u[2 * i], qt_u[2 * i + 1]) for i in range(len(pairs))]
    qv = [jnp.concatenate([q, v], axis=0) for q, v in zip(qt, vp)]
    y_u, s_u = [], []
    for i, (d, h) in enumerate(units):
        p = i // 2
        y_l = jnp.concatenate([top[i], lhs_m[i][0:C]], axis=1)
        y_r = jnp.concatenate([qv[p], st[p], st[p]], axis=0)
        y_u.append(_mm(y_l, y_r, NN, passes))
        nk = dirs[d]["nkt"][h * A_HEAD:(h + 1) * A_HEAD]
        s_u.append(nk[:, 2 * C:] * st[p] + _mm(nk[:, 0:2 * C], qv[p], NN, passes))
    for p, (d, j) in enumerate(pairs):
        sl = slice(j * LANE, (j + 1) * LANE)
        (yf_ref if d == 0 else yb_ref)[0, :, sl] = jnp.where(lo, y_u[2 * p], y_u[2 * p + 1])
        state_ref[p] = jnp.where(lo, s_u[2 * p], s_u[2 * p + 1])


def _scan_call(r, v, kk, lw, kd, nb, ctx_len, passes):
    B, S, _ = r.shape
    C = CHUNK
    nc = S // C
    lc = ctx_len // C
    rev = lambda c: jnp.where(c < lc, lc - 1 - c, nc - 1 - (c - lc))
    f3 = pl.BlockSpec((1, C, BW), lambda b, c: (b, c, 0))
    b3 = pl.BlockSpec((1, C, BW), lambda b, c: (b, rev(c), 0))
    f4 = pl.BlockSpec((1, 1, C, BW), lambda b, c: (0, b, c, 0))
    b4 = pl.BlockSpec((1, 1, C, BW), lambda b, c: (1, b, rev(c), 0))
    out = jax.ShapeDtypeStruct((B, S, BW), F32)
    return pl.pallas_call(
        functools.partial(_scan_kernel, passes),
        out_shape=(out, out),
        grid=(B, nc),
        in_specs=[f3, f3, f3, f4, f4, f4, b3, b3, b3, b4, b4, b4],
        out_specs=(f3, b3),
        scratch_shapes=[pltpu.VMEM((A_HEADS, A_HEAD, LANE), F32)],
        compiler_params=_params(("parallel", "arbitrary")),
        name="scan",
    )(r, v, kk, lw, kd, nb, r, v, kk, lw, kd, nb)


def _rope(x, tab_ref, shift):
    return (x * tab_ref[0] + pltpu.roll(x, shift, 1) * tab_ref[1]
            + pltpu.roll(x, LANE - shift, 1) * tab_ref[2])


def _bcproj_kernel(zb_ref, zc_ref, tbq_ref, tbk_ref, tcq_ref, tck_ref, qln_ref, kvln_ref,
                   wuq_ref, wuk_ref, wuv_ref, onesb_ref, onesc_ref,
                   qb_out, kb_out, vb_out, qc_out, kc_out, vc_out):
    ones_b = onesb_ref[...]
    ones_c = onesc_ref[...]
    zb = zb_ref[0]
    cq = zb[:, 0:B_Q_LORA]
    ckv = zb[:, B_Q_LORA:B_Q_LORA + B_KV_LORA]
    kr_slab = zb[:, B_Q_LORA + B_KV_LORA:B_COLS_P]
    cqn = cq * lax.rsqrt(jnp.mean(cq * cq, axis=-1, keepdims=True) + NORM_EPS) * qln_ref[...]
    ckvn = ckv * lax.rsqrt(jnp.mean(ckv * ckv, axis=-1, keepdims=True) + NORM_EPS) * kvln_ref[...]
    q_all = _dot(cqn.astype(BF16), wuq_ref[...])
    k_all = _dot(ckvn.astype(BF16), wuk_ref[...])
    vb_out[0] = _dot(ckvn.astype(BF16), wuv_ref[...]).astype(BF16)
    for h in range(B_HEADS):
        sl = slice(h * LANE, (h + 1) * LANE)
        q = q_all[:, sl]
        rs = lax.rsqrt(_mm_exact_rhs(q * q, ones_b) * (1.0 / B_QK) + NORM_EPS)
        qb_out[0, :, sl] = (_rope(q, tbq_ref, B_ROPE // 4) * rs).astype(BF16)
        k = k_all[:, sl] + kr_slab
        rs = lax.rsqrt(_mm_exact_rhs(k * k, ones_b) * (1.0 / B_QK) + NORM_EPS)
        kb_out[0, :, sl] = (_rope(k, tbk_ref, B_ROPE // 4) * rs).astype(BF16)

    zc = zc_ref[0]
    lane = lax.broadcasted_iota(jnp.int32, (1, LANE), 1)
    lo = lane < C_HEAD

    def head_norm(x, tab_ref):
        ms = _mm_exact_rhs(x * x, ones_c) * (1.0 / C_HEAD)
        return _rope(x, tab_ref, C_HEAD // 4) * lax.rsqrt(ms + NORM_EPS)

    for j in range(C_HEADS // 2):
        x = head_norm(zc[:, j * LANE:(j + 1) * LANE], tcq_ref)
        xr = pltpu.roll(x, C_HEAD, 1)
        g = (2 * j) // C_GROUP
        for half in range(2):
            h = 2 * j + half
            src = x if half == g else xr
            keep = lo if g == 0 else jnp.logical_not(lo)
            qc_out[0, :, h * LANE:(h + 1) * LANE] = jnp.where(keep, src, 0.0).astype(BF16)
    kc_out[0] = head_norm(zc[:, BW:BW + C_KV_W], tck_ref).astype(BF16)
    vv = zc[:, BW + C_KV_W:BW + 2 * C_KV_W]
    vr = pltpu.roll(vv, C_HEAD, 1)
    vc_out[0, :, 0:LANE] = jnp.where(lo, vv, vr).astype(BF16)
    vc_out[0, :, LANE:2 * LANE] = jnp.where(lo, vr, vv).astype(BF16)


def _bcproj_call(zb, zc, lp):
    B, S, _ = zb.shape
    tt = TOK_TILE
    full = lambda shape: pl.BlockSpec(shape, lambda t, b: (0,) * len(shape))
    tok = lambda w: pl.BlockSpec((1, tt, w), lambda t, b: (b, t, 0))
    tab = pl.BlockSpec((3, tt, LANE), lambda t, b: (0, t, 0))
    shp = lambda w: jax.ShapeDtypeStruct((B, S, w), BF16)
    return pl.pallas_call(
        _bcproj_kernel,
        out_shape=(shp(B_HEADS * LANE), shp(B_HEADS * LANE), shp(BW),
                   shp(C_HEADS * LANE), shp(C_KV_W), shp(2 * LANE)),
        grid=(S // tt, B),
        in_specs=[tok(B_COLS_P), tok(C_COLS), tab, tab, tab, tab,
                  full((1, B_Q_LORA)), full((1, B_KV_LORA)),
                  full((B_Q_LORA, B_HEADS * LANE)), full((B_KV_LORA, B_HEADS * LANE)),
                  full((B_KV_LORA, BW)), full((LANE, LANE)), full((LANE, LANE))],
        out_specs=(tok(B_HEADS * LANE), tok(B_HEADS * LANE), tok(BW),
                   tok(C_HEADS * LANE), tok(C_KV_W), tok(2 * LANE)),
        compiler_params=_params(("parallel", "arbitrary")),
        name="bcproj",
    )(zb, zc, lp["tab_bq"], lp["tab_bk"], lp["tab_cq"], lp["tab_ck"],
      lp["q_ln"], lp["kv_ln"], lp["w_uq"], lp["w_uk"], lp["w_uv"],
      _block_ones(LANE, LANE), _block_ones(LANE, C_HEAD))


def _attb_kernel(tile0, ctx_tiles, ctx_len, q_ref, k_ref, v_ref, o_ref):
    t = pl.program_id(1) + tile0
    lane = lax.broadcasted_iota(jnp.int32, (1, LANE), 1)
    lo = lane < B_V

    def run(n_keys):
        for j in range(B_HEADS // 2):
            outs = []
            for h in (2 * j, 2 * j + 1):
                sl = slice(h * LANE, (h + 1) * LANE)
                s = _dot(q_ref[0, :, sl], k_ref[0, 0:n_keys, sl], NT)
                m = jnp.max(s, axis=-1, keepdims=True)
                p = jnp.exp(s - m)
                den = jnp.sum(p, axis=-1, keepdims=True)
                o = _dot(p.astype(BF16), v_ref[0, 0:n_keys, j * LANE:(j + 1) * LANE])
                outs.append(o / den)
            o_ref[0, :, j * LANE:(j + 1) * LANE] = jnp.where(lo, outs[0], outs[1])

    if tile0 < ctx_tiles:
        @pl.when(t < ctx_tiles)
        def _():
            run(ctx_len)

        @pl.when(t >= ctx_tiles)
        def _():
            run(k_ref.shape[1])
    else:
        run(k_ref.shape[1])


def _attb_call(qb, kb, vb, ctx_len, with_ctx):
    B, S, _ = qb.shape
    tq = TOK_TILE
    ctx_tiles = ctx_len // tq
    tile0 = 0 if with_ctx else ctx_tiles
    n_tiles = S // tq - tile0
    return pl.pallas_call(
        functools.partial(_attb_kernel, tile0, ctx_tiles, ctx_len),
        out_shape=jax.ShapeDtypeStruct((B, n_tiles * tq, BW), F32),
        grid=(B, n_tiles),
        in_specs=[
            pl.BlockSpec((1, tq, B_HEADS * LANE), lambda b, t: (b, t + tile0, 0)),
            pl.BlockSpec((1, S, B_HEADS * LANE), lambda b, t: (b, 0, 0)),
            pl.BlockSpec((1, S, BW), lambda b, t: (b, 0, 0)),
        ],
        out_specs=pl.BlockSpec((1, tq, BW), lambda b, t: (b, t, 0)),
        compiler_params=_params(("parallel", "arbitrary")),
        name="attb",
    )(qb, kb, vb)


def _attc_kernel(tile0, ctx_len, n_lat, sink_ref, q_ref, k_ref, v_ref, valid_ref, o_ref):
    W = WINDOW
    ctx_tiles = ctx_len // W
    n_blk = n_lat // W
    t = pl.program_id(1) + tile0
    lane = lax.broadcasted_iota(jnp.int32, (1, LANE), 1)
    lo = lane < C_HEAD
    rows4 = lax.broadcasted_iota(jnp.int32, (C_GROUP * W, 1), 0)

    def attend(k_cat, v_cat, mask):
        for g in range(C_KV_HEADS):
            q_st = jnp.concatenate(
                [q_ref[0, :, h * LANE:(h + 1) * LANE] for h in range(g * C_GROUP, (g + 1) * C_GROUP)],
                axis=0)
            s = _dot(q_st, k_cat, NT)
            if mask is not None:
                s = jnp.where(jnp.concatenate([mask] * C_GROUP, axis=0) > 0.5, s, NEG)
            sk = jnp.zeros((C_GROUP * W, 1), F32)
            for i in range(C_GROUP):
                sk = jnp.where(rows4 // W == i, sink_ref[g * C_GROUP + i], sk)
            m = jnp.maximum(jnp.max(s, axis=-1, keepdims=True), sk)
            p = jnp.exp(s - m)
            den = jnp.sum(p, axis=-1, keepdims=True) + jnp.exp(sk - m)
            o = _dot(p.astype(BF16), v_cat[:, g * LANE:(g + 1) * LANE]) / den
            for jj in range(C_GROUP // 2):
                o_lo = o[(2 * jj) * W:(2 * jj + 1) * W]
                o_hi = o[(2 * jj + 1) * W:(2 * jj + 2) * W]
                col = (g * C_GROUP // 2 + jj) * LANE
                o_ref[0, :, col:col + LANE] = jnp.where(lo, o_lo, o_hi)

    def latent():
        j = t - ctx_tiles
        bm = jnp.maximum(j - 1, 0)
        bp = jnp.minimum(j + 1, n_blk - 1)

        def blk(ref, b):
            return ref[0, pl.ds(pl.multiple_of(ctx_len + b * W, W), W), :]

        k_cat = jnp.concatenate([k_ref[0, 0:ctx_len, :], blk(k_ref, bm), blk(k_ref, j), blk(k_ref, bp)], axis=0)
        v_cat = jnp.concatenate([v_ref[0, 0:ctx_len, :], blk(v_ref, bm), blk(v_ref, j), blk(v_ref, bp)], axis=0)
        attend(k_cat, v_cat, valid_ref[0])

    def context():
        attend(k_ref[0, 0:ctx_len, :], v_ref[0, 0:ctx_len, :], None)

    if tile0 < ctx_tiles:
        pl.when(t < ctx_tiles)(context)
        pl.when(t >= ctx_tiles)(latent)
    else:
        latent()


def _attc_call(qc, kc, vc, sink, ctx_len, with_ctx):
    B, S, _ = qc.shape
    W = WINDOW
    ctx_tiles = ctx_len // W
    tile0 = 0 if with_ctx else ctx_tiles
    n_tiles = S // W - tile0
    n_blk = (S - ctx_len) // W
    valid = _band_valid(ctx_len)

    def variant(t):
        j = t + tile0 - ctx_tiles
        return jnp.where(j == 0, 1, 0) + jnp.where(j == n_blk - 1, 2, 0)

    return pl.pallas_call(
        functools.partial(_attc_kernel, tile0, ctx_len, S - ctx_len),
        out_shape=jax.ShapeDtypeStruct((B, n_tiles * W, BW), F32),
        grid=(B, n_tiles),
        in_specs=[
            pl.BlockSpec(memory_space=pltpu.SMEM),
            pl.BlockSpec((1, W, C_HEADS * LANE), lambda b, t: (b, t + tile0, 0)),
            pl.BlockSpec((1, S, C_KV_W), lambda b, t: (b, 0, 0)),
            pl.BlockSpec((1, S, 2 * LANE), lambda b, t: (b, 0, 0)),
            pl.BlockSpec((1, W, ctx_len + 3 * W), lambda b, t: (variant(t), 0, 0)),
        ],
        out_specs=pl.BlockSpec((1, W, BW), lambda b, t: (b, t, 0)),
        compiler_params=_params(("parallel", "arbitrary")),
        name="attc",
    )(sink, qc, kc, vc, valid)


def _band_valid(ctx_len):
    W = WINDOW
    qi = np.arange(W)[:, None]
    cb = np.arange(ctx_len + 3 * W)[None, :] - ctx_len
    base = (cb < 0) | (np.abs(cb - W - qi) <= WINDOW)
    out = []
    for variant in range(4):
        first, last = bool(variant & 1), bool(variant & 2)
        ok = base.copy()
        if first:
            ok &= ~((cb >= 0) & (cb < W))
        if last:
            ok &= ~(cb >= 2 * W)
        out.append(ok)
    return jnp.asarray(np.stack(out), dtype=F32)


def _merge_kernel(y0_ref, y1_ref, bonus_ref, ga_ref, yb_ref, gb_ref, yc_ref, gc_ref, zg_ref,
                  x_ref, gate_ref, gng_ref, gnb_ref, ones_ref, wbo_ref, wout_ref, o_ref):
    ones = ones_ref[...]
    y = y0_ref[0] + y1_ref[0]
    mu = _mm_exact_rhs(y, ones) * (1.0 / A_HEAD)
    dlt = y - mu
    var = _mm_exact_rhs(dlt * dlt, ones) * (1.0 / A_HEAD)
    ya = dlt * lax.rsqrt(var + A_GN_EPS) * gng_ref[...] + gnb_ref[...] + bonus_ref[0]
    m = None
    for n, (y_n, g_ref) in enumerate(((ya, ga_ref), (yb_ref[0], gb_ref), (yc_ref[0], gc_ref))):
        u = (y_n * _silu(g_ref[0])).astype(BF16)
        gate_n = _sigmoid(zg_ref[0, :, n * D_MODEL:(n + 1) * D_MODEL])
        term = gate_n * _dot(u, wbo_ref[n])
        m = term if m is None else m + term
    o_ref[0] = x_ref[0] + gate_ref[0] * _dot(m.astype(BF16), wout_ref[...])


def _merge_call(y0, y1, bonus, ga, yb, gb, yc, gc, zg, xall, modl, lp, ones_a, ctx_len, with_ctx):
    B, S, _ = xall.shape
    tt = TOK_TILE
    ctx_tiles = ctx_len // tt
    tile0 = 0 if with_ctx else ctx_tiles
    n_tiles = S // tt - tile0
    row = _mod_row(ctx_tiles, B)
    mod3 = modl.reshape(modl.shape[0], 1, 3 * D_MODEL)
    full = lambda shape: pl.BlockSpec(shape, lambda b, t: (0,) * len(shape))
    tok = lambda w: pl.BlockSpec((1, tt, w), lambda b, t: (b, t + tile0, 0))
    att = lambda w: pl.BlockSpec((1, tt, w), lambda b, t: (b, t, 0))
    return pl.pallas_call(
        _merge_kernel,
        out_shape=jax.ShapeDtypeStruct((B, n_tiles * tt, D_MODEL), F32),
        grid=(B, n_tiles),
        in_specs=[tok(BW), tok(BW), tok(BW), tok(BW), att(BW), tok(BW), att(BW), tok(BW),
                  tok(N_BRANCH * D_MODEL), tok(D_MODEL),
                  pl.BlockSpec((1, 1, D_MODEL), lambda b, t: (row(b, t + tile0), 0, 2)),
                  full((1, BW)), full((1, BW)), full((BW, BW)),
                  full((N_BRANCH, BW, D_MODEL)), full((D_MODEL, D_MODEL))],
        out_specs=pl.BlockSpec((1, tt, D_MODEL), lambda b, t: (b, t, 0)),
        compiler_params=_params(("parallel", "arbitrary")),
        name="merge",
    )(y0, y1, bonus, ga, yb, gb, yc, gc, zg, xall, mod3, lp["gn_g"], lp["gn_b"], ones_a,
      lp["w_bo"], lp["w_out"])


def _rope_tables(n_ctx, n_lat, rot_dim, head_w, lane0):
    rows = n_lat // GRID_W
    row = jnp.repeat(jnp.arange(rows), GRID_W).astype(F32)
    colp = jnp.tile(jnp.arange(GRID_W), rows).astype(F32)
    q = rot_dim // 4
    inv = ROPE_THETA ** (-(2.0 * jnp.arange(q, dtype=F32)) / (rot_dim // 2))
    ang = jnp.concatenate([row[:, None] * inv, colp[:, None] * inv], axis=-1)
    cos, sin = jnp.cos(ang), jnp.sin(ang)
    zero = jnp.zeros_like(sin[:, :q])
    cos_g = jnp.concatenate([cos[:, :q], cos[:, :q], cos[:, q:], cos[:, q:]], axis=-1)
    sp_g = jnp.concatenate([zero, sin[:, :q], zero, sin[:, q:]], axis=-1)
    sm_g = jnp.concatenate([-sin[:, :q], zero, -sin[:, q:], zero], axis=-1)

    def place(t, fill):
        pad_l = jnp.full((n_lat, lane0), fill, F32)
        pad_r = jnp.full((n_lat, head_w - lane0 - rot_dim), fill, F32)
        grp = jnp.concatenate([pad_l, t, pad_r], axis=-1)
        lat = jnp.tile(grp, (1, LANE // head_w))
        return jnp.concatenate([jnp.full((n_ctx, LANE), fill, F32), lat], axis=0)

    return jnp.stack([place(cos_g, 1.0), place(sp_g, 0.0), place(sm_g, 0.0)])


def _block_ones(n, blk):
    i = np.arange(n) // blk
    return jnp.asarray(i[:, None] == i[None, :], dtype=BF16)


def _pad_lanes(g, width):
    return jnp.pad(g, (0, width - g.shape[0]))


def _fold_gain(tab, gain, shift, scale):
    g = jnp.stack([gain, jnp.roll(gain, shift), jnp.roll(gain, -shift)]) * scale
    return tab * g[:, None, :]


def _layer_params(i, tab_b, tab_c, w_in, a_mu_prev, a_mu_next, a_w0, a_w_up, a_a0, a_a_up, a_k_k, a_k_a, a_r_k,
                  a_gn_g, a_gn_b, b_q_ln, b_kv_ln, b_w_uq, b_w_ukv, b_qn_g, b_kn_g,
                  c_qn_g, c_kn_g, c_sink, w_branch_out, w_out):
    w = w_in[i]
    c0 = 0
    segs = {}
    for name, width in (("za", A_COLS), ("ga", BW), ("zb", B_COLS), ("gb", BW),
                        ("zc", C_COLS), ("gc", BW), ("zg", N_BRANCH * D_MODEL)):
        segs[name] = w[:, c0:c0 + width]
        c0 += width
    zb = segs["zb"]
    zpad = lambda n: jnp.zeros((D_MODEL, n), w.dtype)
    segs["zb"] = jnp.concatenate(
        [zb[:, :B_Q_LORA + B_KV_LORA], zpad(B_NOPE), zb[:, B_Q_LORA + B_KV_LORA:], zpad(LANE - B_NOPE - B_ROPE)],
        axis=-1)
    w_in_p = jnp.concatenate([segs[n] for n, _ in IN_SEGS], axis=-1).astype(BF16)

    uq = b_w_uq[i].reshape(B_Q_LORA, B_HEADS, B_QK)
    uq = jnp.pad(uq, ((0, 0), (0, 0), (0, LANE - B_QK))).reshape(B_Q_LORA, B_HEADS * LANE)
    ukv = b_w_ukv[i].reshape(B_KV_LORA, B_HEADS, B_NOPE + B_V)
    uk = jnp.pad(ukv[:, :, :B_NOPE], ((0, 0), (0, 0), (0, LANE - B_NOPE))).reshape(B_KV_LORA, B_HEADS * LANE)
    uv = ukv[:, :, B_NOPE:].reshape(B_KV_LORA, BW)
    row = lambda t: t.reshape(1, -1)
    return dict(
        w_in=w_in_p,
        mu_prev=row(a_mu_prev[i]), mu_next=row(a_mu_next[i]),
        w0=a_w0[i], w_up=a_w_up[i], a0=a_a0[i], a_up=a_a_up[i],
        k_k=row(a_k_k[i]), k_a=row(a_k_a[i]), r_k=row(a_r_k[i]),
        gn_g=row(a_gn_g[i]), gn_b=row(a_gn_b[i]),
        q_ln=row(b_q_ln[i]), kv_ln=row(b_kv_ln[i]),
        w_uq=uq.astype(BF16), w_uk=uk.astype(BF16), w_uv=uv.astype(BF16),
        tab_bq=_fold_gain(tab_b, _pad_lanes(b_qn_g[i], LANE), B_ROPE // 4, B_QK ** -0.5),
        tab_bk=_fold_gain(tab_b, _pad_lanes(b_kn_g[i], LANE), B_ROPE // 4, 1.0),
        tab_cq=_fold_gain(tab_c, jnp.tile(c_qn_g[i], 2), C_HEAD // 4, C_HEAD ** -0.5),
        tab_ck=_fold_gain(tab_c, jnp.tile(c_kn_g[i], 2), C_HEAD // 4, 1.0),
        sink=c_sink[i],
        w_bo=w_branch_out[i].astype(BF16), w_out=w_out[i].astype(BF16),
    )


SCAN_PASSES = 1


def kernel(x, c, ctx, c_ctx, ada_w, ada_b, norm_g, w_in, a_mu_prev, a_mu_next, a_w0, a_w_up, a_a0, a_a_up, a_k_k, a_k_a, a_r_k, a_gn_g, a_gn_b, b_q_ln, b_kv_ln, b_w_uq, b_w_ukv, b_qn_g, b_kn_g, c_qn_g, c_kn_g, c_sink, w_branch_out, w_out):
    B, T, D = x.shape
    L = ctx.shape[1]
    depth = ada_w.shape[0]
    assert D == D_MODEL and L % TOK_TILE == 0 and T % TOK_TILE == 0 and T % GRID_W == 0

    mod_rows = -(-(B + 1) // 8) * 8
    cs = jnp.concatenate([c, c_ctx[None, :], jnp.zeros((mod_rows - B - 1, D), c.dtype)], axis=0)
    mod = _mod_call(cs, ada_w, ada_b)

    tab_b = _rope_tables(L, T, B_ROPE, LANE, B_NOPE)
    tab_c = _rope_tables(L, T, C_HEAD, C_HEAD, 0)
    ones_a = _block_ones(BW, A_HEAD)

    xall = jnp.concatenate([ctx, x], axis=1)
    for i in range(depth):
        with_ctx = i < depth - 1
        lp = _layer_params(i, tab_b, tab_c, w_in, a_mu_prev, a_mu_next, a_w0, a_w_up, a_a0, a_a_up, a_k_k, a_k_a,
                           a_r_k, a_gn_g, a_gn_b, b_q_ln, b_kv_ln, b_w_uq, b_w_ukv, b_qn_g, b_kn_g,
                           c_qn_g, c_kn_g, c_sink, w_branch_out, w_out)
        za, ga, zb, gb, zc, gc, zg = _inproj_call(xall, mod[i], norm_g[i], lp["w_in"], L)
        r, v, kk, lw, kd, nb, bonus = _prep_call(za, lp, ones_a, L)
        y0, y1 = _scan_call(r, v, kk, lw, kd, nb, L, SCAN_PASSES)
        qb, kb, vb, qc, kc, vc = _bcproj_call(zb, zc, lp)
        yb = _attb_call(qb, kb, vb, L, with_ctx)
        yc = _attc_call(qc, kc, vc, lp["sink"], L, with_ctx)
        xall = _merge_call(y0, y1, bonus, ga, yb, gb, yc, gc, zg, xall, mod[i], lp, ones_a, L, with_ctx)
    return xall
```

```python
import functools

import numpy as np
import jax
import jax.numpy as jnp
from jax import lax
from jax.experimental import pallas as pl
from jax.experimental.pallas import tpu as pltpu

F32 = jnp.float32
BF16 = jnp.bfloat16

D_MODEL = 1024
GRID_W = 64
ROPE_THETA = 10000.0
NORM_EPS = 1e-6
NEG = -1e30
BW = 512
N_BRANCH = 3

A_HEAD = 64
A_HEADS = BW // A_HEAD
A_LORA = 64
A_GN_EPS = 64e-5
A_COLS = 3 * BW + 4 * A_LORA

B_HEADS = 8
B_NOPE = 64
B_ROPE = 32
B_QK = B_NOPE + B_ROPE
B_V = BW // B_HEADS
B_Q_LORA = 256
B_KV_LORA = 128
B_COLS = B_Q_LORA + B_KV_LORA + B_ROPE
B_COLS_P = 512

C_HEAD = 64
C_HEADS = BW // C_HEAD
C_KV_HEADS = 2
C_GROUP = C_HEADS // C_KV_HEADS
C_KV_W = C_KV_HEADS * C_HEAD
C_COLS = BW + 2 * C_KV_W
WINDOW = 128

LANE = 128
MXU_W = 256
CHUNK = 64
TOK_TILE = 256
VMEM_LIMIT = 56 * 1024 * 1024

INPROJ_SUB = 3
ATTB_TQ = 512
LOG2E = 1.4426950408889634

IN_SEGS_MIX = (("za", A_COLS), ("zb", B_COLS_P), ("zc", C_COLS))
IN_SEGS_GATE = (("ga", BW), ("gb", BW), ("gc", BW), ("zg", N_BRANCH * D_MODEL))

NN = (((1,), (0,)), ((), ()))
NT = (((1,), (1,)), ((), ()))


def _dot(a, b, dims=NN):
    return lax.dot_general(a, b, dims, preferred_element_type=F32)


def _split(x):
    hi = x.astype(BF16)
    lo = (x - hi.astype(F32)).astype(BF16)
    return hi, lo


def _mm(a, b, dims=NN, passes=1):
    if passes == 1:
        return _dot(a.astype(BF16), b.astype(BF16), dims)
    ah, al = _split(a)
    bh, bl = _split(b)
    return _dot(ah, bh, dims) + (_dot(ah, bl, dims) + _dot(al, bh, dims))


def _mm_exact_rhs(a, b_bf16, parts=2):
    acc = None
    rem = a
    for _ in range(parts):
        p = rem.astype(BF16)
        t = _dot(p, b_bf16)
        acc = t if acc is None else acc + t
        rem = rem - p.astype(F32)
    return acc


def _head_sums(a, ones_blk):
    w = ones_blk.shape[0]
    return jnp.concatenate(
        [_mm_exact_rhs(a[:, c:c + w], ones_blk) for c in range(0, a.shape[1], w)], axis=1)


def _mm_exact_lhs(a_bf16, b, parts=3):
    acc = None
    rem = b
    for _ in range(parts):
        p = rem.astype(BF16)
        t = _dot(a_bf16, p)
        acc = t if acc is None else acc + t
        rem = rem - p.astype(F32)
    return acc


def _sigmoid(x):
    return 1.0 / (1.0 + jnp.exp(-x))


def _silu(x):
    return x * _sigmoid(x)


def _params(sem):
    return pltpu.CompilerParams(dimension_semantics=sem, vmem_limit_bytes=VMEM_LIMIT)


def _mod_kernel(c_ref, w_ref, b_ref, o_ref):
    s = _silu(c_ref[...])
    o_ref[0] = _mm(s, w_ref[0], passes=3) + b_ref[0]


def _mod_call(cs, ada_w, ada_b):
    n_layers = ada_w.shape[0]
    rows = cs.shape[0]
    return pl.pallas_call(
        _mod_kernel,
        out_shape=jax.ShapeDtypeStruct((n_layers, rows, 3 * D_MODEL), F32),
        grid=(n_layers, 3),
        in_specs=[
            pl.BlockSpec((rows, D_MODEL), lambda l, n: (0, 0)),
            pl.BlockSpec((1, D_MODEL, D_MODEL), lambda l, n: (l, 0, n)),
            pl.BlockSpec((1, 1, D_MODEL), lambda l, n: (l, 0, n)),
        ],
        out_specs=pl.BlockSpec((1, rows, D_MODEL), lambda l, n: (l, 0, n)),
        compiler_params=_params(("arbitrary", "arbitrary")),
        name="mod",
    )(cs, ada_w, ada_b.reshape(n_layers, 1, 3 * D_MODEL))


def _inproj_kernel(segs, n_sub, x_ref, *refs):
    mod_refs = refs[:2 * n_sub]
    g_ref, w_ref = refs[2 * n_sub:2 * n_sub + 2]
    o_refs = refs[2 * n_sub + 2:-1]
    hb_ref = refs[-1]
    for s in range(n_sub):
        rows = slice(s * TOK_TILE, (s + 1) * TOK_TILE)
        x = x_ref[0, rows, :]
        ms = jnp.mean(x * x, axis=-1, keepdims=True)
        h = x * lax.rsqrt(ms + NORM_EPS) * g_ref[...]
        h = h * (1.0 + mod_refs[2 * s + 1][0]) + mod_refs[2 * s][0]
        hb_ref[rows, :] = h.astype(BF16)
    hb = hb_ref[...]
    col = 0
    for (_, width), o_ref in zip(segs, o_refs):
        step = 512 if width % 512 == 0 else 256
        for c0 in range(0, width, step):
            o_ref[0, :, c0:c0 + step] = _dot(hb, w_ref[:, col + c0:col + c0 + step])
        col += width


def _mod_row(ctx_tiles, n_batch_rows):
    return lambda b, t: jnp.where(t < ctx_tiles, n_batch_rows, b)


def _inproj_call(xall, modl, norm_g, w_p, segs, ctx_len):
    B, S, _ = xall.shape
    n_sub = INPROJ_SUB
    tm = n_sub * TOK_TILE
    row = _mod_row(ctx_len // TOK_TILE, B)
    mod3 = modl.reshape(modl.shape[0], 1, 3 * D_MODEL)
    n_cols = sum(w for _, w in segs)
    mod_specs = []
    for s in range(n_sub):
        mod_specs.append(pl.BlockSpec((1, 1, D_MODEL), lambda b, t, s=s: (row(b, t * n_sub + s), 0, 0)))
        mod_specs.append(pl.BlockSpec((1, 1, D_MODEL), lambda b, t, s=s: (row(b, t * n_sub + s), 0, 1)))
    return pl.pallas_call(
        functools.partial(_inproj_kernel, segs, n_sub),
        out_shape=tuple(jax.ShapeDtypeStruct((B, S, w), F32) for _, w in segs),
        grid=(B, S // tm),
        in_specs=[pl.BlockSpec((1, tm, D_MODEL), lambda b, t: (b, t, 0))] + mod_specs + [
            pl.BlockSpec((1, D_MODEL), lambda b, t: (0, 0)),
            pl.BlockSpec((D_MODEL, n_cols), lambda b, t: (0, 0), pipeline_mode=pl.Buffered(1)),
        ],
        out_specs=tuple(pl.BlockSpec((1, tm, w), lambda b, t: (b, t, 0)) for _, w in segs),
        scratch_shapes=[pltpu.VMEM((tm, D_MODEL), BF16)],
        compiler_params=_params(("parallel", "arbitrary")),
        name="inproj",
    )(xall, *([mod3] * (2 * n_sub)), norm_g.reshape(1, D_MODEL), w_p)


def _prep_kernel(ctx_tiles, n_tiles,
                 z_ref, zp_ref, zn_ref, mup_ref, mun_ref, w0_ref, wup_ref, a0_ref, aup_ref,
                 kk_ref, ka_ref, rk_ref, ones_ref,
                 r_out, v_out, kk_out, lw_out, kd_out, nb_out, bonus_out):
    t = pl.program_id(1)
    tt = z_ref.shape[1]
    first = jnp.logical_or(t == 0, t == ctx_tiles)
    last = jnp.logical_or(t == ctx_tiles - 1, t == n_tiles - 1)
    z = z_ref[0]
    row = lax.broadcasted_iota(jnp.int32, (tt, 1), 0)
    halo_prev = jnp.where(first, 0.0, zp_ref[0, 7:8, :])
    halo_next = jnp.where(last, 0.0, zn_ref[0, 0:1, :])
    prev = jnp.where(row == 0, halo_prev, pltpu.roll(z, 1, 0))
    nxt = jnp.where(row == tt - 1, halo_next, pltpu.roll(z, tt - 1, 0))
    zsh = z + mup_ref[...] * (prev - z) + mun_ref[...] * (nxt - z)

    r = zsh[:, 0:BW]
    k = zsh[:, BW:2 * BW]
    v = zsh[:, 2 * BW:3 * BW]
    ones = ones_ref[...]
    kk = k * kk_ref[...]
    ss = _head_sums(kk * kk, ones)
    kk = kk * lax.rsqrt(jnp.maximum(ss, 1e-24))
    r_out[0] = r
    v_out[0] = v
    kk_out[0] = kk
    kd_sum = None
    for d in range(2):
        wd = zsh[:, 3 * BW + d * A_LORA:3 * BW + (d + 1) * A_LORA]
        ad = zsh[:, 3 * BW + 2 * A_LORA + d * A_LORA:3 * BW + 2 * A_LORA + (d + 1) * A_LORA]
        u = -(w0_ref[d:d + 1, :] + _mm(jnp.tanh(wd), wup_ref[d], passes=3))
        softplus = jnp.maximum(u, 0.0) + jnp.log(1.0 + jnp.exp(-jnp.abs(u)))
        w_log = -softplus - 0.5
        lw = -jnp.exp(w_log)
        a = _sigmoid(a0_ref[d:d + 1, :] + _mm(ad, aup_ref[d], passes=3))
        kd = k * (1.0 + (a - 1.0) * ka_ref[...])
        lw_out[d, 0] = lw
        kd_out[d, 0] = kd
        nb_out[d, 0] = -(kk * a)
        kd_sum = kd if kd_sum is None else kd_sum + kd
    bonus_out[0] = _head_sums(r * kd_sum * rk_ref[...], ones) * v


def _prep_call(za, lp, ones_a, ctx_len):
    B, S, _ = za.shape
    tt = TOK_TILE
    n_tiles = S // tt
    hb = tt // 8
    n8 = S // 8
    full = lambda shape: pl.BlockSpec(shape, lambda b, t: (0,) * len(shape))
    o3 = jax.ShapeDtypeStruct((B, S, BW), F32)
    o4 = jax.ShapeDtypeStruct((2, B, S, BW), F32)
    s3 = pl.BlockSpec((1, tt, BW), lambda b, t: (b, t, 0))
    s4 = pl.BlockSpec((2, 1, tt, BW), lambda b, t: (0, b, t, 0))
    return pl.pallas_call(
        functools.partial(_prep_kernel, ctx_len // tt, n_tiles),
        out_shape=(o3, o3, o3, o4, o4, o4, o3),
        grid=(B, n_tiles),
        in_specs=[
            pl.BlockSpec((1, tt, A_COLS), lambda b, t: (b, t, 0)),
            pl.BlockSpec((1, 8, A_COLS), lambda b, t: (b, jnp.maximum(t * hb - 1, 0), 0)),
            pl.BlockSpec((1, 8, A_COLS), lambda b, t: (b, jnp.minimum((t + 1) * hb, n8 - 1), 0)),
            full((1, A_COLS)), full((1, A_COLS)),
            full((2, BW)), full((2, A_LORA, BW)), full((2, BW)), full((2, A_LORA, BW)),
            full((1, BW)), full((1, BW)), full((1, BW)), full((MXU_W, MXU_W)),
        ],
        out_specs=(s3, s3, s3, s4, s4, s4, s3),
        compiler_params=_params(("parallel", "arbitrary")),
        name="prep",
    )(za, za, za, lp["mu_prev"], lp["mu_next"], lp["w0"], lp["w_up"], lp["a0"], lp["a_up"],
      lp["k_k"], lp["k_a"], lp["r_k"], ones_a)


def _scan_kernel(passes, fr, fv, fkk, flw, fkd, fnb, br, bv, bkk, blw, bkd, bnb,
                 yf_ref, yb_ref, state_ref):
    C = CHUNK

    @pl.when(pl.program_id(1) == 0)
    def _():
        state_ref[...] = jnp.zeros_like(state_ref)

    lane = lax.broadcasted_iota(jnp.int32, (1, LANE), 1)
    lo = lane < A_HEAD
    row2 = lax.broadcasted_iota(jnp.int32, (2 * C, 2 * C), 0)
    col2 = lax.broadcasted_iota(jnp.int32, (2 * C, 2 * C), 1)
    rows = lax.broadcasted_iota(jnp.int32, (C, C), 0)
    cols = lax.broadcasted_iota(jnp.int32, (C, C), 1)
    eye = (rows == cols).astype(F32)
    n_pairs = A_HEADS // 2

    dirs = []
    for d, (r_ref, v_ref, kk_ref, lw_ref, kd_ref, nb_ref) in enumerate(
            ((fr, fv, fkk, flw, fkd, fnb), (br, bv, bkk, blw, bkd, bnb))):
        order = (rows - cols) if d == 0 else (cols - rows)
        lw = lw_ref[0, 0]
        cum = _mm_exact_lhs((order >= 0).astype(BF16), lw)
        tot = cum[C - 1:C, :] if d == 0 else cum[0:1, :]
        p_inv = jnp.exp(-cum)
        p_tot = jnp.exp(tot - cum)
        nb = nb_ref[0, 0]
        kd = kd_ref[0, 0]
        kt = kk_ref[0] * jnp.exp(cum - lw)
        order2 = (row2 % C - col2 % C) if d == 0 else (col2 % C - row2 % C)
        keep = order2 >= jnp.where(row2 < C, 0, 1)
        e_rows = jnp.broadcast_to(jnp.exp(tot), (LANE, BW))
        dirs.append(dict(
            keep=keep, kt=kt, v=v_ref[0],
            lhs=jnp.concatenate([r_ref[0] * jnp.exp(cum), kt], axis=0),
            rhs=jnp.concatenate([nb * p_inv, kd * p_inv], axis=0),
            nkt=jnp.concatenate([nb * p_tot, kd * p_tot, e_rows], axis=0).T))

    units = [(d, h) for d in range(2) for h in range(A_HEADS)]
    slab = lambda h: slice((h // 2) * LANE, (h // 2 + 1) * LANE)
    own = lambda h: lo if h % 2 == 0 else jnp.logical_not(lo)

    lhs_m, top, bot = [], [], []
    for d, h in units:
        x = dirs[d]
        lm = jnp.where(own(h), x["lhs"][:, slab(h)], 0.0)
        qk = jnp.where(x["keep"], _mm(lm, x["rhs"][:, slab(h)], NT, passes), 0.0)
        lhs_m.append(lm)
        top.append(qk[0:C])
        bot.append(qk[C:2 * C])
    pw = [_mm(b[:, 0:C], b[:, 0:C], NN, passes) for b in bot]
    t_inv = [eye + b[:, 0:C] for b in bot]
    for _ in range(4):
        both = [_mm(jnp.concatenate([t, p], axis=0), p, NN, passes) for t, p in zip(t_inv, pw)]
        t_inv = [t + m[0:C] for t, m in zip(t_inv, both)]
        pw = [m[C:2 * C] for m in both]
    t_inv = [t + _mm(t, p, NN, passes) for t, p in zip(t_inv, pw)]
    wta = []
    for (d, h), t, b in zip(units, t_inv, bot):
        kt_s = dirs[d]["kt"][:, slab(h)]
        if h % 2:
            kt_s = pltpu.roll(kt_s, A_HEAD, 1)
        wta.append(_mm(t, jnp.where(lo, kt_s, b), NN, passes))
    pairs = [(d, j) for d in range(2) for j in range(n_pairs)]
    st = [state_ref[i] for i in range(len(pairs))]
    vp = [dirs[d]["v"][:, j * LANE:(j + 1) * LANE] for d, j in pairs]
    sv = [jnp.concatenate([s, v], axis=0) for s, v in zip(st, vp)]
    qt_u = [_mm(m, sv[i // 2], NN, passes) for i, m in enumerate(wta)]
    qt = [jnp.where(lo, qt_u[2 * i], qt_u[2 * i + 1]) for i in range(len(pairs))]
    qv = [jnp.concatenate([q, v], axis=0) for q, v in zip(qt, vp)]
    y_u, s_u = [], []
    for i, (d, h) in enumerate(units):
        p = i // 2
        y_l = jnp.concatenate([top[i], lhs_m[i][0:C]], axis=1)
        y_r = jnp.concatenate([qv[p], st[p], st[p]], axis=0)
        y_u.append(_mm(y_l, y_r, NN, passes))
        nk = dirs[d]["nkt"][h * A_HEAD:(h + 1) * A_HEAD]
        s_u.append(nk[:, 2 * C:] * st[p] + _mm(nk[:, 0:2 * C], qv[p], NN, passes))
    for p, (d, j) in enumerate(pairs):
        sl = slice(j * LANE, (j + 1) * LANE)
        (yf_ref if d == 0 else yb_ref)[0, :, sl] = jnp.where(lo, y_u[2 * p], y_u[2 * p + 1])
        state_ref[p] = jnp.where(lo, s_u[2 * p], s_u[2 * p + 1])


def _scan_call(r, v, kk, lw, kd, nb, ctx_len, passes):
    B, S, _ = r.shape
    C = CHUNK
    nc = S // C
    lc = ctx_len // C
    rev = lambda c: jnp.where(c < lc, lc - 1 - c, nc - 1 - (c - lc))
    f3 = pl.BlockSpec((1, C, BW), lambda b, c: (b, c, 0))
    b3 = pl.BlockSpec((1, C, BW), lambda b, c: (b, rev(c), 0))
    f4 = pl.BlockSpec((1, 1, C, BW), lambda b, c: (0, b, c, 0))
    b4 = pl.BlockSpec((1, 1, C, BW), lambda b, c: (1, b, rev(c), 0))
    out = jax.ShapeDtypeStruct((B, S, BW), F32)
    return pl.pallas_call(
        functools.partial(_scan_kernel, passes),
        out_shape=(out, out),
        grid=(B, nc),
        in_specs=[f3, f3, f3, f4, f4, f4, b3, b3, b3, b4, b4, b4],
        out_specs=(f3, b3),
        scratch_shapes=[pltpu.VMEM((A_HEADS, A_HEAD, LANE), F32)],
        compiler_params=_params(("parallel", "arbitrary")),
        name="scan",
    )(r, v, kk, lw, kd, nb, r, v, kk, lw, kd, nb)


def _rope(x, tab_ref, shift):
    return (x * tab_ref[0] + pltpu.roll(x, shift, 1) * tab_ref[1]
            + pltpu.roll(x, LANE - shift, 1) * tab_ref[2])


def _bcproj_kernel(zb_ref, zc_ref, tbq_ref, tbk_ref, tcq_ref, tck_ref, qln_ref, kvln_ref,
                   wuq_ref, wuk_ref, wuv_ref, onesb_ref, onesc_ref,
                   qb_out, kb_out, vb_out, qc_out, kc_out, vc_out):
    ones_b = onesb_ref[...]
    ones_c = onesc_ref[...]
    zb = zb_ref[0]
    cq = zb[:, 0:B_Q_LORA]
    ckv = zb[:, B_Q_LORA:B_Q_LORA + B_KV_LORA]
    kr_slab = zb[:, B_Q_LORA + B_KV_LORA:B_COLS_P]
    cqn = cq * lax.rsqrt(jnp.mean(cq * cq, axis=-1, keepdims=True) + NORM_EPS) * qln_ref[...]
    ckvn = ckv * lax.rsqrt(jnp.mean(ckv * ckv, axis=-1, keepdims=True) + NORM_EPS) * kvln_ref[...]
    q_all = _dot(cqn.astype(BF16), wuq_ref[...])
    k_all = _dot(ckvn.astype(BF16), wuk_ref[...])
    vb_out[0] = _dot(ckvn.astype(BF16), wuv_ref[...]).astype(BF16)
    for h in range(B_HEADS):
        sl = slice(h * LANE, (h + 1) * LANE)
        q = q_all[:, sl]
        rs = lax.rsqrt(_mm_exact_rhs(q * q, ones_b) * (1.0 / B_QK) + NORM_EPS)
        qb_out[0, :, sl] = (_rope(q, tbq_ref, B_ROPE // 4) * rs).astype(BF16)
        k = k_all[:, sl] + kr_slab
        rs = lax.rsqrt(_mm_exact_rhs(k * k, ones_b) * (1.0 / B_QK) + NORM_EPS)
        kb_out[0, :, sl] = (_rope(k, tbk_ref, B_ROPE // 4) * rs).astype(BF16)

    zc = zc_ref[0]
    lane = lax.broadcasted_iota(jnp.int32, (1, LANE), 1)
    lo = lane < C_HEAD

    def head_norm(x, tab_ref):
        ms = _mm_exact_rhs(x * x, ones_c) * (1.0 / C_HEAD)
        return _rope(x, tab_ref, C_HEAD // 4) * lax.rsqrt(ms + NORM_EPS)

    for j in range(C_HEADS // 2):
        x = head_norm(zc[:, j * LANE:(j + 1) * LANE], tcq_ref)
        xr = pltpu.roll(x, C_HEAD, 1)
        g = (2 * j) // C_GROUP
        for half in range(2):
            h = 2 * j + half
            src = x if half == g else xr
            keep = lo if g == 0 else jnp.logical_not(lo)
            qc_out[0, :, h * LANE:(h + 1) * LANE] = jnp.where(keep, src, 0.0).astype(BF16)
    kc_out[0] = head_norm(zc[:, BW:BW + C_KV_W], tck_ref).astype(BF16)
    vv = zc[:, BW + C_KV_W:BW + 2 * C_KV_W]
    vr = pltpu.roll(vv, C_HEAD, 1)
    vc_out[0, :, 0:LANE] = jnp.where(lo, vv, vr).astype(BF16)
    vc_out[0, :, LANE:2 * LANE] = jnp.where(lo, vr, vv).astype(BF16)


def _bcproj_call(zb, zc, lp):
    B, S, _ = zb.shape
    tt = TOK_TILE
    full = lambda shape: pl.BlockSpec(shape, lambda t, b: (0,) * len(shape))
    tok = lambda w: pl.BlockSpec((1, tt, w), lambda t, b: (b, t, 0))
    tab = pl.BlockSpec((3, tt, LANE), lambda t, b: (0, t, 0))
    shp = lambda w: jax.ShapeDtypeStruct((B, S, w), BF16)
    return pl.pallas_call(
        _bcproj_kernel,
        out_shape=(shp(B_HEADS * LANE), shp(B_HEADS * LANE), shp(BW),
                   shp(C_HEADS * LANE), shp(C_KV_W), shp(2 * LANE)),
        grid=(S // tt, B),
        in_specs=[tok(B_COLS_P), tok(C_COLS), tab, tab, tab, tab,
                  full((1, B_Q_LORA)), full((1, B_KV_LORA)),
                  full((B_Q_LORA, B_HEADS * LANE)), full((B_KV_LORA, B_HEADS * LANE)),
                  full((B_KV_LORA, BW)), full((LANE, LANE)), full((LANE, LANE))],
        out_specs=(tok(B_HEADS * LANE), tok(B_HEADS * LANE), tok(BW),
                   tok(C_HEADS * LANE), tok(C_KV_W), tok(2 * LANE)),
        compiler_params=_params(("parallel", "arbitrary")),
        name="bcproj",
    )(zb, zc, lp["tab_bq"], lp["tab_bk"], lp["tab_cq"], lp["tab_ck"],
      lp["q_ln"], lp["kv_ln"], lp["w_uq"], lp["w_uk"], lp["w_uv"],
      _block_ones(LANE, LANE), _block_ones(LANE, C_HEAD))


def _attb_kernel(q_ref, k_ref, v_ref, *rest):
    o_ref = rest[-1]
    lane = lax.broadcasted_iota(jnp.int32, (1, LANE), 1)
    lo = lane < B_V
    for j in range(B_HEADS // 2):
        outs = []
        for h in (2 * j, 2 * j + 1):
            sl = slice(h * LANE, (h + 1) * LANE)
            s = _dot(q_ref[:, sl], k_ref[0, :, sl], NT)
            m = jnp.max(s, axis=-1, keepdims=True)
            p = jnp.exp2(s - m)
            den = jnp.sum(p, axis=-1, keepdims=True)
            o = _dot(p.astype(BF16), v_ref[0, :, j * LANE:(j + 1) * LANE])
            outs.append(o / den)
        o_ref[:, j * LANE:(j + 1) * LANE] = jnp.where(lo, outs[0], outs[1])


def _row_block(n_rows, width, seq_len, row0):
    return pl.BlockSpec(
        (pl.Element(n_rows), pl.Element(width)),
        lambda b, t: (pl.multiple_of(b * seq_len + row0 + t * n_rows, TOK_TILE), 0))


def _attb_call(qb, kb, vb, row0, n_rows, tq, n_keys, prev=None):
    B, S, _ = qb.shape
    width = B_HEADS * LANE
    in_specs = [
        _row_block(tq, width, S, row0),
        pl.BlockSpec((1, n_keys, width), lambda b, t: (b, 0, 0)),
        pl.BlockSpec((1, n_keys, BW), lambda b, t: (b, 0, 0)),
    ]
    args = [qb.reshape(B * S, width), kb, vb]
    if prev is not None:
        in_specs.append(pl.BlockSpec(memory_space=pl.ANY))
        args.append(prev)
    return pl.pallas_call(
        _attb_kernel,
        out_shape=jax.ShapeDtypeStruct((B * S, BW), F32),
        grid=(B, n_rows // tq),
        in_specs=in_specs,
        out_specs=_row_block(tq, BW, S, row0),
        input_output_aliases={} if prev is None else {3: 0},
        compiler_params=_params(("parallel", "arbitrary")),
        name="attb",
    )(*args)


def _attc_kernel(tile0, ctx_len, n_lat, sink_ref, q_ref, k_ref, v_ref, valid_ref, o_ref):
    W = WINDOW
    ctx_tiles = ctx_len // W
    n_blk = n_lat // W
    t = pl.program_id(1) + tile0
    lane = lax.broadcasted_iota(jnp.int32, (1, LANE), 1)
    lo = lane < C_HEAD
    rows4 = lax.broadcasted_iota(jnp.int32, (C_GROUP * W, 1), 0)

    def attend(k_cat, v_cat, mask):
        for g in range(C_KV_HEADS):
            q_st = jnp.concatenate(
                [q_ref[0, :, h * LANE:(h + 1) * LANE] for h in range(g * C_GROUP, (g + 1) * C_GROUP)],
                axis=0)
            s = _dot(q_st, k_cat, NT)
            if mask is not None:
                s = jnp.where(jnp.concatenate([mask] * C_GROUP, axis=0) > 0.5, s, NEG)
            sk = jnp.zeros((C_GROUP * W, 1), F32)
            for i in range(C_GROUP):
                sk = jnp.where(rows4 // W == i, sink_ref[g * C_GROUP + i], sk)
            m = jnp.maximum(jnp.max(s, axis=-1, keepdims=True), sk)
            p = jnp.exp2(s - m)
            den = jnp.sum(p, axis=-1, keepdims=True) + jnp.exp2(sk - m)
            o = _dot(p.astype(BF16), v_cat[:, g * LANE:(g + 1) * LANE]) / den
            for jj in range(C_GROUP // 2):
                o_lo = o[(2 * jj) * W:(2 * jj + 1) * W]
                o_hi = o[(2 * jj + 1) * W:(2 * jj + 2) * W]
                col = (g * C_GROUP // 2 + jj) * LANE
                o_ref[0, :, col:col + LANE] = jnp.where(lo, o_lo, o_hi)

    def latent():
        j = t - ctx_tiles
        bm = jnp.maximum(j - 1, 0)
        bp = jnp.minimum(j + 1, n_blk - 1)

        def blk(ref, b):
            return ref[0, pl.ds(pl.multiple_of(ctx_len + b * W, W), W), :]

        k_cat = jnp.concatenate([k_ref[0, 0:ctx_len, :], blk(k_ref, bm), blk(k_ref, j), blk(k_ref, bp)], axis=0)
        v_cat = jnp.concatenate([v_ref[0, 0:ctx_len, :], blk(v_ref, bm), blk(v_ref, j), blk(v_ref, bp)], axis=0)
        attend(k_cat, v_cat, valid_ref[0])

    def context():
        attend(k_ref[0, 0:ctx_len, :], v_ref[0, 0:ctx_len, :], None)

    if tile0 < ctx_tiles:
        pl.when(t < ctx_tiles)(context)
        pl.when(t >= ctx_tiles)(latent)
    else:
        latent()


def _attc_call(qc, kc, vc, sink, ctx_len, with_ctx):
    B, S, _ = qc.shape
    W = WINDOW
    ctx_tiles = ctx_len // W
    tile0 = 0 if with_ctx else ctx_tiles
    n_tiles = S // W - tile0
    n_blk = (S - ctx_len) // W
    valid = _band_valid(ctx_len)

    def variant(t):
        j = t + tile0 - ctx_tiles
        return jnp.where(j == 0, 1, 0) + jnp.where(j == n_blk - 1, 2, 0)

    return pl.pallas_call(
        functools.partial(_attc_kernel, tile0, ctx_len, S - ctx_len),
        out_shape=jax.ShapeDtypeStruct((B, n_tiles * W, BW), F32),
        grid=(B, n_tiles),
        in_specs=[
            pl.BlockSpec(memory_space=pltpu.SMEM),
            pl.BlockSpec((1, W, C_HEADS * LANE), lambda b, t: (b, t + tile0, 0)),
            pl.BlockSpec((1, S, C_KV_W), lambda b, t: (b, 0, 0)),
            pl.BlockSpec((1, S, 2 * LANE), lambda b, t: (b, 0, 0)),
            pl.BlockSpec((1, W, ctx_len + 3 * W), lambda b, t: (variant(t), 0, 0)),
        ],
        out_specs=pl.BlockSpec((1, W, BW), lambda b, t: (b, t, 0)),
        compiler_params=_params(("parallel", "arbitrary")),
        name="attc",
    )(sink, qc, kc, vc, valid)


def _band_valid(ctx_len):
    W = WINDOW
    qi = np.arange(W)[:, None]
    cb = np.arange(ctx_len + 3 * W)[None, :] - ctx_len
    base = (cb < 0) | (np.abs(cb - W - qi) <= WINDOW)
    out = []
    for variant in range(4):
        first, last = bool(variant & 1), bool(variant & 2)
        ok = base.copy()
        if first:
            ok &= ~((cb >= 0) & (cb < W))
        if last:
            ok &= ~(cb >= 2 * W)
        out.append(ok)
    return jnp.asarray(np.stack(out), dtype=F32)


def _merge_kernel(y0_ref, y1_ref, bonus_ref, ga_ref, yb_ref, gb_ref, yc_ref, gc_ref, zg_ref,
                  x_ref, gate_ref, gng_ref, gnb_ref, ones_ref, wbo_ref, wout_ref, o_ref):
    ones = ones_ref[...]
    y = y0_ref[0] + y1_ref[0]
    mu = _head_sums(y, ones) * (1.0 / A_HEAD)
    dlt = y - mu
    var = _head_sums(dlt * dlt, ones) * (1.0 / A_HEAD)
    ya = dlt * lax.rsqrt(var + A_GN_EPS) * gng_ref[...] + gnb_ref[...] + bonus_ref[0]
    m = None
    for n, (y_n, g_ref) in enumerate(((ya, ga_ref), (yb_ref[0], gb_ref), (yc_ref[0], gc_ref))):
        u = (y_n * _silu(g_ref[0])).astype(BF16)
        gate_n = _sigmoid(zg_ref[0, :, n * D_MODEL:(n + 1) * D_MODEL])
        term = gate_n * _dot(u, wbo_ref[n])
        m = term if m is None else m + term
    o_ref[0] = x_ref[0] + gate_ref[0] * _dot(m.astype(BF16), wout_ref[...])


def _merge_call(y0, y1, bonus, ga, yb, gb, yc, gc, zg, xall, modl, lp, ones_a, ctx_len, with_ctx):
    B, S, _ = xall.shape
    tt = TOK_TILE
    ctx_tiles = ctx_len // tt
    tile0 = 0 if with_ctx else ctx_tiles
    n_tiles = S // tt - tile0
    row = _mod_row(ctx_tiles, B)
    mod3 = modl.reshape(modl.shape[0], 1, 3 * D_MODEL)
    full = lambda shape: pl.BlockSpec(shape, lambda b, t: (0,) * len(shape))
    tok = lambda w: pl.BlockSpec((1, tt, w), lambda b, t: (b, t + tile0, 0))
    att = lambda w: pl.BlockSpec((1, tt, w), lambda b, t: (b, t, 0))
    return pl.pallas_call(
        _merge_kernel,
        out_shape=jax.ShapeDtypeStruct((B, n_tiles * tt, D_MODEL), F32),
        grid=(B, n_tiles),
        in_specs=[tok(BW), tok(BW), tok(BW), tok(BW), tok(BW), tok(BW), att(BW), tok(BW),
                  tok(N_BRANCH * D_MODEL), tok(D_MODEL),
                  pl.BlockSpec((1, 1, D_MODEL), lambda b, t: (row(b, t + tile0), 0, 2)),
                  full((1, BW)), full((1, BW)), full((MXU_W, MXU_W)),
                  full((N_BRANCH, BW, D_MODEL)), full((D_MODEL, D_MODEL))],
        out_specs=pl.BlockSpec((1, tt, D_MODEL), lambda b, t: (b, t, 0)),
        compiler_params=_params(("parallel", "arbitrary")),
        name="merge",
    )(y0, y1, bonus, ga, yb, gb, yc, gc, zg, xall, mod3, lp["gn_g"], lp["gn_b"], ones_a,
      lp["w_bo"], lp["w_out"])


def _rope_tables(n_ctx, n_lat, rot_dim, head_w, lane0):
    rows = n_lat // GRID_W
    row = jnp.repeat(jnp.arange(rows), GRID_W).astype(F32)
    colp = jnp.tile(jnp.arange(GRID_W), rows).astype(F32)
    q = rot_dim // 4
    inv = ROPE_THETA ** (-(2.0 * jnp.arange(q, dtype=F32)) / (rot_dim // 2))
    ang = jnp.concatenate([row[:, None] * inv, colp[:, None] * inv], axis=-1)
    cos, sin = jnp.cos(ang), jnp.sin(ang)
    zero = jnp.zeros_like(sin[:, :q])
    cos_g = jnp.concatenate([cos[:, :q], cos[:, :q], cos[:, q:], cos[:, q:]], axis=-1)
    sp_g = jnp.concatenate([zero, sin[:, :q], zero, sin[:, q:]], axis=-1)
    sm_g = jnp.concatenate([-sin[:, :q], zero, -sin[:, q:], zero], axis=-1)

    def place(t, fill):
        pad_l = jnp.full((n_lat, lane0), fill, F32)
        pad_r = jnp.full((n_lat, head_w - lane0 - rot_dim), fill, F32)
        grp = jnp.concatenate([pad_l, t, pad_r], axis=-1)
        lat = jnp.tile(grp, (1, LANE // head_w))
        return jnp.concatenate([jnp.full((n_ctx, LANE), fill, F32), lat], axis=0)

    return jnp.stack([place(cos_g, 1.0), place(sp_g, 0.0), place(sm_g, 0.0)])


def _block_ones(n, blk):
    i = np.arange(n) // blk
    return jnp.asarray(i[:, None] == i[None, :], dtype=BF16)


def _pad_lanes(g, width):
    return jnp.pad(g, (0, width - g.shape[0]))


def _fold_gain(tab, gain, shift, scale):
    g = jnp.stack([gain, jnp.roll(gain, shift), jnp.roll(gain, -shift)]) * scale
    return tab * g[:, None, :]


def _layer_params(i, tab_b, tab_c, w_in, a_mu_prev, a_mu_next, a_w0, a_w_up, a_a0, a_a_up, a_k_k, a_k_a, a_r_k,
                  a_gn_g, a_gn_b, b_q_ln, b_kv_ln, b_w_uq, b_w_ukv, b_qn_g, b_kn_g,
                  c_qn_g, c_kn_g, c_sink, w_branch_out, w_out):
    w = w_in[i]
    c0 = 0
    segs = {}
    for name, width in (("za", A_COLS), ("ga", BW), ("zb", B_COLS), ("gb", BW),
                        ("zc", C_COLS), ("gc", BW), ("zg", N_BRANCH * D_MODEL)):
        segs[name] = w[:, c0:c0 + width]
        c0 += width
    zb = segs["zb"]
    zpad = lambda n: jnp.zeros((D_MODEL, n), w.dtype)
    segs["zb"] = jnp.concatenate(
        [zb[:, :B_Q_LORA + B_KV_LORA], zpad(B_NOPE), zb[:, B_Q_LORA + B_KV_LORA:], zpad(LANE - B_NOPE - B_ROPE)],
        axis=-1)
    w_mix = jnp.concatenate([segs[n] for n, _ in IN_SEGS_MIX], axis=-1).astype(BF16)
    w_gate = jnp.concatenate([segs[n] for n, _ in IN_SEGS_GATE], axis=-1).astype(BF16)

    uq = b_w_uq[i].reshape(B_Q_LORA, B_HEADS, B_QK)
    uq = jnp.pad(uq, ((0, 0), (0, 0), (0, LANE - B_QK))).reshape(B_Q_LORA, B_HEADS * LANE)
    ukv = b_w_ukv[i].reshape(B_KV_LORA, B_HEADS, B_NOPE + B_V)
    uk = jnp.pad(ukv[:, :, :B_NOPE], ((0, 0), (0, 0), (0, LANE - B_NOPE))).reshape(B_KV_LORA, B_HEADS * LANE)
    uv = ukv[:, :, B_NOPE:].reshape(B_KV_LORA, BW)
    row = lambda t: t.reshape(1, -1)
    return dict(
        w_mix=w_mix, w_gate=w_gate,
        mu_prev=row(a_mu_prev[i]), mu_next=row(a_mu_next[i]),
        w0=a_w0[i], w_up=a_w_up[i], a0=a_a0[i], a_up=a_a_up[i],
        k_k=row(a_k_k[i]), k_a=row(a_k_a[i]), r_k=row(a_r_k[i]),
        gn_g=row(a_gn_g[i]), gn_b=row(a_gn_b[i]),
        q_ln=row(b_q_ln[i]), kv_ln=row(b_kv_ln[i]),
        w_uq=uq.astype(BF16), w_uk=uk.astype(BF16), w_uv=uv.astype(BF16),
        tab_bq=_fold_gain(tab_b, _pad_lanes(b_qn_g[i], LANE), B_ROPE // 4, B_QK ** -0.5 * LOG2E),
        tab_bk=_fold_gain(tab_b, _pad_lanes(b_kn_g[i], LANE), B_ROPE // 4, 1.0),
        tab_cq=_fold_gain(tab_c, jnp.tile(c_qn_g[i], 2), C_HEAD // 4, C_HEAD ** -0.5 * LOG2E),
        tab_ck=_fold_gain(tab_c, jnp.tile(c_kn_g[i], 2), C_HEAD // 4, 1.0),
        sink=c_sink[i] * LOG2E,
        w_bo=w_branch_out[i].astype(BF16), w_out=w_out[i].astype(BF16),
    )


SCAN_PASSES = 1


def kernel(x, c, ctx, c_ctx, ada_w, ada_b, norm_g, w_in, a_mu_prev, a_mu_next, a_w0, a_w_up, a_a0, a_a_up, a_k_k, a_k_a, a_r_k, a_gn_g, a_gn_b, b_q_ln, b_kv_ln, b_w_uq, b_w_ukv, b_qn_g, b_kn_g, c_qn_g, c_kn_g, c_sink, w_branch_out, w_out):
    B, T, D = x.shape
    L = ctx.shape[1]
    depth = ada_w.shape[0]
    assert D == D_MODEL and L % TOK_TILE == 0 and T % TOK_TILE == 0 and T % GRID_W == 0
    assert (L + T) % (INPROJ_SUB * TOK_TILE) == 0 and T % ATTB_TQ == 0

    mod_rows = -(-(B + 1) // 8) * 8
    cs = jnp.concatenate([c, c_ctx[None, :], jnp.zeros((mod_rows - B - 1, D), c.dtype)], axis=0)
    mod = _mod_call(cs, ada_w, ada_b)

    tab_b = _rope_tables(L, T, B_ROPE, LANE, B_NOPE)
    tab_c = _rope_tables(L, T, C_HEAD, C_HEAD, 0)
    ones_a = _block_ones(MXU_W, A_HEAD)

    xall = jnp.concatenate([ctx, x], axis=1)
    for i in range(depth):
        with_ctx = i < depth - 1
        lp = _layer_params(i, tab_b, tab_c, w_in, a_mu_prev, a_mu_next, a_w0, a_w_up, a_a0, a_a_up, a_k_k, a_k_a,
                           a_r_k, a_gn_g, a_gn_b, b_q_ln, b_kv_ln, b_w_uq, b_w_ukv, b_qn_g, b_kn_g,
                           c_qn_g, c_kn_g, c_sink, w_branch_out, w_out)
        za, zb, zc = _inproj_call(xall, mod[i], norm_g[i], lp["w_mix"], IN_SEGS_MIX, L)
        ga, gb, gc, zg = _inproj_call(xall, mod[i], norm_g[i], lp["w_gate"], IN_SEGS_GATE, L)
        r, v, kk, lw, kd, nb, bonus = _prep_call(za, lp, ones_a, L)
        y0, y1 = _scan_call(r, v, kk, lw, kd, nb, L, SCAN_PASSES)
        qb, kb, vb, qc, kc, vc = _bcproj_call(zb, zc, lp)
        yb = _attb_call(qb, kb, vb, L, T, ATTB_TQ, L + T)
        if with_ctx:
            yb = _attb_call(qb, kb, vb, 0, L, TOK_TILE, L, prev=yb)
        yb = yb.reshape(B, L + T, BW)
        yc = _attc_call(qc, kc, vc, lp["sink"], L, with_ctx)
        xall = _merge_call(y0, y1, bonus, ga, yb, gb, yc, gc, zg, xall, mod[i], lp, ones_a, L, with_ctx)
    return xall
```

```python
import functools

import numpy as np
import jax
import jax.numpy as jnp
from jax import lax
from jax.experimental import pallas as pl
from jax.experimental.pallas import tpu as pltpu

F32 = jnp.float32
BF16 = jnp.bfloat16

D_MODEL = 1024
GRID_W = 64
ROPE_THETA = 10000.0
NORM_EPS = 1e-6
NEG = -1e30
BW = 512
N_BRANCH = 3

A_HEAD = 64
A_HEADS = BW // A_HEAD
A_LORA = 64
A_GN_EPS = 64e-5
A_COLS = 3 * BW + 4 * A_LORA

B_HEADS = 8
B_NOPE = 64
B_ROPE = 32
B_QK = B_NOPE + B_ROPE
B_V = BW // B_HEADS
B_Q_LORA = 256
B_KV_LORA = 128
B_COLS = B_Q_LORA + B_KV_LORA + B_ROPE
B_COLS_P = 512

C_HEAD = 64
C_HEADS = BW // C_HEAD
C_KV_HEADS = 2
C_GROUP = C_HEADS // C_KV_HEADS
C_KV_W = C_KV_HEADS * C_HEAD
C_COLS = BW + 2 * C_KV_W
WINDOW = 128

LANE = 128
MXU_W = 256
CHUNK = 64
SCAN_ROWS = 2
TOK_TILE = 256
VMEM_LIMIT = 56 * 1024 * 1024

INPROJ_SUB = 3
ATTB_TQ = 512
LOG2E = 1.4426950408889634

IN_SEGS_MIX = (("za", A_COLS), ("zb", B_COLS_P), ("zc", C_COLS))
IN_SEGS_GATE = (("ga", BW), ("gb", BW), ("gc", BW), ("zg", N_BRANCH * D_MODEL))

NN = (((1,), (0,)), ((), ()))
NT = (((1,), (1,)), ((), ()))


def _dot(a, b, dims=NN):
    return lax.dot_general(a, b, dims, preferred_element_type=F32)


def _split(x):
    hi = x.astype(BF16)
    lo = (x - hi.astype(F32)).astype(BF16)
    return hi, lo


def _mm(a, b, dims=NN, passes=1):
    if passes == 1:
        return _dot(a.astype(BF16), b.astype(BF16), dims)
    ah, al = _split(a)
    bh, bl = _split(b)
    return _dot(ah, bh, dims) + (_dot(ah, bl, dims) + _dot(al, bh, dims))


def _mm_exact_rhs(a, b_bf16, parts=2):
    acc = None
    rem = a
    for _ in range(parts):
        p = rem.astype(BF16)
        t = _dot(p, b_bf16)
        acc = t if acc is None else acc + t
        rem = rem - p.astype(F32)
    return acc


def _head_sums(a, ones_blk):
    w = ones_blk.shape[0]
    return jnp.concatenate(
        [_mm_exact_rhs(a[:, c:c + w], ones_blk) for c in range(0, a.shape[1], w)], axis=1)


def _mm_exact_lhs(a_bf16, b, parts=3):
    acc = None
    rem = b
    for _ in range(parts):
        p = rem.astype(BF16)
        t = _dot(a_bf16, p)
        acc = t if acc is None else acc + t
        rem = rem - p.astype(F32)
    return acc


def _sigmoid(x):
    return 1.0 / (1.0 + jnp.exp(-x))


def _silu(x):
    return x * _sigmoid(x)


def _modulated_norm(x, g, shift, scale):
    ms = jnp.mean(x * x, axis=-1, keepdims=True)
    return (x * lax.rsqrt(ms + NORM_EPS) * g) * (1.0 + scale) + shift


def _params(sem):
    return pltpu.CompilerParams(dimension_semantics=sem, vmem_limit_bytes=VMEM_LIMIT)


def _mod_kernel(c_ref, w_ref, b_ref, o_ref):
    s = _silu(c_ref[...])
    o_ref[0] = _mm(s, w_ref[0], passes=3) + b_ref[0]


def _mod_call(cs, ada_w, ada_b):
    n_layers = ada_w.shape[0]
    rows = cs.shape[0]
    return pl.pallas_call(
        _mod_kernel,
        out_shape=jax.ShapeDtypeStruct((n_layers, rows, 3 * D_MODEL), F32),
        grid=(n_layers, 3),
        in_specs=[
            pl.BlockSpec((rows, D_MODEL), lambda l, n: (0, 0)),
            pl.BlockSpec((1, D_MODEL, D_MODEL), lambda l, n: (l, 0, n)),
            pl.BlockSpec((1, 1, D_MODEL), lambda l, n: (l, 0, n)),
        ],
        out_specs=pl.BlockSpec((1, rows, D_MODEL), lambda l, n: (l, 0, n)),
        compiler_params=_params(("arbitrary", "arbitrary")),
        name="mod",
    )(cs, ada_w, ada_b.reshape(n_layers, 1, 3 * D_MODEL))


def _inproj_kernel(segs, n_sub, x_ref, *refs):
    mod_refs = refs[:2 * n_sub]
    g_ref, w_ref = refs[2 * n_sub:2 * n_sub + 2]
    o_refs = refs[2 * n_sub + 2:-1]
    hb_ref = refs[-1]
    for s in range(n_sub):
        rows = slice(s * TOK_TILE, (s + 1) * TOK_TILE)
        h = _modulated_norm(x_ref[0, rows, :], g_ref[...], mod_refs[2 * s][0], mod_refs[2 * s + 1][0])
        hb_ref[rows, :] = h.astype(BF16)
    hb = hb_ref[...]
    col = 0
    for (_, width), o_ref in zip(segs, o_refs):
        step = 512 if width % 512 == 0 else 256
        for c0 in range(0, width, step):
            o_ref[0, :, c0:c0 + step] = _dot(hb, w_ref[:, col + c0:col + c0 + step])
        col += width


def _mod_row(ctx_tiles, n_batch_rows):
    return lambda b, t: jnp.where(t < ctx_tiles, n_batch_rows, b)


def _inproj_call(xall, modl, norm_g, w_p, segs, ctx_len):
    B, S, _ = xall.shape
    n_sub = INPROJ_SUB
    tm = n_sub * TOK_TILE
    row = _mod_row(ctx_len // TOK_TILE, B)
    mod3 = modl.reshape(modl.shape[0], 1, 3 * D_MODEL)
    n_cols = sum(w for _, w in segs)
    mod_specs = []
    for s in range(n_sub):
        mod_specs.append(pl.BlockSpec((1, 1, D_MODEL), lambda b, t, s=s: (row(b, t * n_sub + s), 0, 0)))
        mod_specs.append(pl.BlockSpec((1, 1, D_MODEL), lambda b, t, s=s: (row(b, t * n_sub + s), 0, 1)))
    return pl.pallas_call(
        functools.partial(_inproj_kernel, segs, n_sub),
        out_shape=tuple(jax.ShapeDtypeStruct((B, S, w), F32) for _, w in segs),
        grid=(B, S // tm),
        in_specs=[pl.BlockSpec((1, tm, D_MODEL), lambda b, t: (b, t, 0))] + mod_specs + [
            pl.BlockSpec((1, D_MODEL), lambda b, t: (0, 0)),
            pl.BlockSpec((D_MODEL, n_cols), lambda b, t: (0, 0), pipeline_mode=pl.Buffered(1)),
        ],
        out_specs=tuple(pl.BlockSpec((1, tm, w), lambda b, t: (b, t, 0)) for _, w in segs),
        scratch_shapes=[pltpu.VMEM((tm, D_MODEL), BF16)],
        compiler_params=_params(("parallel", "arbitrary")),
        name="inproj",
    )(xall, *([mod3] * (2 * n_sub)), norm_g.reshape(1, D_MODEL), w_p)


def _prep_kernel(ctx_tiles, n_tiles,
                 z_ref, zp_ref, zn_ref, mup_ref, mun_ref, w0_ref, wup_ref, a0_ref, aup_ref,
                 kk_ref, ka_ref, rk_ref, ones_ref,
                 rvk_out, dir_out, bonus_out):
    t = pl.program_id(1)
    tt = z_ref.shape[1]
    first = jnp.logical_or(t == 0, t == ctx_tiles)
    last = jnp.logical_or(t == ctx_tiles - 1, t == n_tiles - 1)
    z = z_ref[0]
    row = lax.broadcasted_iota(jnp.int32, (tt, 1), 0)
    halo_prev = jnp.where(first, 0.0, zp_ref[0, 7:8, :])
    halo_next = jnp.where(last, 0.0, zn_ref[0, 0:1, :])
    prev = jnp.where(row == 0, halo_prev, pltpu.roll(z, 1, 0))
    nxt = jnp.where(row == tt - 1, halo_next, pltpu.roll(z, tt - 1, 0))
    zsh = z + mup_ref[...] * (prev - z) + mun_ref[...] * (nxt - z)

    r = zsh[:, 0:BW]
    k = zsh[:, BW:2 * BW]
    v = zsh[:, 2 * BW:3 * BW]
    ones = ones_ref[...]
    kk = k * kk_ref[...]
    ss = _head_sums(kk * kk, ones)
    kk = kk * lax.rsqrt(jnp.maximum(ss, 1e-24))
    rvk_out[0, :, 0:BW] = r
    rvk_out[0, :, BW:2 * BW] = v
    rvk_out[0, :, 2 * BW:3 * BW] = kk
    kd_sum = None
    for d in range(2):
        wd = zsh[:, 3 * BW + d * A_LORA:3 * BW + (d + 1) * A_LORA]
        ad = zsh[:, 3 * BW + 2 * A_LORA + d * A_LORA:3 * BW + 2 * A_LORA + (d + 1) * A_LORA]
        u = -(w0_ref[d:d + 1, :] + _mm(jnp.tanh(wd), wup_ref[d], passes=3))
        softplus = jnp.maximum(u, 0.0) + jnp.log(1.0 + jnp.exp(-jnp.abs(u)))
        w_log = -softplus - 0.5
        lw = -jnp.exp(w_log)
        a = _sigmoid(a0_ref[d:d + 1, :] + _mm(ad, aup_ref[d], passes=3))
        kd = k * (1.0 + (a - 1.0) * ka_ref[...])
        dir_out[d, 0, :, 0:BW] = lw
        dir_out[d, 0, :, BW:2 * BW] = kd
        dir_out[d, 0, :, 2 * BW:3 * BW] = -(kk * a)
        kd_sum = kd if kd_sum is None else kd_sum + kd
    bonus_out[0] = _head_sums(r * kd_sum * rk_ref[...], ones) * v


def _prep_call(za, lp, ones_a, ctx_len):
    B, S, _ = za.shape
    tt = TOK_TILE
    n_tiles = S // tt
    hb = tt // 8
    n8 = S // 8
    full = lambda shape: pl.BlockSpec(shape, lambda b, t: (0,) * len(shape))
    o3 = lambda w: jax.ShapeDtypeStruct((B, S, w), F32)
    s3 = lambda w: pl.BlockSpec((1, tt, w), lambda b, t: (b, t, 0))
    return pl.pallas_call(
        functools.partial(_prep_kernel, ctx_len // tt, n_tiles),
        out_shape=(o3(3 * BW), jax.ShapeDtypeStruct((2, B, S, 3 * BW), F32), o3(BW)),
        grid=(B, n_tiles),
        in_specs=[
            pl.BlockSpec((1, tt, A_COLS), lambda b, t: (b, t, 0)),
            pl.BlockSpec((1, 8, A_COLS), lambda b, t: (b, jnp.maximum(t * hb - 1, 0), 0)),
            pl.BlockSpec((1, 8, A_COLS), lambda b, t: (b, jnp.minimum((t + 1) * hb, n8 - 1), 0)),
            full((1, A_COLS)), full((1, A_COLS)),
            full((2, BW)), full((2, A_LORA, BW)), full((2, BW)), full((2, A_LORA, BW)),
            full((1, BW)), full((1, BW)), full((1, BW)), full((MXU_W, MXU_W)),
        ],
        out_specs=(s3(3 * BW), pl.BlockSpec((2, 1, tt, 3 * BW), lambda b, t: (0, b, t, 0)), s3(BW)),
        compiler_params=_params(("parallel", "arbitrary")),
        name="prep",
    )(za, za, za, lp["mu_prev"], lp["mu_next"], lp["w0"], lp["w_up"], lp["a0"], lp["a_up"],
      lp["k_k"], lp["k_a"], lp["r_k"], ones_a)


def _scan_kernel(passes, f_rvk, f_dir, b_rvk, b_dir, yf_ref, yb_ref, state_ref):
    C = CHUNK

    @pl.when(pl.program_id(1) == 0)
    def _():
        state_ref[...] = jnp.zeros_like(state_ref)

    lane = lax.broadcasted_iota(jnp.int32, (1, LANE), 1)
    lo = lane < A_HEAD
    row2 = lax.broadcasted_iota(jnp.int32, (2 * C, 2 * C), 0)
    col2 = lax.broadcasted_iota(jnp.int32, (2 * C, 2 * C), 1)
    rows = lax.broadcasted_iota(jnp.int32, (C, C), 0)
    cols = lax.broadcasted_iota(jnp.int32, (C, C), 1)
    eye = (rows == cols).astype(F32)
    n_pairs = A_HEADS // 2

    n_rows = f_rvk.shape[0]
    dirs = []
    for row, d in [(row, d) for row in range(n_rows) for d in range(2)]:
        rvk_ref, dir_ref = (f_rvk, f_dir) if d == 0 else (b_rvk, b_dir)
        order = (rows - cols) if d == 0 else (cols - rows)
        lw = dir_ref[0, row, :, 0:BW]
        kd = dir_ref[0, row, :, BW:2 * BW]
        nb = dir_ref[0, row, :, 2 * BW:3 * BW]
        cum = _mm_exact_lhs((order >= 0).astype(BF16), lw)
        tot = cum[C - 1:C, :] if d == 0 else cum[0:1, :]
        p_inv = jnp.exp(-cum)
        p_tot = jnp.exp(tot - cum)
        kt = rvk_ref[row, :, 2 * BW:3 * BW] * jnp.exp(cum - lw)
        order2 = (row2 % C - col2 % C) if d == 0 else (col2 % C - row2 % C)
        keep = order2 >= jnp.where(row2 < C, 0, 1)
        e_rows = jnp.broadcast_to(jnp.exp(tot), (LANE, BW))
        dirs.append(dict(
            keep=keep, kt=kt, v=rvk_ref[row, :, BW:2 * BW], y_ref=yf_ref if d == 0 else yb_ref, row=row,
            lhs=jnp.concatenate([rvk_ref[row, :, 0:BW] * jnp.exp(cum), kt], axis=0),
            rhs=jnp.concatenate([nb * p_inv, kd * p_inv], axis=0),
            nkt=jnp.concatenate([nb * p_tot, kd * p_tot, e_rows], axis=0).T))

    units = [(d, h) for d in range(len(dirs)) for h in range(A_HEADS)]
    slab = lambda h: slice((h // 2) * LANE, (h // 2 + 1) * LANE)
    own = lambda h: lo if h % 2 == 0 else jnp.logical_not(lo)

    lhs_m, top, bot = [], [], []
    for d, h in units:
        x = dirs[d]
        lm = jnp.where(own(h), x["lhs"][:, slab(h)], 0.0)
        qk = jnp.where(x["keep"], _mm(lm, x["rhs"][:, slab(h)], NT, passes), 0.0)
        lhs_m.append(lm)
        top.append(qk[0:C])
        bot.append(qk[C:2 * C])
    pw = [_mm(b[:, 0:C], b[:, 0:C], NN, passes) for b in bot]
    t_inv = [eye + b[:, 0:C] for b in bot]
    for _ in range(4):
        both = [_mm(jnp.concatenate([t, p], axis=0), p, NN, passes) for t, p in zip(t_inv, pw)]
        t_inv = [t + m[0:C] for t, m in zip(t_inv, both)]
        pw = [m[C:2 * C] for m in both]
    t_inv = [t + _mm(t, p, NN, passes) for t, p in zip(t_inv, pw)]
    wta = []
    for (d, h), t, b in zip(units, t_inv, bot):
        kt_s = dirs[d]["kt"][:, slab(h)]
        if h % 2:
            kt_s = pltpu.roll(kt_s, A_HEAD, 1)
        wta.append(_mm(t, jnp.where(lo, kt_s, b), NN, passes))
    pairs = [(d, j) for d in range(len(dirs)) for j in range(n_pairs)]
    st = [state_ref[i] for i in range(len(pairs))]
    vp = [dirs[d]["v"][:, j * LANE:(j + 1) * LANE] for d, j in pairs]
    sv = [jnp.concatenate([s, v], axis=0) for s, v in zip(st, vp)]
    qt_u = [_mm(m, sv[i // 2], NN, passes) for i, m in enumerate(wta)]
    qt = [jnp.where(lo, qt_u[2 * i], qt_u[2 * i + 1]) for i in range(len(pairs))]
    qv = [jnp.concatenate([q, v], axis=0) for q, v in zip(qt, vp)]
    y_u, s_u = [], []
    for i, (d, h) in enumerate(units):
        p = i // 2
        y_l = jnp.concatenate([top[i], lhs_m[i][0:C]], axis=1)
        y_r = jnp.concatenate([qv[p], st[p], st[p]], axis=0)
        y_u.append(_mm(y_l, y_r, NN, passes))
        nk = dirs[d]["nkt"][h * A_HEAD:(h + 1) * A_HEAD]
        s_u.append(nk[:, 2 * C:] * st[p] + _mm(nk[:, 0:2 * C], qv[p], NN, passes))
    for p, (d, j) in enumerate(pairs):
        sl = slice(j * LANE, (j + 1) * LANE)
        dirs[d]["y_ref"][dirs[d]["row"], :, sl] = jnp.where(lo, y_u[2 * p], y_u[2 * p + 1])
        state_ref[p] = jnp.where(lo, s_u[2 * p], s_u[2 * p + 1])


def _scan_call(rvk, dirp, ctx_len, passes):
    B, S, _ = rvk.shape
    C = CHUNK
    R = SCAN_ROWS
    nc = S // C
    lc = ctx_len // C
    rev = lambda c: jnp.where(c < lc, lc - 1 - c, nc - 1 - (c - lc))
    f3 = lambda w: pl.BlockSpec((R, C, w), lambda b, c: (b, c, 0))
    b3 = lambda w: pl.BlockSpec((R, C, w), lambda b, c: (b, rev(c), 0))
    f4 = pl.BlockSpec((1, R, C, 3 * BW), lambda b, c: (0, b, c, 0))
    b4 = pl.BlockSpec((1, R, C, 3 * BW), lambda b, c: (1, b, rev(c), 0))
    out = jax.ShapeDtypeStruct((B, S, BW), F32)
    return pl.pallas_call(
        functools.partial(_scan_kernel, passes),
        out_shape=(out, out),
        grid=(B // R, nc),
        in_specs=[f3(3 * BW), f4, b3(3 * BW), b4],
        out_specs=(f3(BW), b3(BW)),
        scratch_shapes=[pltpu.VMEM((R * A_HEADS, A_HEAD, LANE), F32)],
        compiler_params=_params(("parallel", "arbitrary")),
        name="scan",
    )(rvk, dirp, rvk, dirp)


def _rope(x, tab_ref, shift):
    return (x * tab_ref[0] + pltpu.roll(x, shift, 1) * tab_ref[1]
            + pltpu.roll(x, LANE - shift, 1) * tab_ref[2])


def _bcproj_kernel(zb_ref, zc_ref, tbq_ref, tbk_ref, tcq_ref, tck_ref, qln_ref, kvln_ref,
                   wuq_ref, wuk_ref, wuv_ref, onesb_ref, onesc_ref,
                   qb_out, kb_out, vb_out, qc_out, kc_out, vc_out):
    ones_b = onesb_ref[...]
    ones_c = onesc_ref[...]
    zb = zb_ref[0]
    cq = zb[:, 0:B_Q_LORA]
    ckv = zb[:, B_Q_LORA:B_Q_LORA + B_KV_LORA]
    kr_slab = zb[:, B_Q_LORA + B_KV_LORA:B_COLS_P]
    cqn = cq * lax.rsqrt(jnp.mean(cq * cq, axis=-1, keepdims=True) + NORM_EPS) * qln_ref[...]
    ckvn = ckv * lax.rsqrt(jnp.mean(ckv * ckv, axis=-1, keepdims=True) + NORM_EPS) * kvln_ref[...]
    q_all = _dot(cqn.astype(BF16), wuq_ref[...])
    k_all = _dot(ckvn.astype(BF16), wuk_ref[...])
    vb_out[0] = _dot(ckvn.astype(BF16), wuv_ref[...]).astype(BF16)
    for h in range(B_HEADS):
        sl = slice(h * LANE, (h + 1) * LANE)
        q = q_all[:, sl]
        rs = lax.rsqrt(_mm_exact_rhs(q * q, ones_b) * (1.0 / B_QK) + NORM_EPS)
        qb_out[0, :, sl] = (_rope(q, tbq_ref, B_ROPE // 4) * rs).astype(BF16)
        k = k_all[:, sl] + kr_slab
        rs = lax.rsqrt(_mm_exact_rhs(k * k, ones_b) * (1.0 / B_QK) + NORM_EPS)
        kb_out[0, :, sl] = (_rope(k, tbk_ref, B_ROPE // 4) * rs).astype(BF16)

    zc = zc_ref[0]
    lane = lax.broadcasted_iota(jnp.int32, (1, LANE), 1)
    lo = lane < C_HEAD

    def head_norm(x, tab_ref):
        ms = _mm_exact_rhs(x * x, ones_c) * (1.0 / C_HEAD)
        return _rope(x, tab_ref, C_HEAD // 4) * lax.rsqrt(ms + NORM_EPS)

    for j in range(C_HEADS // 2):
        x = head_norm(zc[:, j * LANE:(j + 1) * LANE], tcq_ref)
        xr = pltpu.roll(x, C_HEAD, 1)
        g = (2 * j) // C_GROUP
        for half in range(2):
            h = 2 * j + half
            src = x if half == g else xr
            keep = lo if g == 0 else jnp.logical_not(lo)
            qc_out[0, :, h * LANE:(h + 1) * LANE] = jnp.where(keep, src, 0.0).astype(BF16)
    kc_out[0] = head_norm(zc[:, BW:BW + C_KV_W], tck_ref).astype(BF16)
    vv = zc[:, BW + C_KV_W:BW + 2 * C_KV_W]
    vr = pltpu.roll(vv, C_HEAD, 1)
    vc_out[0, :, 0:LANE] = jnp.where(lo, vv, vr).astype(BF16)
    vc_out[0, :, LANE:2 * LANE] = jnp.where(lo, vr, vv).astype(BF16)


def _bcproj_call(zb, zc, lp):
    B, S, _ = zb.shape
    tt = TOK_TILE
    full = lambda shape: pl.BlockSpec(shape, lambda t, b: (0,) * len(shape))
    tok = lambda w: pl.BlockSpec((1, tt, w), lambda t, b: (b, t, 0))
    tab = pl.BlockSpec((3, tt, LANE), lambda t, b: (0, t, 0))
    shp = lambda w: jax.ShapeDtypeStruct((B, S, w), BF16)
    return pl.pallas_call(
        _bcproj_kernel,
        out_shape=(shp(B_HEADS * LANE), shp(B_HEADS * LANE), shp(BW),
                   shp(C_HEADS * LANE), shp(C_KV_W), shp(2 * LANE)),
        grid=(S // tt, B),
        in_specs=[tok(B_COLS_P), tok(C_COLS), tab, tab, tab, tab,
                  full((1, B_Q_LORA)), full((1, B_KV_LORA)),
                  full((B_Q_LORA, B_HEADS * LANE)), full((B_KV_LORA, B_HEADS * LANE)),
                  full((B_KV_LORA, BW)), full((LANE, LANE)), full((LANE, LANE))],
        out_specs=(tok(B_HEADS * LANE), tok(B_HEADS * LANE), tok(BW),
                   tok(C_HEADS * LANE), tok(C_KV_W), tok(2 * LANE)),
        compiler_params=_params(("parallel", "arbitrary")),
        name="bcproj",
    )(zb, zc, lp["tab_bq"], lp["tab_bk"], lp["tab_cq"], lp["tab_ck"],
      lp["q_ln"], lp["kv_ln"], lp["w_uq"], lp["w_uk"], lp["w_uv"],
      _block_ones(LANE, LANE), _block_ones(LANE, C_HEAD))


def _attb_kernel(q_ref, k_ref, v_ref, *rest):
    o_ref = rest[-1]
    lane = lax.broadcasted_iota(jnp.int32, (1, LANE), 1)
    lo = lane < B_V
    for j in range(B_HEADS // 2):
        outs = []
        for h in (2 * j, 2 * j + 1):
            sl = slice(h * LANE, (h + 1) * LANE)
            s = _dot(q_ref[:, sl], k_ref[0, :, sl], NT)
            m = jnp.max(s, axis=-1, keepdims=True)
            p = jnp.exp2(s - m)
            den = jnp.sum(p, axis=-1, keepdims=True)
            o = _dot(p.astype(BF16), v_ref[0, :, j * LANE:(j + 1) * LANE])
            outs.append(o / den)
        o_ref[:, j * LANE:(j + 1) * LANE] = jnp.where(lo, outs[0], outs[1])


def _row_block(n_rows, width, seq_len, row0):
    return pl.BlockSpec(
        (pl.Element(n_rows), pl.Element(width)),
        lambda b, t: (pl.multiple_of(b * seq_len + row0 + t * n_rows, TOK_TILE), 0))


def _attb_call(qb, kb, vb, row0, n_rows, tq, n_keys, prev=None):
    B, S, _ = qb.shape
    width = B_HEADS * LANE
    in_specs = [
        _row_block(tq, width, S, row0),
        pl.BlockSpec((1, n_keys, width), lambda b, t: (b, 0, 0)),
        pl.BlockSpec((1, n_keys, BW), lambda b, t: (b, 0, 0)),
    ]
    args = [qb.reshape(B * S, width), kb, vb]
    if prev is not None:
        in_specs.append(pl.BlockSpec(memory_space=pl.ANY))
        args.append(prev)
    return pl.pallas_call(
        _attb_kernel,
        out_shape=jax.ShapeDtypeStruct((B * S, BW), F32),
        grid=(B, n_rows // tq),
        in_specs=in_specs,
        out_specs=_row_block(tq, BW, S, row0),
        input_output_aliases={} if prev is None else {3: 0},
        compiler_params=_params(("parallel", "arbitrary")),
        name="attb",
    )(*args)


def _attc_kernel(tile0, ctx_len, n_lat, sink_ref, q_ref, k_ref, v_ref, valid_ref, o_ref):
    W = WINDOW
    ctx_tiles = ctx_len // W
    n_blk = n_lat // W
    t = pl.program_id(1) + tile0
    lane = lax.broadcasted_iota(jnp.int32, (1, LANE), 1)
    lo = lane < C_HEAD
    rows4 = lax.broadcasted_iota(jnp.int32, (C_GROUP * W, 1), 0)

    def attend(k_cat, v_cat, mask):
        for g in range(C_KV_HEADS):
            q_st = jnp.concatenate(
                [q_ref[0, :, h * LANE:(h + 1) * LANE] for h in range(g * C_GROUP, (g + 1) * C_GROUP)],
                axis=0)
            s = _dot(q_st, k_cat, NT)
            if mask is not None:
                s = jnp.where(jnp.concatenate([mask] * C_GROUP, axis=0) > 0.5, s, NEG)
            sk = jnp.zeros((C_GROUP * W, 1), F32)
            for i in range(C_GROUP):
                sk = jnp.where(rows4 // W == i, sink_ref[g * C_GROUP + i], sk)
            m = jnp.maximum(jnp.max(s, axis=-1, keepdims=True), sk)
            p = jnp.exp2(s - m)
            den = jnp.sum(p, axis=-1, keepdims=True) + jnp.exp2(sk - m)
            o = _dot(p.astype(BF16), v_cat[:, g * LANE:(g + 1) * LANE]) / den
            for jj in range(C_GROUP // 2):
                o_lo = o[(2 * jj) * W:(2 * jj + 1) * W]
                o_hi = o[(2 * jj + 1) * W:(2 * jj + 2) * W]
                col = (g * C_GROUP // 2 + jj) * LANE
                o_ref[0, :, col:col + LANE] = jnp.where(lo, o_lo, o_hi)

    def latent():
        j = t - ctx_tiles
        bm = jnp.maximum(j - 1, 0)
        bp = jnp.minimum(j + 1, n_blk - 1)

        def blk(ref, b):
            return ref[0, pl.ds(pl.multiple_of(ctx_len + b * W, W), W), :]

        k_cat = jnp.concatenate([k_ref[0, 0:ctx_len, :], blk(k_ref, bm), blk(k_ref, j), blk(k_ref, bp)], axis=0)
        v_cat = jnp.concatenate([v_ref[0, 0:ctx_len, :], blk(v_ref, bm), blk(v_ref, j), blk(v_ref, bp)], axis=0)
        attend(k_cat, v_cat, valid_ref[0])

    def context():
        attend(k_ref[0, 0:ctx_len, :], v_ref[0, 0:ctx_len, :], None)

    if tile0 < ctx_tiles:
        pl.when(t < ctx_tiles)(context)
        pl.when(t >= ctx_tiles)(latent)
    else:
        latent()


def _attc_call(qc, kc, vc, sink, ctx_len, with_ctx):
    B, S, _ = qc.shape
    W = WINDOW
    ctx_tiles = ctx_len // W
    tile0 = 0 if with_ctx else ctx_tiles
    n_tiles = S // W - tile0
    n_blk = (S - ctx_len) // W
    valid = _band_valid(ctx_len)

    def variant(t):
        j = t + tile0 - ctx_tiles
        return jnp.where(j == 0, 1, 0) + jnp.where(j == n_blk - 1, 2, 0)

    return pl.pallas_call(
        functools.partial(_attc_kernel, tile0, ctx_len, S - ctx_len),
        out_shape=jax.ShapeDtypeStruct((B, n_tiles * W, BW), F32),
        grid=(B, n_tiles),
        in_specs=[
            pl.BlockSpec(memory_space=pltpu.SMEM),
            pl.BlockSpec((1, W, C_HEADS * LANE), lambda b, t: (b, t + tile0, 0)),
            pl.BlockSpec((1, S, C_KV_W), lambda b, t: (b, 0, 0)),
            pl.BlockSpec((1, S, 2 * LANE), lambda b, t: (b, 0, 0)),
            pl.BlockSpec((1, W, ctx_len + 3 * W), lambda b, t: (variant(t), 0, 0)),
        ],
        out_specs=pl.BlockSpec((1, W, BW), lambda b, t: (b, t, 0)),
        compiler_params=_params(("parallel", "arbitrary")),
        name="attc",
    )(sink, qc, kc, vc, valid)


def _band_valid(ctx_len):
    W = WINDOW
    qi = np.arange(W)[:, None]
    cb = np.arange(ctx_len + 3 * W)[None, :] - ctx_len
    base = (cb < 0) | (np.abs(cb - W - qi) <= WINDOW)
    out = []
    for variant in range(4):
        first, last = bool(variant & 1), bool(variant & 2)
        ok = base.copy()
        if first:
            ok &= ~((cb >= 0) & (cb < W))
        if last:
            ok &= ~(cb >= 2 * W)
        out.append(ok)
    return jnp.asarray(np.stack(out), dtype=F32)


def _merge_kernel(y0_ref, y1_ref, bonus_ref, yb_ref, yc_ref, x_ref, shift_ref, scale_ref, gate_ref,
                  g_ref, wg_ref, gng_ref, gnb_ref, ones_ref, wbo_ref, wout_ref, o_ref):
    x = x_ref[0]
    hb = _modulated_norm(x, g_ref[...], shift_ref[0], scale_ref[0]).astype(BF16)
    ones = ones_ref[...]
    y = y0_ref[0] + y1_ref[0]
    mu = _head_sums(y, ones) * (1.0 / A_HEAD)
    dlt = y - mu
    var = _head_sums(dlt * dlt, ones) * (1.0 / A_HEAD)
    ya = dlt * lax.rsqrt(var + A_GN_EPS) * gng_ref[...] + gnb_ref[...] + bonus_ref[0]
    m = None
    zg0 = N_BRANCH * BW
    for n, y_n in enumerate((ya, yb_ref[0], yc_ref[0])):
        g = _dot(hb, wg_ref[:, n * BW:(n + 1) * BW])
        u = (y_n * _silu(g)).astype(BF16)
        zg = _dot(hb, wg_ref[:, zg0 + n * D_MODEL:zg0 + (n + 1) * D_MODEL])
        term = _sigmoid(zg) * _dot(u, wbo_ref[n])
        m = term if m is None else m + term
    o_ref[0] = x + gate_ref[0] * _dot(m.astype(BF16), wout_ref[...])


def _merge_call(y0, y1, bonus, yb, yc, xall, modl, norm_g, lp, ones_a, ctx_len, with_ctx):
    B, S, _ = xall.shape
    tt = TOK_TILE
    ctx_tiles = ctx_len // tt
    tile0 = 0 if with_ctx else ctx_tiles
    n_tiles = S // tt - tile0
    row = _mod_row(ctx_tiles, B)
    mod3 = modl.reshape(modl.shape[0], 1, 3 * D_MODEL)
    mod = lambda col: pl.BlockSpec((1, 1, D_MODEL), lambda b, t: (row(b, t + tile0), 0, col))
    once = lambda shape: pl.BlockSpec(shape, lambda b, t: (0,) * len(shape), pipeline_mode=pl.Buffered(1))
    tok = lambda w: pl.BlockSpec((1, tt, w), lambda b, t: (b, t + tile0, 0))
    att = lambda w: pl.BlockSpec((1, tt, w), lambda b, t: (b, t, 0))
    n_gate = sum(w for _, w in IN_SEGS_GATE)
    return pl.pallas_call(
        _merge_kernel,
        out_shape=jax.ShapeDtypeStruct((B, n_tiles * tt, D_MODEL), F32),
        grid=(B, n_tiles),
        in_specs=[tok(BW), tok(BW), tok(BW), tok(BW), att(BW), tok(D_MODEL),
                  mod(0), mod(1), mod(2),
                  once((1, D_MODEL)), once((D_MODEL, n_gate)),
                  once((1, BW)), once((1, BW)), once((MXU_W, MXU_W)),
                  once((N_BRANCH, BW, D_MODEL)), once((D_MODEL, D_MODEL))],
        out_specs=pl.BlockSpec((1, tt, D_MODEL), lambda b, t: (b, t, 0)),
        compiler_params=_params(("parallel", "arbitrary")),
        name="merge",
    )(y0, y1, bonus, yb, yc, xall, mod3, mod3, mod3, norm_g.reshape(1, D_MODEL), lp["w_gate"],
      lp["gn_g"], lp["gn_b"], ones_a, lp["w_bo"], lp["w_out"])


def _rope_tables(n_ctx, n_lat, rot_dim, head_w, lane0):
    rows = n_lat // GRID_W
    row = jnp.repeat(jnp.arange(rows), GRID_W).astype(F32)
    colp = jnp.tile(jnp.arange(GRID_W), rows).astype(F32)
    q = rot_dim // 4
    inv = ROPE_THETA ** (-(2.0 * jnp.arange(q, dtype=F32)) / (rot_dim // 2))
    ang = jnp.concatenate([row[:, None] * inv, colp[:, None] * inv], axis=-1)
    cos, sin = jnp.cos(ang), jnp.sin(ang)
    zero = jnp.zeros_like(sin[:, :q])
    cos_g = jnp.concatenate([cos[:, :q], cos[:, :q], cos[:, q:], cos[:, q:]], axis=-1)
    sp_g = jnp.concatenate([zero, sin[:, :q], zero, sin[:, q:]], axis=-1)
    sm_g = jnp.concatenate([-sin[:, :q], zero, -sin[:, q:], zero], axis=-1)

    def place(t, fill):
        pad_l = jnp.full((n_lat, lane0), fill, F32)
        pad_r = jnp.full((n_lat, head_w - lane0 - rot_dim), fill, F32)
        grp = jnp.concatenate([pad_l, t, pad_r], axis=-1)
        lat = jnp.tile(grp, (1, LANE // head_w))
        return jnp.concatenate([jnp.full((n_ctx, LANE), fill, F32), lat], axis=0)

    return jnp.stack([place(cos_g, 1.0), place(sp_g, 0.0), place(sm_g, 0.0)])


def _block_ones(n, blk):
    i = np.arange(n) // blk
    return jnp.asarray(i[:, None] == i[None, :], dtype=BF16)


def _pad_lanes(g, width):
    return jnp.pad(g, (0, width - g.shape[0]))


def _fold_gain(tab, gain, shift, scale):
    g = jnp.stack([gain, jnp.roll(gain, shift), jnp.roll(gain, -shift)]) * scale
    return tab * g[:, None, :]


def _layer_params(i, tab_b, tab_c, w_in, a_mu_prev, a_mu_next, a_w0, a_w_up, a_a0, a_a_up, a_k_k, a_k_a, a_r_k,
                  a_gn_g, a_gn_b, b_q_ln, b_kv_ln, b_w_uq, b_w_ukv, b_qn_g, b_kn_g,
                  c_qn_g, c_kn_g, c_sink, w_branch_out, w_out):
    w = w_in[i]
    c0 = 0
    segs = {}
    for name, width in (("za", A_COLS), ("ga", BW), ("zb", B_COLS), ("gb", BW),
                        ("zc", C_COLS), ("gc", BW), ("zg", N_BRANCH * D_MODEL)):
        segs[name] = w[:, c0:c0 + width]
        c0 += width
    zb = segs["zb"]
    zpad = lambda n: jnp.zeros((D_MODEL, n), w.dtype)
    segs["zb"] = jnp.concatenate(
        [zb[:, :B_Q_LORA + B_KV_LORA], zpad(B_NOPE), zb[:, B_Q_LORA + B_KV_LORA:], zpad(LANE - B_NOPE - B_ROPE)],
        axis=-1)
    w_mix = jnp.concatenate([segs[n] for n, _ in IN_SEGS_MIX], axis=-1).astype(BF16)
    w_gate = jnp.concatenate([segs[n] for n, _ in IN_SEGS_GATE], axis=-1).astype(BF16)

    uq = b_w_uq[i].reshape(B_Q_LORA, B_HEADS, B_QK)
    uq = jnp.pad(uq, ((0, 0), (0, 0), (0, LANE - B_QK))).reshape(B_Q_LORA, B_HEADS * LANE)
    ukv = b_w_ukv[i].reshape(B_KV_LORA, B_HEADS, B_NOPE + B_V)
    uk = jnp.pad(ukv[:, :, :B_NOPE], ((0, 0), (0, 0), (0, LANE - B_NOPE))).reshape(B_KV_LORA, B_HEADS * LANE)
    uv = ukv[:, :, B_NOPE:].reshape(B_KV_LORA, BW)
    row = lambda t: t.reshape(1, -1)
    return dict(
        w_mix=w_mix, w_gate=w_gate,
        mu_prev=row(a_mu_prev[i]), mu_next=row(a_mu_next[i]),
        w0=a_w0[i], w_up=a_w_up[i], a0=a_a0[i], a_up=a_a_up[i],
        k_k=row(a_k_k[i]), k_a=row(a_k_a[i]), r_k=row(a_r_k[i]),
        gn_g=row(a_gn_g[i]), gn_b=row(a_gn_b[i]),
        q_ln=row(b_q_ln[i]), kv_ln=row(b_kv_ln[i]),
        w_uq=uq.astype(BF16), w_uk=uk.astype(BF16), w_uv=uv.astype(BF16),
        tab_bq=_fold_gain(tab_b, _pad_lanes(b_qn_g[i], LANE), B_ROPE // 4, B_QK ** -0.5 * LOG2E),
        tab_bk=_fold_gain(tab_b, _pad_lanes(b_kn_g[i], LANE), B_ROPE // 4, 1.0),
        tab_cq=_fold_gain(tab_c, jnp.tile(c_qn_g[i], 2), C_HEAD // 4, C_HEAD ** -0.5 * LOG2E),
        tab_ck=_fold_gain(tab_c, jnp.tile(c_kn_g[i], 2), C_HEAD // 4, 1.0),
        sink=c_sink[i] * LOG2E,
        w_bo=w_branch_out[i].astype(BF16), w_out=w_out[i].astype(BF16),
    )


SCAN_PASSES = 1


def kernel(x, c, ctx, c_ctx, ada_w, ada_b, norm_g, w_in, a_mu_prev, a_mu_next, a_w0, a_w_up, a_a0, a_a_up, a_k_k, a_k_a, a_r_k, a_gn_g, a_gn_b, b_q_ln, b_kv_ln, b_w_uq, b_w_ukv, b_qn_g, b_kn_g, c_qn_g, c_kn_g, c_sink, w_branch_out, w_out):
    B, T, D = x.shape
    L = ctx.shape[1]
    depth = ada_w.shape[0]
    assert D == D_MODEL and L % TOK_TILE == 0 and T % TOK_TILE == 0 and T % GRID_W == 0
    assert (L + T) % (INPROJ_SUB * TOK_TILE) == 0 and T % ATTB_TQ == 0 and B % SCAN_ROWS == 0

    mod_rows = -(-(B + 1) // 8) * 8
    cs = jnp.concatenate([c, c_ctx[None, :], jnp.zeros((mod_rows - B - 1, D), c.dtype)], axis=0)
    mod = _mod_call(cs, ada_w, ada_b)

    tab_b = _rope_tables(L, T, B_ROPE, LANE, B_NOPE)
    tab_c = _rope_tables(L, T, C_HEAD, C_HEAD, 0)
    ones_a = _block_ones(MXU_W, A_HEAD)

    xall = jnp.concatenate([ctx, x], axis=1)
    for i in range(depth):
        with_ctx = i < depth - 1
        lp = _layer_params(i, tab_b, tab_c, w_in, a_mu_prev, a_mu_next, a_w0, a_w_up, a_a0, a_a_up, a_k_k, a_k_a,
                           a_r_k, a_gn_g, a_gn_b, b_q_ln, b_kv_ln, b_w_uq, b_w_ukv, b_qn_g, b_kn_g,
                           c_qn_g, c_kn_g, c_sink, w_branch_out, w_out)
        za, zb, zc = _inproj_call(xall, mod[i], norm_g[i], lp["w_mix"], IN_SEGS_MIX, L)
        rvk, dirp, bonus = _prep_call(za, lp, ones_a, L)
        y0, y1 = _scan_call(rvk, dirp, L, SCAN_PASSES)
        qb, kb, vb, qc, kc, vc = _bcproj_call(zb, zc, lp)
        yb = _attb_call(qb, kb, vb, L, T, ATTB_TQ, L + T)
        if with_ctx:
            yb = _attb_call(qb, kb, vb, 0, L, TOK_TILE, L, prev=yb)
        yb = yb.reshape(B, L + T, BW)
        yc = _attc_call(qc, kc, vc, lp["sink"], L, with_ctx)
        xall = _merge_call(y0, y1, bonus, yb, yc, xall, mod[i], norm_g[i], lp, ones_a, L, with_ctx)
    return xall
```

```python
import functools

import numpy as np
import jax
import jax.numpy as jnp
from jax import lax
from jax.experimental import pallas as pl
from jax.experimental.pallas import tpu as pltpu

F32 = jnp.float32
BF16 = jnp.bfloat16

D_MODEL = 1024
GRID_W = 64
ROPE_THETA = 10000.0
NORM_EPS = 1e-6
NEG = -1e30
BW = 512
N_BRANCH = 3

A_HEAD = 64
A_HEADS = BW // A_HEAD
A_LORA = 64
A_GN_EPS = 64e-5
A_COLS = 3 * BW + 4 * A_LORA

B_HEADS = 8
B_NOPE = 64
B_ROPE = 32
B_QK = B_NOPE + B_ROPE
B_V = BW // B_HEADS
B_Q_LORA = 256
B_KV_LORA = 128
B_COLS = B_Q_LORA + B_KV_LORA + B_ROPE
B_COLS_P = 512

C_HEAD = 64
C_HEADS = BW // C_HEAD
C_KV_HEADS = 2
C_GROUP = C_HEADS // C_KV_HEADS
C_KV_W = C_KV_HEADS * C_HEAD
C_COLS = BW + 2 * C_KV_W
WINDOW = 128

LANE = 128
MXU_W = 256
CHUNK = 64
SCAN_ROWS = 2
TOK_TILE = 256
VMEM_LIMIT = 56 * 1024 * 1024

INPROJ_SUB = 3
ATTB_TQ = 512
LOG2E = 1.4426950408889634

IN_SEGS_MIX = (("za", A_COLS), ("zb", B_COLS_P), ("zc", C_COLS))
IN_SEGS_GATE = (("ga", BW), ("gb", BW), ("gc", BW), ("zg", N_BRANCH * D_MODEL))

NN = (((1,), (0,)), ((), ()))
NT = (((1,), (1,)), ((), ()))


def _dot(a, b, dims=NN):
    return lax.dot_general(a, b, dims, preferred_element_type=F32)


def _split(x):
    hi = x.astype(BF16)
    lo = (x - hi.astype(F32)).astype(BF16)
    return hi, lo


def _mm(a, b, dims=NN, passes=1):
    if passes == 1:
        return _dot(a.astype(BF16), b.astype(BF16), dims)
    ah, al = _split(a)
    bh, bl = _split(b)
    return _dot(ah, bh, dims) + (_dot(ah, bl, dims) + _dot(al, bh, dims))


def _mm_exact_rhs(a, b_bf16, parts=2):
    acc = None
    rem = a
    for _ in range(parts):
        p = rem.astype(BF16)
        t = _dot(p, b_bf16)
        acc = t if acc is None else acc + t
        rem = rem - p.astype(F32)
    return acc


def _head_sums(a, ones_blk):
    w = ones_blk.shape[0]
    return jnp.concatenate(
        [_mm_exact_rhs(a[:, c:c + w], ones_blk) for c in range(0, a.shape[1], w)], axis=1)


def _mm_exact_lhs(a_bf16, b, parts=3):
    acc = None
    rem = b
    for _ in range(parts):
        p = rem.astype(BF16)
        t = _dot(a_bf16, p)
        acc = t if acc is None else acc + t
        rem = rem - p.astype(F32)
    return acc


def _sigmoid(x):
    return 1.0 / (1.0 + jnp.exp(-x))


def _silu(x):
    return x * _sigmoid(x)


def _modulated_norm(x, g, shift, scale):
    ms = jnp.mean(x * x, axis=-1, keepdims=True)
    return (x * lax.rsqrt(ms + NORM_EPS) * g) * (1.0 + scale) + shift


def _params(sem):
    return pltpu.CompilerParams(dimension_semantics=sem, vmem_limit_bytes=VMEM_LIMIT)


def _mod_kernel(c_ref, w_ref, b_ref, o_ref):
    s = _silu(c_ref[...])
    o_ref[0] = _mm(s, w_ref[0], passes=3) + b_ref[0]


def _mod_call(cs, ada_w, ada_b):
    n_layers = ada_w.shape[0]
    rows = cs.shape[0]
    return pl.pallas_call(
        _mod_kernel,
        out_shape=jax.ShapeDtypeStruct((n_layers, rows, 3 * D_MODEL), F32),
        grid=(n_layers, 3),
        in_specs=[
            pl.BlockSpec((rows, D_MODEL), lambda l, n: (0, 0)),
            pl.BlockSpec((1, D_MODEL, D_MODEL), lambda l, n: (l, 0, n)),
            pl.BlockSpec((1, 1, D_MODEL), lambda l, n: (l, 0, n)),
        ],
        out_specs=pl.BlockSpec((1, rows, D_MODEL), lambda l, n: (l, 0, n)),
        compiler_params=_params(("arbitrary", "arbitrary")),
        name="mod",
    )(cs, ada_w, ada_b.reshape(n_layers, 1, 3 * D_MODEL))


def _inproj_kernel(segs, n_sub, x_ref, *refs):
    mod_refs = refs[:2 * n_sub]
    g_ref, w_ref = refs[2 * n_sub:2 * n_sub + 2]
    o_refs = refs[2 * n_sub + 2:-1]
    hb_ref = refs[-1]
    for s in range(n_sub):
        rows = slice(s * TOK_TILE, (s + 1) * TOK_TILE)
        h = _modulated_norm(x_ref[0, rows, :], g_ref[...], mod_refs[2 * s][0], mod_refs[2 * s + 1][0])
        hb_ref[rows, :] = h.astype(BF16)
    hb = hb_ref[...]
    col = 0
    for (_, width), o_ref in zip(segs, o_refs):
        step = 512 if width % 512 == 0 else 256
        for c0 in range(0, width, step):
            o_ref[0, :, c0:c0 + step] = _dot(hb, w_ref[:, col + c0:col + c0 + step])
        col += width


def _mod_row(ctx_tiles, n_batch_rows):
    return lambda b, t: jnp.where(t < ctx_tiles, n_batch_rows, b)


def _inproj_call(xall, modl, norm_g, w_p, segs, ctx_len):
    B, S, _ = xall.shape
    n_sub = INPROJ_SUB
    tm = n_sub * TOK_TILE
    row = _mod_row(ctx_len // TOK_TILE, B)
    mod3 = modl.reshape(modl.shape[0], 1, 3 * D_MODEL)
    n_cols = sum(w for _, w in segs)
    mod_specs = []
    for s in range(n_sub):
        mod_specs.append(pl.BlockSpec((1, 1, D_MODEL), lambda b, t, s=s: (row(b, t * n_sub + s), 0, 0)))
        mod_specs.append(pl.BlockSpec((1, 1, D_MODEL), lambda b, t, s=s: (row(b, t * n_sub + s), 0, 1)))
    return pl.pallas_call(
        functools.partial(_inproj_kernel, segs, n_sub),
        out_shape=tuple(jax.ShapeDtypeStruct((B, S, w), F32) for _, w in segs),
        grid=(B, S // tm),
        in_specs=[pl.BlockSpec((1, tm, D_MODEL), lambda b, t: (b, t, 0))] + mod_specs + [
            pl.BlockSpec((1, D_MODEL), lambda b, t: (0, 0)),
            pl.BlockSpec((D_MODEL, n_cols), lambda b, t: (0, 0), pipeline_mode=pl.Buffered(1)),
        ],
        out_specs=tuple(pl.BlockSpec((1, tm, w), lambda b, t: (b, t, 0)) for _, w in segs),
        scratch_shapes=[pltpu.VMEM((tm, D_MODEL), BF16)],
        compiler_params=_params(("parallel", "arbitrary")),
        name="inproj",
    )(xall, *([mod3] * (2 * n_sub)), norm_g.reshape(1, D_MODEL), w_p)


def _prep_kernel(ctx_tiles, n_tiles,
                 z_ref, zp_ref, zn_ref, mup_ref, mun_ref, w0_ref, wup_ref, a0_ref, aup_ref,
                 kk_ref, ka_ref, rk_ref, ones_ref,
                 rvk_out, dkn_out, lw_out, bonus_out):
    t = pl.program_id(1)
    tt = z_ref.shape[1]
    first = jnp.logical_or(t == 0, t == ctx_tiles)
    last = jnp.logical_or(t == ctx_tiles - 1, t == n_tiles - 1)
    z = z_ref[0]
    row = lax.broadcasted_iota(jnp.int32, (tt, 1), 0)
    halo_prev = jnp.where(first, 0.0, zp_ref[0, 7:8, :])
    halo_next = jnp.where(last, 0.0, zn_ref[0, 0:1, :])
    prev = jnp.where(row == 0, halo_prev, pltpu.roll(z, 1, 0))
    nxt = jnp.where(row == tt - 1, halo_next, pltpu.roll(z, tt - 1, 0))
    zsh = z + mup_ref[...] * (prev - z) + mun_ref[...] * (nxt - z)

    r = zsh[:, 0:BW]
    k = zsh[:, BW:2 * BW]
    v = zsh[:, 2 * BW:3 * BW]
    ones = ones_ref[...]
    kk = k * kk_ref[...]
    ss = _head_sums(kk * kk, ones)
    kk = kk * lax.rsqrt(jnp.maximum(ss, 1e-24))
    rvk_out[0, :, 0:BW] = r.astype(BF16)
    rvk_out[0, :, BW:2 * BW] = v.astype(BF16)
    rvk_out[0, :, 2 * BW:3 * BW] = kk.astype(BF16)
    kd_sum = None
    for d in range(2):
        wd = zsh[:, 3 * BW + d * A_LORA:3 * BW + (d + 1) * A_LORA]
        ad = zsh[:, 3 * BW + 2 * A_LORA + d * A_LORA:3 * BW + 2 * A_LORA + (d + 1) * A_LORA]
        u = -(w0_ref[d:d + 1, :] + _mm(jnp.tanh(wd), wup_ref[d], passes=3))
        softplus = jnp.maximum(u, 0.0) + jnp.log(1.0 + jnp.exp(-jnp.abs(u)))
        w_log = -softplus - 0.5
        lw = -jnp.exp(w_log)
        a = _sigmoid(a0_ref[d:d + 1, :] + _mm(ad, aup_ref[d], passes=3))
        kd = k * (1.0 + (a - 1.0) * ka_ref[...])
        lw_out[d, 0] = lw
        dkn_out[d, 0, :, 0:BW] = kd.astype(BF16)
        dkn_out[d, 0, :, BW:2 * BW] = (-(kk * a)).astype(BF16)
        kd_sum = kd if kd_sum is None else kd_sum + kd
    bonus_out[0] = _head_sums(r * kd_sum * rk_ref[...], ones) * v


def _prep_call(za, lp, ones_a, ctx_len):
    B, S, _ = za.shape
    tt = TOK_TILE
    n_tiles = S // tt
    hb = tt // 8
    n8 = S // 8
    full = lambda shape: pl.BlockSpec(shape, lambda b, t: (0,) * len(shape))
    s3 = lambda w: pl.BlockSpec((1, tt, w), lambda b, t: (b, t, 0))
    s4 = lambda w: pl.BlockSpec((2, 1, tt, w), lambda b, t: (0, b, t, 0))
    return pl.pallas_call(
        functools.partial(_prep_kernel, ctx_len // tt, n_tiles),
        out_shape=(jax.ShapeDtypeStruct((B, S, 3 * BW), BF16), jax.ShapeDtypeStruct((2, B, S, 2 * BW), BF16),
                   jax.ShapeDtypeStruct((2, B, S, BW), F32), jax.ShapeDtypeStruct((B, S, BW), F32)),
        grid=(B, n_tiles),
        in_specs=[
            pl.BlockSpec((1, tt, A_COLS), lambda b, t: (b, t, 0)),
            pl.BlockSpec((1, 8, A_COLS), lambda b, t: (b, jnp.maximum(t * hb - 1, 0), 0)),
            pl.BlockSpec((1, 8, A_COLS), lambda b, t: (b, jnp.minimum((t + 1) * hb, n8 - 1), 0)),
            full((1, A_COLS)), full((1, A_COLS)),
            full((2, BW)), full((2, A_LORA, BW)), full((2, BW)), full((2, A_LORA, BW)),
            full((1, BW)), full((1, BW)), full((1, BW)), full((MXU_W, MXU_W)),
        ],
        out_specs=(s3(3 * BW), s4(2 * BW), s4(BW), s3(BW)),
        compiler_params=_params(("parallel", "arbitrary")),
        name="prep",
    )(za, za, za, lp["mu_prev"], lp["mu_next"], lp["w0"], lp["w_up"], lp["a0"], lp["a_up"],
      lp["k_k"], lp["k_a"], lp["r_k"], ones_a)


def _scan_kernel(passes, f_rvk, f_dkn, f_lw, b_rvk, b_dkn, b_lw, yf_ref, yb_ref, state_ref):
    C = CHUNK

    @pl.when(pl.program_id(1) == 0)
    def _():
        state_ref[...] = jnp.zeros_like(state_ref)

    lane = lax.broadcasted_iota(jnp.int32, (1, LANE), 1)
    lo = lane < A_HEAD
    row2 = lax.broadcasted_iota(jnp.int32, (2 * C, 2 * C), 0)
    col2 = lax.broadcasted_iota(jnp.int32, (2 * C, 2 * C), 1)
    rows = lax.broadcasted_iota(jnp.int32, (C, C), 0)
    cols = lax.broadcasted_iota(jnp.int32, (C, C), 1)
    n_pairs = A_HEADS // 2

    n_rows = f_rvk.shape[0]
    dirs = []
    for row, d in [(row, d) for row in range(n_rows) for d in range(2)]:
        rvk_ref, dkn_ref, lw_ref = (f_rvk, f_dkn, f_lw) if d == 0 else (b_rvk, b_dkn, b_lw)
        order = (rows - cols) if d == 0 else (cols - rows)
        lw = lw_ref[0, row]
        kd = dkn_ref[0, row, :, 0:BW].astype(F32)
        nb = dkn_ref[0, row, :, BW:2 * BW].astype(F32)
        cum = _mm_exact_lhs((order >= 0).astype(BF16), lw)
        tot = cum[C - 1:C, :] if d == 0 else cum[0:1, :]
        p_inv = jnp.exp(-cum)
        p_tot = jnp.exp(tot - cum)
        kt = rvk_ref[row, :, 2 * BW:3 * BW].astype(F32) * jnp.exp(cum - lw)
        order2 = (row2 % C - col2 % C) if d == 0 else (col2 % C - row2 % C)
        keep = order2 >= jnp.where(row2 < C, 0, 1)
        e_rows = jnp.broadcast_to(jnp.exp(tot), (LANE, BW))
        dirs.append(dict(
            keep=keep, kt=kt, v=rvk_ref[row, :, BW:2 * BW].astype(F32),
            y_ref=yf_ref if d == 0 else yb_ref, row=row,
            lhs=jnp.concatenate([rvk_ref[row, :, 0:BW].astype(F32) * jnp.exp(cum), kt], axis=0),
            rhs_e=jnp.concatenate([nb * p_inv, kd * p_inv], axis=0),
            rhs_o=jnp.concatenate([kd * p_inv, nb * p_inv], axis=0),
            nkt=jnp.concatenate([nb * p_tot, kd * p_tot, e_rows], axis=0).T))

    units = [(d, h) for d in range(len(dirs)) for h in range(A_HEADS)]
    slab = lambda h: slice((h // 2) * LANE, (h // 2 + 1) * LANE)
    own = lambda h: lo if h % 2 == 0 else jnp.logical_not(lo)

    lhs_m, top, bot = [], [], []
    for d, h in units:
        x = dirs[d]
        lm = jnp.where(own(h), x["lhs"][:, slab(h)], 0.0)
        rhs = x["rhs_o" if h % 2 else "rhs_e"][:, slab(h)]
        qk = jnp.where(x["keep"], _mm(lm, rhs, NT, passes), 0.0)
        lhs_m.append(lm)
        top.append(qk[0:C])
        bot.append(qk[C:2 * C])
    pairs = [(d, j) for d in range(len(dirs)) for j in range(n_pairs)]
    eye2 = (lax.broadcasted_iota(jnp.int32, (C, 2 * C), 1) % C
            == lax.broadcasted_iota(jnp.int32, (C, 2 * C), 0)).astype(F32)
    bd = lambda m: jnp.concatenate([jnp.where(lo, m, 0.0), jnp.where(lo, 0.0, m)], axis=0)
    n_pair = [jnp.where(lo, bot[2 * p], bot[2 * p + 1]) for p in range(len(pairs))]
    t_inv = [eye2 + n for n in n_pair]
    pw = [_mm(n, bd(n), NN, passes) for n in n_pair]
    for _ in range(4):
        both = [_mm(jnp.concatenate([t, p], axis=0), bd(p), NN, passes) for t, p in zip(t_inv, pw)]
        t_inv = [t + m[0:C] for t, m in zip(t_inv, both)]
        pw = [m[C:2 * C] for m in both]
    t_inv = [t + _mm(t, bd(p), NN, passes) for t, p in zip(t_inv, pw)]
    wta = []
    for i, (d, h) in enumerate(units):
        t = t_inv[i // 2]
        kt_s = dirs[d]["kt"][:, slab(h)]
        if h % 2 == 0:
            wta.append(_mm(t[:, 0:C], jnp.where(lo, kt_s, bot[i]), NN, passes))
        else:
            rhs = jnp.where(lo, bot[i], kt_s)
            wta.append(_mm(jnp.where(lo, 0.0, t), jnp.concatenate([jnp.zeros_like(rhs), rhs], axis=0),
                           NN, passes))
    st = [state_ref[i] for i in range(len(pairs))]
    vp = [dirs[d]["v"][:, j * LANE:(j + 1) * LANE] for d, j in pairs]
    sv = [(jnp.concatenate([s, v], axis=0), jnp.concatenate([v, s], axis=0)) for s, v in zip(st, vp)]
    qt_u = [_mm(m, sv[i // 2][i % 2], NN, passes) for i, m in enumerate(wta)]
    qt = [jnp.where(lo, qt_u[2 * i], qt_u[2 * i + 1]) for i in range(len(pairs))]
    qv = [(jnp.concatenate([q, v], axis=0), jnp.concatenate([v, q], axis=0)) for q, v in zip(qt, vp)]
    y_u = []
    for i, (d, h) in enumerate(units):
        p = i // 2
        y_l = jnp.concatenate([top[i], lhs_m[i][0:C]], axis=1)
        y_r = jnp.concatenate([qv[p][h % 2], st[p], st[p]], axis=0)
        y_u.append(_mm(y_l, y_r, NN, passes))
    for p, (d, j) in enumerate(pairs):
        sl = slice(j * LANE, (j + 1) * LANE)
        dirs[d]["y_ref"][dirs[d]["row"], :, sl] = jnp.where(lo, y_u[2 * p], y_u[2 * p + 1])
        nk = dirs[d]["nkt"][sl]
        upd = _mm(nk[:, 0:2 * C], qv[p][0], NN, passes)
        dec = nk[:, 2 * C:] * jnp.concatenate([st[p], st[p]], axis=0)
        state_ref[p] = jnp.where(lo, (dec + upd)[0:A_HEAD], (dec + upd)[A_HEAD:])


def _scan_call(rvk, dkn, lw, ctx_len, passes):
    B, S, _ = rvk.shape
    C = CHUNK
    R = SCAN_ROWS
    nc = S // C
    lc = ctx_len // C
    rev = lambda c: jnp.where(c < lc, lc - 1 - c, nc - 1 - (c - lc))
    f3 = lambda w: pl.BlockSpec((R, C, w), lambda b, c: (b, c, 0))
    b3 = lambda w: pl.BlockSpec((R, C, w), lambda b, c: (b, rev(c), 0))
    f4 = lambda w: pl.BlockSpec((1, R, C, w), lambda b, c: (0, b, c, 0))
    b4 = lambda w: pl.BlockSpec((1, R, C, w), lambda b, c: (1, b, rev(c), 0))
    out = jax.ShapeDtypeStruct((B, S, BW), F32)
    return pl.pallas_call(
        functools.partial(_scan_kernel, passes),
        out_shape=(out, out),
        grid=(B // R, nc),
        in_specs=[f3(3 * BW), f4(2 * BW), f4(BW), b3(3 * BW), b4(2 * BW), b4(BW)],
        out_specs=(f3(BW), b3(BW)),
        scratch_shapes=[pltpu.VMEM((R * A_HEADS, A_HEAD, LANE), F32)],
        compiler_params=_params(("parallel", "arbitrary")),
        name="scan",
    )(rvk, dkn, lw, rvk, dkn, lw)


def _rope(x, tab_ref, shift):
    return (x * tab_ref[0] + pltpu.roll(x, shift, 1) * tab_ref[1]
            + pltpu.roll(x, LANE - shift, 1) * tab_ref[2])


def _bcproj_kernel(zb_ref, zc_ref, tbq_ref, tbk_ref, tcq_ref, tck_ref, qln_ref, kvln_ref,
                   wuq_ref, wuk_ref, wuv_ref, onesb_ref, onesc_ref,
                   qb_out, kb_out, vb_out, qc_out, kc_out, vc_out):
    ones_b = onesb_ref[...]
    ones_c = onesc_ref[...]
    zb = zb_ref[0]
    cq = zb[:, 0:B_Q_LORA]
    ckv = zb[:, B_Q_LORA:B_Q_LORA + B_KV_LORA]
    kr_slab = zb[:, B_Q_LORA + B_KV_LORA:B_COLS_P]
    cqn = cq * lax.rsqrt(jnp.mean(cq * cq, axis=-1, keepdims=True) + NORM_EPS) * qln_ref[...]
    ckvn = ckv * lax.rsqrt(jnp.mean(ckv * ckv, axis=-1, keepdims=True) + NORM_EPS) * kvln_ref[...]
    q_all = _dot(cqn.astype(BF16), wuq_ref[...])
    k_all = _dot(ckvn.astype(BF16), wuk_ref[...])
    vb_out[0] = _dot(ckvn.astype(BF16), wuv_ref[...]).astype(BF16)
    for h in range(B_HEADS):
        sl = slice(h * LANE, (h + 1) * LANE)
        q = q_all[:, sl]
        rs = lax.rsqrt(_mm_exact_rhs(q * q, ones_b) * (1.0 / B_QK) + NORM_EPS)
        qb_out[0, :, sl] = (_rope(q, tbq_ref, B_ROPE // 4) * rs).astype(BF16)
        k = k_all[:, sl] + kr_slab
        rs = lax.rsqrt(_mm_exact_rhs(k * k, ones_b) * (1.0 / B_QK) + NORM_EPS)
        kb_out[0, :, sl] = (_rope(k, tbk_ref, B_ROPE // 4) * rs).astype(BF16)

    zc = zc_ref[0]
    lane = lax.broadcasted_iota(jnp.int32, (1, LANE), 1)
    lo = lane < C_HEAD

    def head_norm(x, tab_ref):
        ms = _mm_exact_rhs(x * x, ones_c) * (1.0 / C_HEAD)
        return _rope(x, tab_ref, C_HEAD // 4) * lax.rsqrt(ms + NORM_EPS)

    for j in range(C_HEADS // 2):
        x = head_norm(zc[:, j * LANE:(j + 1) * LANE], tcq_ref)
        xr = pltpu.roll(x, C_HEAD, 1)
        g = (2 * j) // C_GROUP
        for half in range(2):
            h = 2 * j + half
            src = x if half == g else xr
            keep = lo if g == 0 else jnp.logical_not(lo)
            qc_out[0, :, h * LANE:(h + 1) * LANE] = jnp.where(keep, src, 0.0).astype(BF16)
    kc_out[0] = head_norm(zc[:, BW:BW + C_KV_W], tck_ref).astype(BF16)
    vv = zc[:, BW + C_KV_W:BW + 2 * C_KV_W]
    vr = pltpu.roll(vv, C_HEAD, 1)
    vc_out[0, :, 0:LANE] = jnp.where(lo, vv, vr).astype(BF16)
    vc_out[0, :, LANE:2 * LANE] = jnp.where(lo, vr, vv).astype(BF16)


def _bcproj_call(zb, zc, lp):
    B, S, _ = zb.shape
    tt = TOK_TILE
    full = lambda shape: pl.BlockSpec(shape, lambda t, b: (0,) * len(shape))
    tok = lambda w: pl.BlockSpec((1, tt, w), lambda t, b: (b, t, 0))
    tab = pl.BlockSpec((3, tt, LANE), lambda t, b: (0, t, 0))
    shp = lambda w: jax.ShapeDtypeStruct((B, S, w), BF16)
    return pl.pallas_call(
        _bcproj_kernel,
        out_shape=(shp(B_HEADS * LANE), shp(B_HEADS * LANE), shp(BW),
                   shp(C_HEADS * LANE), shp(C_KV_W), shp(2 * LANE)),
        grid=(S // tt, B),
        in_specs=[tok(B_COLS_P), tok(C_COLS), tab, tab, tab, tab,
                  full((1, B_Q_LORA)), full((1, B_KV_LORA)),
                  full((B_Q_LORA, B_HEADS * LANE)), full((B_KV_LORA, B_HEADS * LANE)),
                  full((B_KV_LORA, BW)), full((LANE, LANE)), full((LANE, LANE))],
        out_specs=(tok(B_HEADS * LANE), tok(B_HEADS * LANE), tok(BW),
                   tok(C_HEADS * LANE), tok(C_KV_W), tok(2 * LANE)),
        compiler_params=_params(("parallel", "arbitrary")),
        name="bcproj",
    )(zb, zc, lp["tab_bq"], lp["tab_bk"], lp["tab_cq"], lp["tab_ck"],
      lp["q_ln"], lp["kv_ln"], lp["w_uq"], lp["w_uk"], lp["w_uv"],
      _block_ones(LANE, LANE), _block_ones(LANE, C_HEAD))


def _attb_kernel(q_ref, k_ref, v_ref, *rest):
    o_ref = rest[-1]
    lane = lax.broadcasted_iota(jnp.int32, (1, LANE), 1)
    lo = lane < B_V
    for j in range(B_HEADS // 2):
        outs = []
        for h in (2 * j, 2 * j + 1):
            sl = slice(h * LANE, (h + 1) * LANE)
            s = _dot(q_ref[:, sl], k_ref[0, :, sl], NT)
            m = jnp.max(s, axis=-1, keepdims=True)
            p = jnp.exp2(s - m)
            den = jnp.sum(p, axis=-1, keepdims=True)
            o = _dot(p.astype(BF16), v_ref[0, :, j * LANE:(j + 1) * LANE])
            outs.append(o / den)
        o_ref[:, j * LANE:(j + 1) * LANE] = jnp.where(lo, outs[0], outs[1])


def _row_block(n_rows, width, seq_len, row0):
    return pl.BlockSpec(
        (pl.Element(n_rows), pl.Element(width)),
        lambda b, t: (pl.multiple_of(b * seq_len + row0 + t * n_rows, TOK_TILE), 0))


def _attb_call(qb, kb, vb, row0, n_rows, tq, n_keys, prev=None):
    B, S, _ = qb.shape
    width = B_HEADS * LANE
    in_specs = [
        _row_block(tq, width, S, row0),
        pl.BlockSpec((1, n_keys, width), lambda b, t: (b, 0, 0)),
        pl.BlockSpec((1, n_keys, BW), lambda b, t: (b, 0, 0)),
    ]
    args = [qb.reshape(B * S, width), kb, vb]
    if prev is not None:
        in_specs.append(pl.BlockSpec(memory_space=pl.ANY))
        args.append(prev)
    return pl.pallas_call(
        _attb_kernel,
        out_shape=jax.ShapeDtypeStruct((B * S, BW), F32),
        grid=(B, n_rows // tq),
        in_specs=in_specs,
        out_specs=_row_block(tq, BW, S, row0),
        input_output_aliases={} if prev is None else {3: 0},
        compiler_params=_params(("parallel", "arbitrary")),
        name="attb",
    )(*args)


def _attc_kernel(tile0, ctx_len, n_lat, sink_ref, q_ref, k_ref, v_ref, valid_ref, o_ref):
    W = WINDOW
    ctx_tiles = ctx_len // W
    n_blk = n_lat // W
    t = pl.program_id(1) + tile0
    lane = lax.broadcasted_iota(jnp.int32, (1, LANE), 1)
    lo = lane < C_HEAD
    rows4 = lax.broadcasted_iota(jnp.int32, (C_GROUP * W, 1), 0)

    def attend(k_cat, v_cat, mask):
        for g in range(C_KV_HEADS):
            q_st = jnp.concatenate(
                [q_ref[0, :, h * LANE:(h + 1) * LANE] for h in range(g * C_GROUP, (g + 1) * C_GROUP)],
                axis=0)
            s = _dot(q_st, k_cat, NT)
            if mask is not None:
                s = jnp.where(jnp.concatenate([mask] * C_GROUP, axis=0) > 0.5, s, NEG)
            sk = jnp.zeros((C_GROUP * W, 1), F32)
            for i in range(C_GROUP):
                sk = jnp.where(rows4 // W == i, sink_ref[g * C_GROUP + i], sk)
            m = jnp.maximum(jnp.max(s, axis=-1, keepdims=True), sk)
            p = jnp.exp2(s - m)
            den = jnp.sum(p, axis=-1, keepdims=True) + jnp.exp2(sk - m)
            o = _dot(p.astype(BF16), v_cat[:, g * LANE:(g + 1) * LANE]) / den
            for jj in range(C_GROUP // 2):
                o_lo = o[(2 * jj) * W:(2 * jj + 1) * W]
                o_hi = o[(2 * jj + 1) * W:(2 * jj + 2) * W]
                col = (g * C_GROUP // 2 + jj) * LANE
                o_ref[0, :, col:col + LANE] = jnp.where(lo, o_lo, o_hi)

    def latent():
        j = t - ctx_tiles
        bm = jnp.maximum(j - 1, 0)
        bp = jnp.minimum(j + 1, n_blk - 1)

        def blk(ref, b):
            return ref[0, pl.ds(pl.multiple_of(ctx_len + b * W, W), W), :]

        k_cat = jnp.concatenate([k_ref[0, 0:ctx_len, :], blk(k_ref, bm), blk(k_ref, j), blk(k_ref, bp)], axis=0)
        v_cat = jnp.concatenate([v_ref[0, 0:ctx_len, :], blk(v_ref, bm), blk(v_ref, j), blk(v_ref, bp)], axis=0)
        attend(k_cat, v_cat, valid_ref[0])

    def context():
        attend(k_ref[0, 0:ctx_len, :], v_ref[0, 0:ctx_len, :], None)

    if tile0 < ctx_tiles:
        pl.when(t < ctx_tiles)(context)
        pl.when(t >= ctx_tiles)(latent)
    else:
        latent()


def _attc_call(qc, kc, vc, sink, ctx_len, with_ctx):
    B, S, _ = qc.shape
    W = WINDOW
    ctx_tiles = ctx_len // W
    tile0 = 0 if with_ctx else ctx_tiles
    n_tiles = S // W - tile0
    n_blk = (S - ctx_len) // W
    valid = _band_valid(ctx_len)

    def variant(t):
        j = t + tile0 - ctx_tiles
        return jnp.where(j == 0, 1, 0) + jnp.where(j == n_blk - 1, 2, 0)

    return pl.pallas_call(
        functools.partial(_attc_kernel, tile0, ctx_len, S - ctx_len),
        out_shape=jax.ShapeDtypeStruct((B, n_tiles * W, BW), F32),
        grid=(B, n_tiles),
        in_specs=[
            pl.BlockSpec(memory_space=pltpu.SMEM),
            pl.BlockSpec((1, W, C_HEADS * LANE), lambda b, t: (b, t + tile0, 0)),
            pl.BlockSpec((1, S, C_KV_W), lambda b, t: (b, 0, 0)),
            pl.BlockSpec((1, S, 2 * LANE), lambda b, t: (b, 0, 0)),
            pl.BlockSpec((1, W, ctx_len + 3 * W), lambda b, t: (variant(t), 0, 0)),
        ],
        out_specs=pl.BlockSpec((1, W, BW), lambda b, t: (b, t, 0)),
        compiler_params=_params(("parallel", "arbitrary")),
        name="attc",
    )(sink, qc, kc, vc, valid)


def _band_valid(ctx_len):
    W = WINDOW
    qi = np.arange(W)[:, None]
    cb = np.arange(ctx_len + 3 * W)[None, :] - ctx_len
    base = (cb < 0) | (np.abs(cb - W - qi) <= WINDOW)
    out = []
    for variant in range(4):
        first, last = bool(variant & 1), bool(variant & 2)
        ok = base.copy()
        if first:
            ok &= ~((cb >= 0) & (cb < W))
        if last:
            ok &= ~(cb >= 2 * W)
        out.append(ok)
    return jnp.asarray(np.stack(out), dtype=F32)


def _merge_kernel(y0_ref, y1_ref, bonus_ref, yb_ref, yc_ref, x_ref, shift_ref, scale_ref, gate_ref,
                  g_ref, wg_ref, gng_ref, gnb_ref, ones_ref, wbo_ref, wout_ref, o_ref):
    x = x_ref[0]
    hb = _modulated_norm(x, g_ref[...], shift_ref[0], scale_ref[0]).astype(BF16)
    ones = ones_ref[...]
    y = y0_ref[0] + y1_ref[0]
    mu = _head_sums(y, ones) * (1.0 / A_HEAD)
    dlt = y - mu
    var = _head_sums(dlt * dlt, ones) * (1.0 / A_HEAD)
    ya = dlt * lax.rsqrt(var + A_GN_EPS) * gng_ref[...] + gnb_ref[...] + bonus_ref[0]
    m = None
    zg0 = N_BRANCH * BW
    for n, y_n in enumerate((ya, yb_ref[0], yc_ref[0])):
        g = _dot(hb, wg_ref[:, n * BW:(n + 1) * BW])
        u = (y_n * _silu(g)).astype(BF16)
        zg = _dot(hb, wg_ref[:, zg0 + n * D_MODEL:zg0 + (n + 1) * D_MODEL])
        term = _sigmoid(zg) * _dot(u, wbo_ref[n])
        m = term if m is None else m + term
    o_ref[0] = x + gate_ref[0] * _dot(m.astype(BF16), wout_ref[...])


def _merge_call(y0, y1, bonus, yb, yc, xall, modl, norm_g, lp, ones_a, ctx_len, with_ctx):
    B, S, _ = xall.shape
    tt = TOK_TILE
    ctx_tiles = ctx_len // tt
    tile0 = 0 if with_ctx else ctx_tiles
    n_tiles = S // tt - tile0
    row = _mod_row(ctx_tiles, B)
    mod3 = modl.reshape(modl.shape[0], 1, 3 * D_MODEL)
    mod = lambda col: pl.BlockSpec((1, 1, D_MODEL), lambda b, t: (row(b, t + tile0), 0, col))
    once = lambda shape: pl.BlockSpec(shape, lambda b, t: (0,) * len(shape), pipeline_mode=pl.Buffered(1))
    tok = lambda w: pl.BlockSpec((1, tt, w), lambda b, t: (b, t + tile0, 0))
    att = lambda w: pl.BlockSpec((1, tt, w), lambda b, t: (b, t, 0))
    n_gate = sum(w for _, w in IN_SEGS_GATE)
    return pl.pallas_call(
        _merge_kernel,
        out_shape=jax.ShapeDtypeStruct((B, n_tiles * tt, D_MODEL), F32),
        grid=(B, n_tiles),
        in_specs=[tok(BW), tok(BW), tok(BW), tok(BW), att(BW), tok(D_MODEL),
                  mod(0), mod(1), mod(2),
                  once((1, D_MODEL)), once((D_MODEL, n_gate)),
                  once((1, BW)), once((1, BW)), once((MXU_W, MXU_W)),
                  once((N_BRANCH, BW, D_MODEL)), once((D_MODEL, D_MODEL))],
        out_specs=pl.BlockSpec((1, tt, D_MODEL), lambda b, t: (b, t, 0)),
        compiler_params=_params(("parallel", "arbitrary")),
        name="merge",
    )(y0, y1, bonus, yb, yc, xall, mod3, mod3, mod3, norm_g.reshape(1, D_MODEL), lp["w_gate"],
      lp["gn_g"], lp["gn_b"], ones_a, lp["w_bo"], lp["w_out"])


def _rope_tables(n_ctx, n_lat, rot_dim, head_w, lane0):
    rows = n_lat // GRID_W
    row = jnp.repeat(jnp.arange(rows), GRID_W).astype(F32)
    colp = jnp.tile(jnp.arange(GRID_W), rows).astype(F32)
    q = rot_dim // 4
    inv = ROPE_THETA ** (-(2.0 * jnp.arange(q, dtype=F32)) / (rot_dim // 2))
    ang = jnp.concatenate([row[:, None] * inv, colp[:, None] * inv], axis=-1)
    cos, sin = jnp.cos(ang), jnp.sin(ang)
    zero = jnp.zeros_like(sin[:, :q])
    cos_g = jnp.concatenate([cos[:, :q], cos[:, :q], cos[:, q:], cos[:, q:]], axis=-1)
    sp_g = jnp.concatenate([zero, sin[:, :q], zero, sin[:, q:]], axis=-1)
    sm_g = jnp.concatenate([-sin[:, :q], zero, -sin[:, q:], zero], axis=-1)

    def place(t, fill):
        pad_l = jnp.full((n_lat, lane0), fill, F32)
        pad_r = jnp.full((n_lat, head_w - lane0 - rot_dim), fill, F32)
        grp = jnp.concatenate([pad_l, t, pad_r], axis=-1)
        lat = jnp.tile(grp, (1, LANE // head_w))
        return jnp.concatenate([jnp.full((n_ctx, LANE), fill, F32), lat], axis=0)

    return jnp.stack([place(cos_g, 1.0), place(sp_g, 0.0), place(sm_g, 0.0)])


def _block_ones(n, blk):
    i = np.arange(n) // blk
    return jnp.asarray(i[:, None] == i[None, :], dtype=BF16)


def _pad_lanes(g, width):
    return jnp.pad(g, (0, width - g.shape[0]))


def _fold_gain(tab, gain, shift, scale):
    g = jnp.stack([gain, jnp.roll(gain, shift), jnp.roll(gain, -shift)]) * scale
    return tab * g[:, None, :]


def _layer_params(i, tab_b, tab_c, w_in, a_mu_prev, a_mu_next, a_w0, a_w_up, a_a0, a_a_up, a_k_k, a_k_a, a_r_k,
                  a_gn_g, a_gn_b, b_q_ln, b_kv_ln, b_w_uq, b_w_ukv, b_qn_g, b_kn_g,
                  c_qn_g, c_kn_g, c_sink, w_branch_out, w_out):
    w = w_in[i]
    c0 = 0
    segs = {}
    for name, width in (("za", A_COLS), ("ga", BW), ("zb", B_COLS), ("gb", BW),
                        ("zc", C_COLS), ("gc", BW), ("zg", N_BRANCH * D_MODEL)):
        segs[name] = w[:, c0:c0 + width]
        c0 += width
    zb = segs["zb"]
    zpad = lambda n: jnp.zeros((D_MODEL, n), w.dtype)
    segs["zb"] = jnp.concatenate(
        [zb[:, :B_Q_LORA + B_KV_LORA], zpad(B_NOPE), zb[:, B_Q_LORA + B_KV_LORA:], zpad(LANE - B_NOPE - B_ROPE)],
        axis=-1)
    w_mix = jnp.concatenate([segs[n] for n, _ in IN_SEGS_MIX], axis=-1).astype(BF16)
    w_gate = jnp.concatenate([segs[n] for n, _ in IN_SEGS_GATE], axis=-1).astype(BF16)

    uq = b_w_uq[i].reshape(B_Q_LORA, B_HEADS, B_QK)
    uq = jnp.pad(uq, ((0, 0), (0, 0), (0, LANE - B_QK))).reshape(B_Q_LORA, B_HEADS * LANE)
    ukv = b_w_ukv[i].reshape(B_KV_LORA, B_HEADS, B_NOPE + B_V)
    uk = jnp.pad(ukv[:, :, :B_NOPE], ((0, 0), (0, 0), (0, LANE - B_NOPE))).reshape(B_KV_LORA, B_HEADS * LANE)
    uv = ukv[:, :, B_NOPE:].reshape(B_KV_LORA, BW)
    row = lambda t: t.reshape(1, -1)
    return dict(
        w_mix=w_mix, w_gate=w_gate,
        mu_prev=row(a_mu_prev[i]), mu_next=row(a_mu_next[i]),
        w0=a_w0[i], w_up=a_w_up[i], a0=a_a0[i], a_up=a_a_up[i],
        k_k=row(a_k_k[i]), k_a=row(a_k_a[i]), r_k=row(a_r_k[i]),
        gn_g=row(a_gn_g[i]), gn_b=row(a_gn_b[i]),
        q_ln=row(b_q_ln[i]), kv_ln=row(b_kv_ln[i]),
        w_uq=uq.astype(BF16), w_uk=uk.astype(BF16), w_uv=uv.astype(BF16),
        tab_bq=_fold_gain(tab_b, _pad_lanes(b_qn_g[i], LANE), B_ROPE // 4, B_QK ** -0.5 * LOG2E),
        tab_bk=_fold_gain(tab_b, _pad_lanes(b_kn_g[i], LANE), B_ROPE // 4, 1.0),
        tab_cq=_fold_gain(tab_c, jnp.tile(c_qn_g[i], 2), C_HEAD // 4, C_HEAD ** -0.5 * LOG2E),
        tab_ck=_fold_gain(tab_c, jnp.tile(c_kn_g[i], 2), C_HEAD // 4, 1.0),
        sink=c_sink[i] * LOG2E,
        w_bo=w_branch_out[i].astype(BF16), w_out=w_out[i].astype(BF16),
    )


SCAN_PASSES = 1


def kernel(x, c, ctx, c_ctx, ada_w, ada_b, norm_g, w_in, a_mu_prev, a_mu_next, a_w0, a_w_up, a_a0, a_a_up, a_k_k, a_k_a, a_r_k, a_gn_g, a_gn_b, b_q_ln, b_kv_ln, b_w_uq, b_w_ukv, b_qn_g, b_kn_g, c_qn_g, c_kn_g, c_sink, w_branch_out, w_out):
    B, T, D = x.shape
    L = ctx.shape[1]
    depth = ada_w.shape[0]
    assert D == D_MODEL and L % TOK_TILE == 0 and T % TOK_TILE == 0 and T % GRID_W == 0
    assert (L + T) % (INPROJ_SUB * TOK_TILE) == 0 and T % ATTB_TQ == 0 and B % SCAN_ROWS == 0

    mod_rows = -(-(B + 1) // 8) * 8
    cs = jnp.concatenate([c, c_ctx[None, :], jnp.zeros((mod_rows - B - 1, D), c.dtype)], axis=0)
    mod = _mod_call(cs, ada_w, ada_b)

    tab_b = _rope_tables(L, T, B_ROPE, LANE, B_NOPE)
    tab_c = _rope_tables(L, T, C_HEAD, C_HEAD, 0)
    ones_a = _block_ones(MXU_W, A_HEAD)

    xall = jnp.concatenate([ctx, x], axis=1)
    for i in range(depth):
        with_ctx = i < depth - 1
        lp = _layer_params(i, tab_b, tab_c, w_in, a_mu_prev, a_mu_next, a_w0, a_w_up, a_a0, a_a_up, a_k_k, a_k_a,
                           a_r_k, a_gn_g, a_gn_b, b_q_ln, b_kv_ln, b_w_uq, b_w_ukv, b_qn_g, b_kn_g,
                           c_qn_g, c_kn_g, c_sink, w_branch_out, w_out)
        za, zb, zc = _inproj_call(xall, mod[i], norm_g[i], lp["w_mix"], IN_SEGS_MIX, L)
        rvk, dkn, lw, bonus = _prep_call(za, lp, ones_a, L)
        y0, y1 = _scan_call(rvk, dkn, lw, L, SCAN_PASSES)
        qb, kb, vb, qc, kc, vc = _bcproj_call(zb, zc, lp)
        yb = _attb_call(qb, kb, vb, L, T, ATTB_TQ, L + T)
        if with_ctx:
            yb = _attb_call(qb, kb, vb, 0, L, TOK_TILE, L, prev=yb)
        yb = yb.reshape(B, L + T, BW)
        yc = _attc_call(qc, kc, vc, lp["sink"], L, with_ctx)
        xall = _merge_call(y0, y1, bonus, yb, yc, xall, mod[i], norm_g[i], lp, ones_a, L, with_ctx)
    return xall
```

```python
import functools

import numpy as np
import jax
import jax.numpy as jnp
from jax import lax
from jax.experimental import pallas as pl
from jax.experimental.pallas import tpu as pltpu

F32 = jnp.float32
BF16 = jnp.bfloat16

D_MODEL = 1024
GRID_W = 64
ROPE_THETA = 10000.0
NORM_EPS = 1e-6
NEG = -1e30
BW = 512
N_BRANCH = 3

A_HEAD = 64
A_HEADS = BW // A_HEAD
A_LORA = 64
A_GN_EPS = 64e-5
A_COLS = 3 * BW + 4 * A_LORA

B_HEADS = 8
B_NOPE = 64
B_ROPE = 32
B_QK = B_NOPE + B_ROPE
B_V = BW // B_HEADS
B_Q_LORA = 256
B_KV_LORA = 128
B_COLS = B_Q_LORA + B_KV_LORA + B_ROPE
B_COLS_P = 512

C_HEAD = 64
C_HEADS = BW // C_HEAD
C_KV_HEADS = 2
C_GROUP = C_HEADS // C_KV_HEADS
C_KV_W = C_KV_HEADS * C_HEAD
C_COLS = BW + 2 * C_KV_W
WINDOW = 128

LANE = 128
MXU_W = 256
CHUNK = 64
SCAN_ROWS = 2
TOK_TILE = 256
VMEM_LIMIT = 56 * 1024 * 1024

ATTB_TQ = 512
LOG2E = 1.4426950408889634

IN_SEGS_MIX = (("za", A_COLS), ("zb", B_COLS_P), ("zc", C_COLS))
IN_SEGS_GATE = (("ga", BW), ("gb", BW), ("gc", BW), ("zg", N_BRANCH * D_MODEL))

NN = (((1,), (0,)), ((), ()))
NT = (((1,), (1,)), ((), ()))


def _dot(a, b, dims=NN):
    return lax.dot_general(a, b, dims, preferred_element_type=F32)


def _split(x):
    hi = x.astype(BF16)
    lo = (x - hi.astype(F32)).astype(BF16)
    return hi, lo


def _mm(a, b, dims=NN, passes=1):
    if passes == 1:
        return _dot(a.astype(BF16), b.astype(BF16), dims)
    ah, al = _split(a)
    bh, bl = _split(b)
    return _dot(ah, bh, dims) + (_dot(ah, bl, dims) + _dot(al, bh, dims))


def _mm_exact_rhs(a, b_bf16, parts=2):
    acc = None
    rem = a
    for _ in range(parts):
        p = rem.astype(BF16)
        t = _dot(p, b_bf16)
        acc = t if acc is None else acc + t
        rem = rem - p.astype(F32)
    return acc


def _head_sums(a, ones_blk):
    w = ones_blk.shape[0]
    return jnp.concatenate(
        [_mm_exact_rhs(a[:, c:c + w], ones_blk) for c in range(0, a.shape[1], w)], axis=1)


def _mm_exact_lhs(a_bf16, b, parts=3):
    acc = None
    rem = b
    for _ in range(parts):
        p = rem.astype(BF16)
        t = _dot(a_bf16, p)
        acc = t if acc is None else acc + t
        rem = rem - p.astype(F32)
    return acc


def _sigmoid(x):
    return 1.0 / (1.0 + jnp.exp(-x))


def _silu(x):
    return x * _sigmoid(x)


def _modulated_norm(x, g, shift, scale):
    ms = jnp.mean(x * x, axis=-1, keepdims=True)
    return (x * lax.rsqrt(ms + NORM_EPS) * g) * (1.0 + scale) + shift


def _params(sem):
    return pltpu.CompilerParams(dimension_semantics=sem, vmem_limit_bytes=VMEM_LIMIT)


def _mod_kernel(c_ref, w_ref, b_ref, o_ref):
    s = _silu(c_ref[...])
    o_ref[0] = _mm(s, w_ref[0], passes=3) + b_ref[0]


def _mod_call(cs, ada_w, ada_b):
    n_layers = ada_w.shape[0]
    rows = cs.shape[0]
    return pl.pallas_call(
        _mod_kernel,
        out_shape=jax.ShapeDtypeStruct((n_layers, rows, 3 * D_MODEL), F32),
        grid=(n_layers, 3),
        in_specs=[
            pl.BlockSpec((rows, D_MODEL), lambda l, n: (0, 0)),
            pl.BlockSpec((1, D_MODEL, D_MODEL), lambda l, n: (l, 0, n)),
            pl.BlockSpec((1, 1, D_MODEL), lambda l, n: (l, 0, n)),
        ],
        out_specs=pl.BlockSpec((1, rows, D_MODEL), lambda l, n: (l, 0, n)),
        compiler_params=_params(("arbitrary", "arbitrary")),
        name="mod",
    )(cs, ada_w, ada_b.reshape(n_layers, 1, 3 * D_MODEL))


def _mod_row(ctx_tiles, n_batch_rows):
    return lambda b, t: jnp.where(t < ctx_tiles, n_batch_rows, b)


def _inprep_kernel(ctx_tiles, n_tiles,
                   x_ref, xp_ref, xn_ref, shift_ref, scale_ref, g_ref, w_ref,
                   mup_ref, mun_ref, w0_ref, wup_ref, a0_ref, aup_ref, kk_ref, ka_ref, rk_ref, ones_ref,
                   zb_out, zc_out, rvk_out, dkn_out, lw_out, bonus_out, hb_ref, za_ref):
    t = pl.program_id(1)
    tt = TOK_TILE
    first = jnp.logical_or(t == 0, t == ctx_tiles)
    last = jnp.logical_or(t == ctx_tiles - 1, t == n_tiles - 1)
    g, shift, scale = g_ref[...], shift_ref[0], scale_ref[0]
    hb_ref[0:tt, :] = _modulated_norm(x_ref[0], g, shift, scale).astype(BF16)
    hb_ref[tt:tt + 8, :] = _modulated_norm(xp_ref[0], g, shift, scale).astype(BF16)
    hb_ref[tt + 8:tt + 16, :] = _modulated_norm(xn_ref[0], g, shift, scale).astype(BF16)
    hb = hb_ref[...]
    for c0 in range(0, A_COLS, MXU_W):
        za_ref[:, c0:c0 + MXU_W] = _dot(hb, w_ref[:, c0:c0 + MXU_W])
    hb_t = hb[0:tt]
    zb_out[0] = _dot(hb_t, w_ref[:, A_COLS:A_COLS + B_COLS_P])
    for c0 in range(0, C_COLS, MXU_W):
        col = A_COLS + B_COLS_P + c0
        zc_out[0, :, c0:c0 + MXU_W] = _dot(hb_t, w_ref[:, col:col + MXU_W])

    z = za_ref[0:tt, :]
    row = lax.broadcasted_iota(jnp.int32, (tt, 1), 0)
    halo_prev = jnp.where(first, 0.0, za_ref[tt + 7:tt + 8, :])
    halo_next = jnp.where(last, 0.0, za_ref[tt + 8:tt + 9, :])
    prev = jnp.where(row == 0, halo_prev, pltpu.roll(z, 1, 0))
    nxt = jnp.where(row == tt - 1, halo_next, pltpu.roll(z, tt - 1, 0))
    zsh = z + mup_ref[...] * (prev - z) + mun_ref[...] * (nxt - z)

    r = zsh[:, 0:BW]
    k = zsh[:, BW:2 * BW]
    v = zsh[:, 2 * BW:3 * BW]
    ones = ones_ref[...]
    kk = k * kk_ref[...]
    ss = _head_sums(kk * kk, ones)
    kk = kk * lax.rsqrt(jnp.maximum(ss, 1e-24))
    rvk_out[0, :, 0:BW] = r.astype(BF16)
    rvk_out[0, :, BW:2 * BW] = v.astype(BF16)
    rvk_out[0, :, 2 * BW:3 * BW] = kk.astype(BF16)
    kd_sum = None
    for d in range(2):
        wd = zsh[:, 3 * BW + d * A_LORA:3 * BW + (d + 1) * A_LORA]
        ad = zsh[:, 3 * BW + 2 * A_LORA + d * A_LORA:3 * BW + 2 * A_LORA + (d + 1) * A_LORA]
        u = -(w0_ref[d:d + 1, :] + _mm(jnp.tanh(wd), wup_ref[d], passes=3))
        softplus = jnp.maximum(u, 0.0) + jnp.log(1.0 + jnp.exp(-jnp.abs(u)))
        w_log = -softplus - 0.5
        lw = -jnp.exp(w_log)
        a = _sigmoid(a0_ref[d:d + 1, :] + _mm(ad, aup_ref[d], passes=3))
        kd = k * (1.0 + (a - 1.0) * ka_ref[...])
        lw_out[d, 0] = lw
        dkn_out[d, 0, :, 0:BW] = kd.astype(BF16)
        dkn_out[d, 0, :, BW:2 * BW] = (-(kk * a)).astype(BF16)
        kd_sum = kd if kd_sum is None else kd_sum + kd
    bonus_out[0] = _head_sums(r * kd_sum * rk_ref[...], ones) * v


def _inprep_call(xall, modl, norm_g, lp, ones_a, ctx_len):
    B, S, _ = xall.shape
    tt = TOK_TILE
    n_tiles = S // tt
    hb = tt // 8
    n8 = S // 8
    row = _mod_row(ctx_len // tt, B)
    mod3 = modl.reshape(modl.shape[0], 1, 3 * D_MODEL)
    mod = lambda col: pl.BlockSpec((1, 1, D_MODEL), lambda b, t: (row(b, t), 0, col))
    once = lambda shape: pl.BlockSpec(shape, lambda b, t: (0,) * len(shape), pipeline_mode=pl.Buffered(1))
    s3 = lambda w: pl.BlockSpec((1, tt, w), lambda b, t: (b, t, 0))
    s4 = lambda w: pl.BlockSpec((2, 1, tt, w), lambda b, t: (0, b, t, 0))
    o3 = lambda w, dt: jax.ShapeDtypeStruct((B, S, w), dt)
    o4 = lambda w, dt: jax.ShapeDtypeStruct((2, B, S, w), dt)
    n_mix = sum(w for _, w in IN_SEGS_MIX)
    return pl.pallas_call(
        functools.partial(_inprep_kernel, ctx_len // tt, n_tiles),
        out_shape=(o3(B_COLS_P, F32), o3(C_COLS, F32), o3(3 * BW, BF16), o4(2 * BW, BF16),
                   o4(BW, F32), o3(BW, F32)),
        grid=(B, n_tiles),
        in_specs=[
            s3(D_MODEL),
            pl.BlockSpec((1, 8, D_MODEL), lambda b, t: (b, jnp.maximum(t * hb - 1, 0), 0)),
            pl.BlockSpec((1, 8, D_MODEL), lambda b, t: (b, jnp.minimum((t + 1) * hb, n8 - 1), 0)),
            mod(0), mod(1), once((1, D_MODEL)), once((D_MODEL, n_mix)),
            once((1, A_COLS)), once((1, A_COLS)),
            once((2, BW)), once((2, A_LORA, BW)), once((2, BW)), once((2, A_LORA, BW)),
            once((1, BW)), once((1, BW)), once((1, BW)), once((MXU_W, MXU_W)),
        ],
        out_specs=(s3(B_COLS_P), s3(C_COLS), s3(3 * BW), s4(2 * BW), s4(BW), s3(BW)),
        scratch_shapes=[pltpu.VMEM((tt + 16, D_MODEL), BF16), pltpu.VMEM((tt + 16, A_COLS), F32)],
        compiler_params=_params(("parallel", "arbitrary")),
        name="inprep",
    )(xall, xall, xall, mod3, mod3, norm_g.reshape(1, D_MODEL), lp["w_mix"],
      lp["mu_prev"], lp["mu_next"], lp["w0"], lp["w_up"], lp["a0"], lp["a_up"],
      lp["k_k"], lp["k_a"], lp["r_k"], ones_a)


def _scan_kernel(passes, f_rvk, f_dkn, f_lw, b_rvk, b_dkn, b_lw, yf_ref, yb_ref, state_ref):
    C = CHUNK

    @pl.when(pl.program_id(1) == 0)
    def _():
        state_ref[...] = jnp.zeros_like(state_ref)

    lane = lax.broadcasted_iota(jnp.int32, (1, LANE), 1)
    lo = lane < A_HEAD
    row2 = lax.broadcasted_iota(jnp.int32, (2 * C, 2 * C), 0)
    col2 = lax.broadcasted_iota(jnp.int32, (2 * C, 2 * C), 1)
    rows = lax.broadcasted_iota(jnp.int32, (C, C), 0)
    cols = lax.broadcasted_iota(jnp.int32, (C, C), 1)
    n_pairs = A_HEADS // 2

    n_rows = f_rvk.shape[0]
    dirs = []
    for row, d in [(row, d) for row in range(n_rows) for d in range(2)]:
        rvk_ref, dkn_ref, lw_ref = (f_rvk, f_dkn, f_lw) if d == 0 else (b_rvk, b_dkn, b_lw)
        order = (rows - cols) if d == 0 else (cols - rows)
        lw = lw_ref[0, row]
        kd = dkn_ref[0, row, :, 0:BW].astype(F32)
        nb = dkn_ref[0, row, :, BW:2 * BW].astype(F32)
        cum = _mm_exact_lhs((order >= 0).astype(BF16), lw)
        tot = cum[C - 1:C, :] if d == 0 else cum[0:1, :]
        p_inv = jnp.exp(-cum)
        p_tot = jnp.exp(tot - cum)
        kt = rvk_ref[row, :, 2 * BW:3 * BW].astype(F32) * jnp.exp(cum - lw)
        order2 = (row2 % C - col2 % C) if d == 0 else (col2 % C - row2 % C)
        keep = order2 >= jnp.where(row2 < C, 0, 1)
        e_rows = jnp.broadcast_to(jnp.exp(tot), (LANE, BW))
        dirs.append(dict(
            keep=keep, kt=kt, v=rvk_ref[row, :, BW:2 * BW].astype(F32),
            y_ref=yf_ref if d == 0 else yb_ref, row=row,
            lhs=jnp.concatenate([rvk_ref[row, :, 0:BW].astype(F32) * jnp.exp(cum), kt], axis=0),
            rhs_e=jnp.concatenate([nb * p_inv, kd * p_inv], axis=0),
            rhs_o=jnp.concatenate([kd * p_inv, nb * p_inv], axis=0),
            nkt=jnp.concatenate([nb * p_tot, kd * p_tot, e_rows], axis=0).T))

    units = [(d, h) for d in range(len(dirs)) for h in range(A_HEADS)]
    slab = lambda h: slice((h // 2) * LANE, (h // 2 + 1) * LANE)
    own = lambda h: lo if h % 2 == 0 else jnp.logical_not(lo)

    lhs_m, top, bot = [], [], []
    for d, h in units:
        x = dirs[d]
        lm = jnp.where(own(h), x["lhs"][:, slab(h)], 0.0)
        rhs = x["rhs_o" if h % 2 else "rhs_e"][:, slab(h)]
        qk = jnp.where(x["keep"], _mm(lm, rhs, NT, passes), 0.0)
        lhs_m.append(lm)
        top.append(qk[0:C])
        bot.append(qk[C:2 * C])
    pairs = [(d, j) for d in range(len(dirs)) for j in range(n_pairs)]
    eye2 = (lax.broadcasted_iota(jnp.int32, (C, 2 * C), 1) % C
            == lax.broadcasted_iota(jnp.int32, (C, 2 * C), 0)).astype(F32)
    bd = lambda m: jnp.concatenate([jnp.where(lo, m, 0.0), jnp.where(lo, 0.0, m)], axis=0)
    n_pair = [jnp.where(lo, bot[2 * p], bot[2 * p + 1]) for p in range(len(pairs))]
    t_inv = [eye2 + n for n in n_pair]
    pw = [_mm(n, bd(n), NN, passes) for n in n_pair]
    for _ in range(4):
        both = [_mm(jnp.concatenate([t, p], axis=0), bd(p), NN, passes) for t, p in zip(t_inv, pw)]
        t_inv = [t + m[0:C] for t, m in zip(t_inv, both)]
        pw = [m[C:2 * C] for m in both]
    t_inv = [t + _mm(t, bd(p), NN, passes) for t, p in zip(t_inv, pw)]
    wta = []
    for i, (d, h) in enumerate(units):
        t = t_inv[i // 2]
        kt_s = dirs[d]["kt"][:, slab(h)]
        if h % 2 == 0:
            wta.append(_mm(t[:, 0:C], jnp.where(lo, kt_s, bot[i]), NN, passes))
        else:
            rhs = jnp.where(lo, bot[i], kt_s)
            wta.append(_mm(jnp.where(lo, 0.0, t), jnp.concatenate([jnp.zeros_like(rhs), rhs], axis=0),
                           NN, passes))
    st = [state_ref[i] for i in range(len(pairs))]
    vp = [dirs[d]["v"][:, j * LANE:(j + 1) * LANE] for d, j in pairs]
    sv = [(jnp.concatenate([s, v], axis=0), jnp.concatenate([v, s], axis=0)) for s, v in zip(st, vp)]
    qt_u = [_mm(m, sv[i // 2][i % 2], NN, passes) for i, m in enumerate(wta)]
    qt = [jnp.where(lo, qt_u[2 * i], qt_u[2 * i + 1]) for i in range(len(pairs))]
    qv = [(jnp.concatenate([q, v], axis=0), jnp.concatenate([v, q], axis=0)) for q, v in zip(qt, vp)]
    y_u = []
    for i, (d, h) in enumerate(units):
        p = i // 2
        y_l = jnp.concatenate([top[i], lhs_m[i][0:C]], axis=1)
        y_r = jnp.concatenate([qv[p][h % 2], st[p], st[p]], axis=0)
        y_u.append(_mm(y_l, y_r, NN, passes))
    for p, (d, j) in enumerate(pairs):
        sl = slice(j * LANE, (j + 1) * LANE)
        dirs[d]["y_ref"][dirs[d]["row"], :, sl] = jnp.where(lo, y_u[2 * p], y_u[2 * p + 1])
        nk = dirs[d]["nkt"][sl]
        upd = _mm(nk[:, 0:2 * C], qv[p][0], NN, passes)
        dec = nk[:, 2 * C:] * jnp.concatenate([st[p], st[p]], axis=0)
        state_ref[p] = jnp.where(lo, (dec + upd)[0:A_HEAD], (dec + upd)[A_HEAD:])


def _scan_call(rvk, dkn, lw, ctx_len, passes):
    B, S, _ = rvk.shape
    C = CHUNK
    R = SCAN_ROWS
    nc = S // C
    lc = ctx_len // C
    rev = lambda c: jnp.where(c < lc, lc - 1 - c, nc - 1 - (c - lc))
    f3 = lambda w: pl.BlockSpec((R, C, w), lambda b, c: (b, c, 0))
    b3 = lambda w: pl.BlockSpec((R, C, w), lambda b, c: (b, rev(c), 0))
    f4 = lambda w: pl.BlockSpec((1, R, C, w), lambda b, c: (0, b, c, 0))
    b4 = lambda w: pl.BlockSpec((1, R, C, w), lambda b, c: (1, b, rev(c), 0))
    out = jax.ShapeDtypeStruct((B, S, BW), F32)
    return pl.pallas_call(
        functools.partial(_scan_kernel, passes),
        out_shape=(out, out),
        grid=(B // R, nc),
        in_specs=[f3(3 * BW), f4(2 * BW), f4(BW), b3(3 * BW), b4(2 * BW), b4(BW)],
        out_specs=(f3(BW), b3(BW)),
        scratch_shapes=[pltpu.VMEM((R * A_HEADS, A_HEAD, LANE), F32)],
        compiler_params=_params(("parallel", "arbitrary")),
        name="scan",
    )(rvk, dkn, lw, rvk, dkn, lw)


def _rope(x, tab_ref, shift):
    return (x * tab_ref[0] + pltpu.roll(x, shift, 1) * tab_ref[1]
            + pltpu.roll(x, LANE - shift, 1) * tab_ref[2])


def _bcproj_kernel(zb_ref, zc_ref, tbq_ref, tbk_ref, tcq_ref, tck_ref, qln_ref, kvln_ref,
                   wuq_ref, wuk_ref, wuv_ref, onesb_ref, onesc_ref,
                   qb_out, kb_out, vb_out, qc_out, kc_out, vc_out):
    ones_b = onesb_ref[...]
    ones_c = onesc_ref[...]
    zb = zb_ref[0]
    cq = zb[:, 0:B_Q_LORA]
    ckv = zb[:, B_Q_LORA:B_Q_LORA + B_KV_LORA]
    kr_slab = zb[:, B_Q_LORA + B_KV_LORA:B_COLS_P]
    cqn = cq * lax.rsqrt(jnp.mean(cq * cq, axis=-1, keepdims=True) + NORM_EPS) * qln_ref[...]
    ckvn = ckv * lax.rsqrt(jnp.mean(ckv * ckv, axis=-1, keepdims=True) + NORM_EPS) * kvln_ref[...]
    q_all = _dot(cqn.astype(BF16), wuq_ref[...])
    k_all = _dot(ckvn.astype(BF16), wuk_ref[...])
    vb_out[0] = _dot(ckvn.astype(BF16), wuv_ref[...]).astype(BF16)
    for h in range(B_HEADS):
        sl = slice(h * LANE, (h + 1) * LANE)
        q = q_all[:, sl]
        rs = lax.rsqrt(_mm_exact_rhs(q * q, ones_b) * (1.0 / B_QK) + NORM_EPS)
        qb_out[0, :, sl] = (_rope(q, tbq_ref, B_ROPE // 4) * rs).astype(BF16)
        k = k_all[:, sl] + kr_slab
        rs = lax.rsqrt(_mm_exact_rhs(k * k, ones_b) * (1.0 / B_QK) + NORM_EPS)
        kb_out[0, :, sl] = (_rope(k, tbk_ref, B_ROPE // 4) * rs).astype(BF16)

    zc = zc_ref[0]
    lane = lax.broadcasted_iota(jnp.int32, (1, LANE), 1)
    lo = lane < C_HEAD

    def head_norm(x, tab_ref):
        ms = _mm_exact_rhs(x * x, ones_c) * (1.0 / C_HEAD)
        return _rope(x, tab_ref, C_HEAD // 4) * lax.rsqrt(ms + NORM_EPS)

    for j in range(C_HEADS // 2):
        x = head_norm(zc[:, j * LANE:(j + 1) * LANE], tcq_ref)
        xr = pltpu.roll(x, C_HEAD, 1)
        g = (2 * j) // C_GROUP
        for half in range(2):
            h = 2 * j + half
            src = x if half == g else xr
            keep = lo if g == 0 else jnp.logical_not(lo)
            qc_out[0, :, h * LANE:(h + 1) * LANE] = jnp.where(keep, src, 0.0).astype(BF16)
    kc_out[0] = head_norm(zc[:, BW:BW + C_KV_W], tck_ref).astype(BF16)
    vv = zc[:, BW + C_KV_W:BW + 2 * C_KV_W]
    vr = pltpu.roll(vv, C_HEAD, 1)
    vc_out[0, :, 0:LANE] = jnp.where(lo, vv, vr).astype(BF16)
    vc_out[0, :, LANE:2 * LANE] = jnp.where(lo, vr, vv).astype(BF16)


def _bcproj_call(zb, zc, lp):
    B, S, _ = zb.shape
    tt = TOK_TILE
    full = lambda shape: pl.BlockSpec(shape, lambda t, b: (0,) * len(shape))
    tok = lambda w: pl.BlockSpec((1, tt, w), lambda t, b: (b, t, 0))
    tab = pl.BlockSpec((3, tt, LANE), lambda t, b: (0, t, 0))
    shp = lambda w: jax.ShapeDtypeStruct((B, S, w), BF16)
    return pl.pallas_call(
        _bcproj_kernel,
        out_shape=(shp(B_HEADS * LANE), shp(B_HEADS * LANE), shp(BW),
                   shp(C_HEADS * LANE), shp(C_KV_W), shp(2 * LANE)),
        grid=(S // tt, B),
        in_specs=[tok(B_COLS_P), tok(C_COLS), tab, tab, tab, tab,
                  full((1, B_Q_LORA)), full((1, B_KV_LORA)),
                  full((B_Q_LORA, B_HEADS * LANE)), full((B_KV_LORA, B_HEADS * LANE)),
                  full((B_KV_LORA, BW)), full((LANE, LANE)), full((LANE, LANE))],
        out_specs=(tok(B_HEADS * LANE), tok(B_HEADS * LANE), tok(BW),
                   tok(C_HEADS * LANE), tok(C_KV_W), tok(2 * LANE)),
        compiler_params=_params(("parallel", "arbitrary")),
        name="bcproj",
    )(zb, zc, lp["tab_bq"], lp["tab_bk"], lp["tab_cq"], lp["tab_ck"],
      lp["q_ln"], lp["kv_ln"], lp["w_uq"], lp["w_uk"], lp["w_uv"],
      _block_ones(LANE, LANE), _block_ones(LANE, C_HEAD))


def _attb_kernel(q_ref, k_ref, v_ref, *rest):
    o_ref = rest[-1]
    lane = lax.broadcasted_iota(jnp.int32, (1, LANE), 1)
    lo = lane < B_V
    for j in range(B_HEADS // 2):
        outs = []
        for h in (2 * j, 2 * j + 1):
            sl = slice(h * LANE, (h + 1) * LANE)
            s = _dot(q_ref[:, sl], k_ref[0, :, sl], NT)
            m = jnp.max(s, axis=-1, keepdims=True)
            p = jnp.exp2(s - m)
            den = jnp.sum(p, axis=-1, keepdims=True)
            o = _dot(p.astype(BF16), v_ref[0, :, j * LANE:(j + 1) * LANE])
            outs.append(o / den)
        o_ref[:, j * LANE:(j + 1) * LANE] = jnp.where(lo, outs[0], outs[1])


def _row_block(n_rows, width, seq_len, row0):
    return pl.BlockSpec(
        (pl.Element(n_rows), pl.Element(width)),
        lambda b, t: (pl.multiple_of(b * seq_len + row0 + t * n_rows, TOK_TILE), 0))


def _attb_call(qb, kb, vb, row0, n_rows, tq, n_keys, prev=None):
    B, S, _ = qb.shape
    width = B_HEADS * LANE
    in_specs = [
        _row_block(tq, width, S, row0),
        pl.BlockSpec((1, n_keys, width), lambda b, t: (b, 0, 0)),
        pl.BlockSpec((1, n_keys, BW), lambda b, t: (b, 0, 0)),
    ]
    args = [qb.reshape(B * S, width), kb, vb]
    if prev is not None:
        in_specs.append(pl.BlockSpec(memory_space=pl.ANY))
        args.append(prev)
    return pl.pallas_call(
        _attb_kernel,
        out_shape=jax.ShapeDtypeStruct((B * S, BW), F32),
        grid=(B, n_rows // tq),
        in_specs=in_specs,
        out_specs=_row_block(tq, BW, S, row0),
        input_output_aliases={} if prev is None else {3: 0},
        compiler_params=_params(("parallel", "arbitrary")),
        name="attb",
    )(*args)


def _attc_kernel(tile0, ctx_len, n_lat, sink_ref, q_ref, k_ref, v_ref, valid_ref, o_ref):
    W = WINDOW
    ctx_tiles = ctx_len // W
    n_blk = n_lat // W
    t = pl.program_id(1) + tile0
    lane = lax.broadcasted_iota(jnp.int32, (1, LANE), 1)
    lo = lane < C_HEAD
    rows4 = lax.broadcasted_iota(jnp.int32, (C_GROUP * W, 1), 0)

    def attend(k_cat, v_cat, mask):
        for g in range(C_KV_HEADS):
            q_st = jnp.concatenate(
                [q_ref[0, :, h * LANE:(h + 1) * LANE] for h in range(g * C_GROUP, (g + 1) * C_GROUP)],
                axis=0)
            s = _dot(q_st, k_cat, NT)
            if mask is not None:
                s = jnp.where(jnp.concatenate([mask] * C_GROUP, axis=0) > 0.5, s, NEG)
            sk = jnp.zeros((C_GROUP * W, 1), F32)
            for i in range(C_GROUP):
                sk = jnp.where(rows4 // W == i, sink_ref[g * C_GROUP + i], sk)
            m = jnp.maximum(jnp.max(s, axis=-1, keepdims=True), sk)
            p = jnp.exp2(s - m)
            den = jnp.sum(p, axis=-1, keepdims=True) + jnp.exp2(sk - m)
            o = _dot(p.astype(BF16), v_cat[:, g * LANE:(g + 1) * LANE]) / den
            for jj in range(C_GROUP // 2):
                o_lo = o[(2 * jj) * W:(2 * jj + 1) * W]
                o_hi = o[(2 * jj + 1) * W:(2 * jj + 2) * W]
                col = (g * C_GROUP // 2 + jj) * LANE
                o_ref[0, :, col:col + LANE] = jnp.where(lo, o_lo, o_hi)

    def latent():
        j = t - ctx_tiles
        bm = jnp.maximum(j - 1, 0)
        bp = jnp.minimum(j + 1, n_blk - 1)

        def blk(ref, b):
            return ref[0, pl.ds(pl.multiple_of(ctx_len + b * W, W), W), :]

        k_cat = jnp.concatenate([k_ref[0, 0:ctx_len, :], blk(k_ref, bm), blk(k_ref, j), blk(k_ref, bp)], axis=0)
        v_cat = jnp.concatenate([v_ref[0, 0:ctx_len, :], blk(v_ref, bm), blk(v_ref, j), blk(v_ref, bp)], axis=0)
        attend(k_cat, v_cat, valid_ref[0])

    def context():
        attend(k_ref[0, 0:ctx_len, :], v_ref[0, 0:ctx_len, :], None)

    if tile0 < ctx_tiles:
        pl.when(t < ctx_tiles)(context)
        pl.when(t >= ctx_tiles)(latent)
    else:
        latent()


def _attc_call(qc, kc, vc, sink, ctx_len, with_ctx):
    B, S, _ = qc.shape
    W = WINDOW
    ctx_tiles = ctx_len // W
    tile0 = 0 if with_ctx else ctx_tiles
    n_tiles = S // W - tile0
    n_blk = (S - ctx_len) // W
    valid = _band_valid(ctx_len)

    def variant(t):
        j = t + tile0 - ctx_tiles
        return jnp.where(j == 0, 1, 0) + jnp.where(j == n_blk - 1, 2, 0)

    return pl.pallas_call(
        functools.partial(_attc_kernel, tile0, ctx_len, S - ctx_len),
        out_shape=jax.ShapeDtypeStruct((B, n_tiles * W, BW), F32),
        grid=(B, n_tiles),
        in_specs=[
            pl.BlockSpec(memory_space=pltpu.SMEM),
            pl.BlockSpec((1, W, C_HEADS * LANE), lambda b, t: (b, t + tile0, 0)),
            pl.BlockSpec((1, S, C_KV_W), lambda b, t: (b, 0, 0)),
            pl.BlockSpec((1, S, 2 * LANE), lambda b, t: (b, 0, 0)),
            pl.BlockSpec((1, W, ctx_len + 3 * W), lambda b, t: (variant(t), 0, 0)),
        ],
        out_specs=pl.BlockSpec((1, W, BW), lambda b, t: (b, t, 0)),
        compiler_params=_params(("parallel", "arbitrary")),
        name="attc",
    )(sink, qc, kc, vc, valid)


def _band_valid(ctx_len):
    W = WINDOW
    qi = np.arange(W)[:, None]
    cb = np.arange(ctx_len + 3 * W)[None, :] - ctx_len
    base = (cb < 0) | (np.abs(cb - W - qi) <= WINDOW)
    out = []
    for variant in range(4):
        first, last = bool(variant & 1), bool(variant & 2)
        ok = base.copy()
        if first:
            ok &= ~((cb >= 0) & (cb < W))
        if last:
            ok &= ~(cb >= 2 * W)
        out.append(ok)
    return jnp.asarray(np.stack(out), dtype=F32)


def _merge_kernel(y0_ref, y1_ref, bonus_ref, yb_ref, yc_ref, x_ref, shift_ref, scale_ref, gate_ref,
                  g_ref, wg_ref, gng_ref, gnb_ref, ones_ref, wbo_ref, wout_ref, o_ref):
    x = x_ref[0]
    hb = _modulated_norm(x, g_ref[...], shift_ref[0], scale_ref[0]).astype(BF16)
    ones = ones_ref[...]
    y = y0_ref[0] + y1_ref[0]
    mu = _head_sums(y, ones) * (1.0 / A_HEAD)
    dlt = y - mu
    var = _head_sums(dlt * dlt, ones) * (1.0 / A_HEAD)
    ya = dlt * lax.rsqrt(var + A_GN_EPS) * gng_ref[...] + gnb_ref[...] + bonus_ref[0]
    m = None
    zg0 = N_BRANCH * BW
    for n, y_n in enumerate((ya, yb_ref[0], yc_ref[0])):
        g = _dot(hb, wg_ref[:, n * BW:(n + 1) * BW])
        u = (y_n * _silu(g)).astype(BF16)
        zg = _dot(hb, wg_ref[:, zg0 + n * D_MODEL:zg0 + (n + 1) * D_MODEL])
        term = _sigmoid(zg) * _dot(u, wbo_ref[n])
        m = term if m is None else m + term
    o_ref[0] = x + gate_ref[0] * _dot(m.astype(BF16), wout_ref[...])


def _merge_call(y0, y1, bonus, yb, yc, xall, modl, norm_g, lp, ones_a, ctx_len, with_ctx):
    B, S, _ = xall.shape
    tt = TOK_TILE
    ctx_tiles = ctx_len // tt
    tile0 = 0 if with_ctx else ctx_tiles
    n_tiles = S // tt - tile0
    row = _mod_row(ctx_tiles, B)
    mod3 = modl.reshape(modl.shape[0], 1, 3 * D_MODEL)
    mod = lambda col: pl.BlockSpec((1, 1, D_MODEL), lambda b, t: (row(b, t + tile0), 0, col))
    once = lambda shape: pl.BlockSpec(shape, lambda b, t: (0,) * len(shape), pipeline_mode=pl.Buffered(1))
    tok = lambda w: pl.BlockSpec((1, tt, w), lambda b, t: (b, t + tile0, 0))
    att = lambda w: pl.BlockSpec((1, tt, w), lambda b, t: (b, t, 0))
    n_gate = sum(w for _, w in IN_SEGS_GATE)
    return pl.pallas_call(
        _merge_kernel,
        out_shape=jax.ShapeDtypeStruct((B, n_tiles * tt, D_MODEL), F32),
        grid=(B, n_tiles),
        in_specs=[tok(BW), tok(BW), tok(BW), tok(BW), att(BW), tok(D_MODEL),
                  mod(0), mod(1), mod(2),
                  once((1, D_MODEL)), once((D_MODEL, n_gate)),
                  once((1, BW)), once((1, BW)), once((MXU_W, MXU_W)),
                  once((N_BRANCH, BW, D_MODEL)), once((D_MODEL, D_MODEL))],
        out_specs=pl.BlockSpec((1, tt, D_MODEL), lambda b, t: (b, t, 0)),
        compiler_params=_params(("parallel", "arbitrary")),
        name="merge",
    )(y0, y1, bonus, yb, yc, xall, mod3, mod3, mod3, norm_g.reshape(1, D_MODEL), lp["w_gate"],
      lp["gn_g"], lp["gn_b"], ones_a, lp["w_bo"], lp["w_out"])


def _rope_tables(n_ctx, n_lat, rot_dim, head_w, lane0):
    rows = n_lat // GRID_W
    row = jnp.repeat(jnp.arange(rows), GRID_W).astype(F32)
    colp = jnp.tile(jnp.arange(GRID_W), rows).astype(F32)
    q = rot_dim // 4
    inv = ROPE_THETA ** (-(2.0 * jnp.arange(q, dtype=F32)) / (rot_dim // 2))
    ang = jnp.concatenate([row[:, None] * inv, colp[:, None] * inv], axis=-1)
    cos, sin = jnp.cos(ang), jnp.sin(ang)
    zero = jnp.zeros_like(sin[:, :q])
    cos_g = jnp.concatenate([cos[:, :q], cos[:, :q], cos[:, q:], cos[:, q:]], axis=-1)
    sp_g = jnp.concatenate([zero, sin[:, :q], zero, sin[:, q:]], axis=-1)
    sm_g = jnp.concatenate([-sin[:, :q], zero, -sin[:, q:], zero], axis=-1)

    def place(t, fill):
        pad_l = jnp.full((n_lat, lane0), fill, F32)
        pad_r = jnp.full((n_lat, head_w - lane0 - rot_dim), fill, F32)
        grp = jnp.concatenate([pad_l, t, pad_r], axis=-1)
        lat = jnp.tile(grp, (1, LANE // head_w))
        return jnp.concatenate([jnp.full((n_ctx, LANE), fill, F32), lat], axis=0)

    return jnp.stack([place(cos_g, 1.0), place(sp_g, 0.0), place(sm_g, 0.0)])


def _block_ones(n, blk):
    i = np.arange(n) // blk
    return jnp.asarray(i[:, None] == i[None, :], dtype=BF16)


def _pad_lanes(g, width):
    return jnp.pad(g, (0, width - g.shape[0]))


def _fold_gain(tab, gain, shift, scale):
    g = jnp.stack([gain, jnp.roll(gain, shift), jnp.roll(gain, -shift)]) * scale
    return tab * g[:, None, :]


def _layer_params(i, tab_b, tab_c, w_in, a_mu_prev, a_mu_next, a_w0, a_w_up, a_a0, a_a_up, a_k_k, a_k_a, a_r_k,
                  a_gn_g, a_gn_b, b_q_ln, b_kv_ln, b_w_uq, b_w_ukv, b_qn_g, b_kn_g,
                  c_qn_g, c_kn_g, c_sink, w_branch_out, w_out):
    w = w_in[i].astype(BF16)
    c0 = 0
    segs = {}
    for name, width in (("za", A_COLS), ("ga", BW), ("zb", B_COLS), ("gb", BW),
                        ("zc", C_COLS), ("gc", BW), ("zg", N_BRANCH * D_MODEL)):
        segs[name] = w[:, c0:c0 + width]
        c0 += width
    zb = segs["zb"]
    zpad = lambda n: jnp.zeros((D_MODEL, n), w.dtype)
    segs["zb"] = jnp.concatenate(
        [zb[:, :B_Q_LORA + B_KV_LORA], zpad(B_NOPE), zb[:, B_Q_LORA + B_KV_LORA:], zpad(LANE - B_NOPE - B_ROPE)],
        axis=-1)
    w_mix = jnp.concatenate([segs[n] for n, _ in IN_SEGS_MIX], axis=-1).astype(BF16)
    w_gate = jnp.concatenate([segs[n] for n, _ in IN_SEGS_GATE], axis=-1).astype(BF16)

    uq = b_w_uq[i].reshape(B_Q_LORA, B_HEADS, B_QK)
    uq = jnp.pad(uq, ((0, 0), (0, 0), (0, LANE - B_QK))).reshape(B_Q_LORA, B_HEADS * LANE)
    ukv = b_w_ukv[i].reshape(B_KV_LORA, B_HEADS, B_NOPE + B_V)
    uk = jnp.pad(ukv[:, :, :B_NOPE], ((0, 0), (0, 0), (0, LANE - B_NOPE))).reshape(B_KV_LORA, B_HEADS * LANE)
    uv = ukv[:, :, B_NOPE:].reshape(B_KV_LORA, BW)
    row = lambda t: t.reshape(1, -1)
    return dict(
        w_mix=w_mix, w_gate=w_gate,
        mu_prev=row(a_mu_prev[i]), mu_next=row(a_mu_next[i]),
        w0=a_w0[i], w_up=a_w_up[i], a0=a_a0[i], a_up=a_a_up[i],
        k_k=row(a_k_k[i]), k_a=row(a_k_a[i]), r_k=row(a_r_k[i]),
        gn_g=row(a_gn_g[i]), gn_b=row(a_gn_b[i]),
        q_ln=row(b_q_ln[i]), kv_ln=row(b_kv_ln[i]),
        w_uq=uq.astype(BF16), w_uk=uk.astype(BF16), w_uv=uv.astype(BF16),
        tab_bq=_fold_gain(tab_b, _pad_lanes(b_qn_g[i], LANE), B_ROPE // 4, B_QK ** -0.5 * LOG2E),
        tab_bk=_fold_gain(tab_b, _pad_lanes(b_kn_g[i], LANE), B_ROPE // 4, 1.0),
        tab_cq=_fold_gain(tab_c, jnp.tile(c_qn_g[i], 2), C_HEAD // 4, C_HEAD ** -0.5 * LOG2E),
        tab_ck=_fold_gain(tab_c, jnp.tile(c_kn_g[i], 2), C_HEAD // 4, 1.0),
        sink=c_sink[i] * LOG2E,
        w_bo=w_branch_out[i].astype(BF16), w_out=w_out[i].astype(BF16),
    )


SCAN_PASSES = 1


def kernel(x, c, ctx, c_ctx, ada_w, ada_b, norm_g, w_in, a_mu_prev, a_mu_next, a_w0, a_w_up, a_a0, a_a_up, a_k_k, a_k_a, a_r_k, a_gn_g, a_gn_b, b_q_ln, b_kv_ln, b_w_uq, b_w_ukv, b_qn_g, b_kn_g, c_qn_g, c_kn_g, c_sink, w_branch_out, w_out):
    B, T, D = x.shape
    L = ctx.shape[1]
    depth = ada_w.shape[0]
    assert D == D_MODEL and L % TOK_TILE == 0 and T % TOK_TILE == 0 and T % GRID_W == 0
    assert T % ATTB_TQ == 0 and B % SCAN_ROWS == 0

    mod_rows = -(-(B + 1) // 8) * 8
    cs = jnp.concatenate([c, c_ctx[None, :], jnp.zeros((mod_rows - B - 1, D), c.dtype)], axis=0)
    mod = _mod_call(cs, ada_w, ada_b)

    tab_b = _rope_tables(L, T, B_ROPE, LANE, B_NOPE)
    tab_c = _rope_tables(L, T, C_HEAD, C_HEAD, 0)
    ones_a = _block_ones(MXU_W, A_HEAD)

    xall = jnp.concatenate([ctx, x], axis=1)
    for i in range(depth):
        with_ctx = i < depth - 1
        lp = _layer_params(i, tab_b, tab_c, w_in, a_mu_prev, a_mu_next, a_w0, a_w_up, a_a0, a_a_up, a_k_k, a_k_a,
                           a_r_k, a_gn_g, a_gn_b, b_q_ln, b_kv_ln, b_w_uq, b_w_ukv, b_qn_g, b_kn_g,
                           c_qn_g, c_kn_g, c_sink, w_branch_out, w_out)
        zb, zc, rvk, dkn, lw, bonus = _inprep_call(xall, mod[i], norm_g[i], lp, ones_a, L)
        y0, y1 = _scan_call(rvk, dkn, lw, L, SCAN_PASSES)
        qb, kb, vb, qc, kc, vc = _bcproj_call(zb, zc, lp)
        yb = _attb_call(qb, kb, vb, L, T, ATTB_TQ, L + T)
        if with_ctx:
            yb = _attb_call(qb, kb, vb, 0, L, TOK_TILE, L, prev=yb)
        yb = yb.reshape(B, L + T, BW)
        yc = _attc_call(qc, kc, vc, lp["sink"], L, with_ctx)
        xall = _merge_call(y0, y1, bonus, yb, yc, xall, mod[i], norm_g[i], lp, ones_a, L, with_ctx)
    return xall
```

```python
import functools

import numpy as np
import jax
import jax.numpy as jnp
from jax import lax
from jax.experimental import pallas as pl
from jax.experimental.pallas import tpu as pltpu

F32 = jnp.float32
BF16 = jnp.bfloat16

D_MODEL = 1024
GRID_W = 64
ROPE_THETA = 10000.0
NORM_EPS = 1e-6
NEG = -1e30
BW = 512
N_BRANCH = 3

A_HEAD = 64
A_HEADS = BW // A_HEAD
A_LORA = 64
A_GN_EPS = 64e-5
A_COLS = 3 * BW + 4 * A_LORA

B_HEADS = 8
B_NOPE = 64
B_ROPE = 32
B_QK = B_NOPE + B_ROPE
B_V = BW // B_HEADS
B_Q_LORA = 256
B_KV_LORA = 128
B_COLS = B_Q_LORA + B_KV_LORA + B_ROPE
B_COLS_P = 512

C_HEAD = 64
C_HEADS = BW // C_HEAD
C_KV_HEADS = 2
C_GROUP = C_HEADS // C_KV_HEADS
C_KV_W = C_KV_HEADS * C_HEAD
C_COLS = BW + 2 * C_KV_W
WINDOW = 128

LANE = 128
MXU_W = 256
CHUNK = 64
SCAN_ROWS = 4
TOK_TILE = 256
VMEM_LIMIT = 56 * 1024 * 1024

ATTB_TQ = 512
ATTC_BLOCKS = 2
LOG2E = 1.4426950408889634

IN_SEGS_MIX = (("za", A_COLS), ("zb", B_COLS_P), ("zc", C_COLS))
IN_SEGS_GATE = (("ga", BW), ("gb", BW), ("gc", BW), ("zg", N_BRANCH * D_MODEL))

NN = (((1,), (0,)), ((), ()))
NT = (((1,), (1,)), ((), ()))


def _dot(a, b, dims=NN):
    return lax.dot_general(a, b, dims, preferred_element_type=F32)


def _split(x):
    hi = x.astype(BF16)
    lo = (x - hi.astype(F32)).astype(BF16)
    return hi, lo


def _mm(a, b, dims=NN, passes=1):
    if passes == 1:
        return _dot(a.astype(BF16), b.astype(BF16), dims)
    ah, al = _split(a)
    bh, bl = _split(b)
    return _dot(ah, bh, dims) + (_dot(ah, bl, dims) + _dot(al, bh, dims))


def _mm_exact_rhs(a, b_bf16, parts=2):
    acc = None
    rem = a
    for _ in range(parts):
        p = rem.astype(BF16)
        t = _dot(p, b_bf16)
        acc = t if acc is None else acc + t
        rem = rem - p.astype(F32)
    return acc


def _head_sums(a, ones_blk):
    w = ones_blk.shape[0]
    return jnp.concatenate(
        [_mm_exact_rhs(a[:, c:c + w], ones_blk) for c in range(0, a.shape[1], w)], axis=1)


def _mm_exact_lhs(a_bf16, b, parts=3):
    acc = None
    rem = b
    for _ in range(parts):
        p = rem.astype(BF16)
        t = _dot(a_bf16, p)
        acc = t if acc is None else acc + t
        rem = rem - p.astype(F32)
    return acc


def _sigmoid(x):
    return 1.0 / (1.0 + jnp.exp(-x))


def _silu(x):
    return x * _sigmoid(x)


def _modulated_norm(x, g, shift, scale):
    ms = jnp.mean(x * x, axis=-1, keepdims=True)
    return (x * lax.rsqrt(ms + NORM_EPS) * g) * (1.0 + scale) + shift


def _params(sem):
    return pltpu.CompilerParams(dimension_semantics=sem, vmem_limit_bytes=VMEM_LIMIT)


def _mod_kernel(c_ref, w_ref, b_ref, o_ref):
    s = _silu(c_ref[...])
    o_ref[0] = _mm(s, w_ref[0], passes=3) + b_ref[0]


def _mod_call(cs, ada_w, ada_b):
    n_layers = ada_w.shape[0]
    rows = cs.shape[0]
    return pl.pallas_call(
        _mod_kernel,
        out_shape=jax.ShapeDtypeStruct((n_layers, rows, 3 * D_MODEL), F32),
        grid=(n_layers, 3),
        in_specs=[
            pl.BlockSpec((rows, D_MODEL), lambda l, n: (0, 0)),
            pl.BlockSpec((1, D_MODEL, D_MODEL), lambda l, n: (l, 0, n)),
            pl.BlockSpec((1, 1, D_MODEL), lambda l, n: (l, 0, n)),
        ],
        out_specs=pl.BlockSpec((1, rows, D_MODEL), lambda l, n: (l, 0, n)),
        compiler_params=_params(("arbitrary", "arbitrary")),
        name="mod",
    )(cs, ada_w, ada_b.reshape(n_layers, 1, 3 * D_MODEL))


def _mod_row(ctx_tiles, n_batch_rows):
    return lambda b, t: jnp.where(t < ctx_tiles, n_batch_rows, b)


def _inprep_kernel(ctx_tiles, n_tiles,
                   x_ref, xp_ref, xn_ref, shift_ref, scale_ref, g_ref, w_ref,
                   mup_ref, mun_ref, w0_ref, wup_ref, a0_ref, aup_ref, kk_ref, ka_ref, rk_ref, ones_ref,
                   zb_out, zc_out, rvk_out, dkn_out, lw_out, bonus_out, hb_ref, za_ref):
    t = pl.program_id(1)
    tt = TOK_TILE
    first = jnp.logical_or(t == 0, t == ctx_tiles)
    last = jnp.logical_or(t == ctx_tiles - 1, t == n_tiles - 1)
    g, shift, scale = g_ref[...], shift_ref[0], scale_ref[0]
    hb_ref[0:tt, :] = _modulated_norm(x_ref[0], g, shift, scale).astype(BF16)
    hb_ref[tt:tt + 8, :] = _modulated_norm(xp_ref[0], g, shift, scale).astype(BF16)
    hb_ref[tt + 8:tt + 16, :] = _modulated_norm(xn_ref[0], g, shift, scale).astype(BF16)
    hb = hb_ref[...]
    for c0 in range(0, A_COLS, MXU_W):
        za_ref[:, c0:c0 + MXU_W] = _dot(hb, w_ref[:, c0:c0 + MXU_W])
    hb_t = hb[0:tt]
    zb_out[0] = _dot(hb_t, w_ref[:, A_COLS:A_COLS + B_COLS_P])
    for c0 in range(0, C_COLS, MXU_W):
        col = A_COLS + B_COLS_P + c0
        zc_out[0, :, c0:c0 + MXU_W] = _dot(hb_t, w_ref[:, col:col + MXU_W])

    z = za_ref[0:tt, :]
    row = lax.broadcasted_iota(jnp.int32, (tt, 1), 0)
    halo_prev = jnp.where(first, 0.0, za_ref[tt + 7:tt + 8, :])
    halo_next = jnp.where(last, 0.0, za_ref[tt + 8:tt + 9, :])
    prev = jnp.where(row == 0, halo_prev, pltpu.roll(z, 1, 0))
    nxt = jnp.where(row == tt - 1, halo_next, pltpu.roll(z, tt - 1, 0))
    zsh = z + mup_ref[...] * (prev - z) + mun_ref[...] * (nxt - z)

    r = zsh[:, 0:BW]
    k = zsh[:, BW:2 * BW]
    v = zsh[:, 2 * BW:3 * BW]
    ones = ones_ref[...]
    kk = k * kk_ref[...]
    ss = _head_sums(kk * kk, ones)
    kk = kk * lax.rsqrt(jnp.maximum(ss, 1e-24))
    rvk_out[0, :, 0:BW] = r.astype(BF16)
    rvk_out[0, :, BW:2 * BW] = v.astype(BF16)
    rvk_out[0, :, 2 * BW:3 * BW] = kk.astype(BF16)
    kd_sum = None
    for d in range(2):
        wd = zsh[:, 3 * BW + d * A_LORA:3 * BW + (d + 1) * A_LORA]
        ad = zsh[:, 3 * BW + 2 * A_LORA + d * A_LORA:3 * BW + 2 * A_LORA + (d + 1) * A_LORA]
        u = -(w0_ref[d:d + 1, :] + _mm(jnp.tanh(wd), wup_ref[d], passes=3))
        softplus = jnp.maximum(u, 0.0) + jnp.log(1.0 + jnp.exp(-jnp.abs(u)))
        w_log = -softplus - 0.5
        lw = -jnp.exp(w_log)
        a = _sigmoid(a0_ref[d:d + 1, :] + _mm(ad, aup_ref[d], passes=3))
        kd = k * (1.0 + (a - 1.0) * ka_ref[...])
        lw_out[d, 0] = lw
        dkn_out[d, 0, :, 0:BW] = kd.astype(BF16)
        dkn_out[d, 0, :, BW:2 * BW] = (-(kk * a)).astype(BF16)
        kd_sum = kd if kd_sum is None else kd_sum + kd
    bonus_out[0] = _head_sums(r * kd_sum * rk_ref[...], ones) * v


def _inprep_call(xall, modl, norm_g, lp, ones_a, ctx_len):
    B, S, _ = xall.shape
    tt = TOK_TILE
    n_tiles = S // tt
    hb = tt // 8
    n8 = S // 8
    row = _mod_row(ctx_len // tt, B)
    mod3 = modl.reshape(modl.shape[0], 1, 3 * D_MODEL)
    mod = lambda col: pl.BlockSpec((1, 1, D_MODEL), lambda b, t: (row(b, t), 0, col))
    once = lambda shape: pl.BlockSpec(shape, lambda b, t: (0,) * len(shape), pipeline_mode=pl.Buffered(1))
    s3 = lambda w: pl.BlockSpec((1, tt, w), lambda b, t: (b, t, 0))
    s4 = lambda w: pl.BlockSpec((2, 1, tt, w), lambda b, t: (0, b, t, 0))
    o3 = lambda w, dt: jax.ShapeDtypeStruct((B, S, w), dt)
    o4 = lambda w, dt: jax.ShapeDtypeStruct((2, B, S, w), dt)
    n_mix = sum(w for _, w in IN_SEGS_MIX)
    return pl.pallas_call(
        functools.partial(_inprep_kernel, ctx_len // tt, n_tiles),
        out_shape=(o3(B_COLS_P, F32), o3(C_COLS, F32), o3(3 * BW, BF16), o4(2 * BW, BF16),
                   o4(BW, F32), o3(BW, F32)),
        grid=(B, n_tiles),
        in_specs=[
            s3(D_MODEL),
            pl.BlockSpec((1, 8, D_MODEL), lambda b, t: (b, jnp.maximum(t * hb - 1, 0), 0)),
            pl.BlockSpec((1, 8, D_MODEL), lambda b, t: (b, jnp.minimum((t + 1) * hb, n8 - 1), 0)),
            mod(0), mod(1), once((1, D_MODEL)), once((D_MODEL, n_mix)),
            once((1, A_COLS)), once((1, A_COLS)),
            once((2, BW)), once((2, A_LORA, BW)), once((2, BW)), once((2, A_LORA, BW)),
            once((1, BW)), once((1, BW)), once((1, BW)), once((MXU_W, MXU_W)),
        ],
        out_specs=(s3(B_COLS_P), s3(C_COLS), s3(3 * BW), s4(2 * BW), s4(BW), s3(BW)),
        scratch_shapes=[pltpu.VMEM((tt + 16, D_MODEL), BF16), pltpu.VMEM((tt + 16, A_COLS), F32)],
        compiler_params=_params(("parallel", "arbitrary")),
        name="inprep",
    )(xall, xall, xall, mod3, mod3, norm_g.reshape(1, D_MODEL), lp["w_mix"],
      lp["mu_prev"], lp["mu_next"], lp["w0"], lp["w_up"], lp["a0"], lp["a_up"],
      lp["k_k"], lp["k_a"], lp["r_k"], ones_a)


def _scan_kernel(passes, f_rvk, f_dkn, f_lw, b_rvk, b_dkn, b_lw, yf_ref, yb_ref, state_ref):
    C = CHUNK

    @pl.when(pl.program_id(1) == 0)
    def _():
        state_ref[...] = jnp.zeros_like(state_ref)

    lane = lax.broadcasted_iota(jnp.int32, (1, LANE), 1)
    lo = lane < A_HEAD
    row2 = lax.broadcasted_iota(jnp.int32, (2 * C, 2 * C), 0)
    col2 = lax.broadcasted_iota(jnp.int32, (2 * C, 2 * C), 1)
    rows = lax.broadcasted_iota(jnp.int32, (C, C), 0)
    cols = lax.broadcasted_iota(jnp.int32, (C, C), 1)
    n_pairs = A_HEADS // 2

    n_rows = f_rvk.shape[0]
    dirs = []
    for row, d in [(row, d) for row in range(n_rows) for d in range(2)]:
        rvk_ref, dkn_ref, lw_ref = (f_rvk, f_dkn, f_lw) if d == 0 else (b_rvk, b_dkn, b_lw)
        order = (rows - cols) if d == 0 else (cols - rows)
        lw = lw_ref[0, row]
        kd = dkn_ref[0, row, :, 0:BW].astype(F32)
        nb = dkn_ref[0, row, :, BW:2 * BW].astype(F32)
        cum = _mm_exact_lhs((order >= 0).astype(BF16), lw)
        tot = cum[C - 1:C, :] if d == 0 else cum[0:1, :]
        p_inv = jnp.exp(-cum)
        p_tot = jnp.exp(tot - cum)
        kt = rvk_ref[row, :, 2 * BW:3 * BW].astype(F32) * jnp.exp(cum - lw)
        order2 = (row2 % C - col2 % C) if d == 0 else (col2 % C - row2 % C)
        keep = order2 >= jnp.where(row2 < C, 0, 1)
        e_rows = jnp.broadcast_to(jnp.exp(tot), (LANE, BW))
        dirs.append(dict(
            keep=keep, kt=kt, v=rvk_ref[row, :, BW:2 * BW].astype(F32),
            y_ref=yf_ref if d == 0 else yb_ref, row=row,
            lhs=jnp.concatenate([rvk_ref[row, :, 0:BW].astype(F32) * jnp.exp(cum), kt], axis=0),
            rhs_e=jnp.concatenate([nb * p_inv, kd * p_inv], axis=0),
            rhs_o=jnp.concatenate([kd * p_inv, nb * p_inv], axis=0),
            nkt=jnp.concatenate([nb * p_tot, kd * p_tot, e_rows], axis=0).T))

    units = [(d, h) for d in range(len(dirs)) for h in range(A_HEADS)]
    slab = lambda h: slice((h // 2) * LANE, (h // 2 + 1) * LANE)
    own = lambda h: lo if h % 2 == 0 else jnp.logical_not(lo)

    lhs_m, top, bot = [], [], []
    for d, h in units:
        x = dirs[d]
        lm = jnp.where(own(h), x["lhs"][:, slab(h)], 0.0)
        rhs = x["rhs_o" if h % 2 else "rhs_e"][:, slab(h)]
        qk = jnp.where(x["keep"], _mm(lm, rhs, NT, passes), 0.0)
        lhs_m.append(lm)
        top.append(qk[0:C])
        bot.append(qk[C:2 * C])
    pairs = [(d, j) for d in range(len(dirs)) for j in range(n_pairs)]
    eye2 = (lax.broadcasted_iota(jnp.int32, (C, 2 * C), 1) % C
            == lax.broadcasted_iota(jnp.int32, (C, 2 * C), 0)).astype(F32)
    bd = lambda m: jnp.concatenate([jnp.where(lo, m, 0.0), jnp.where(lo, 0.0, m)], axis=0)
    n_pair = [jnp.where(lo, bot[2 * p], bot[2 * p + 1]) for p in range(len(pairs))]
    t_inv = [eye2 + n for n in n_pair]
    pw = [_mm(n, bd(n), NN, passes) for n in n_pair]
    for _ in range(4):
        both = [_mm(jnp.concatenate([t, p], axis=0), bd(p), NN, passes) for t, p in zip(t_inv, pw)]
        t_inv = [t + m[0:C] for t, m in zip(t_inv, both)]
        pw = [m[C:2 * C] for m in both]
    t_inv = [t + _mm(t, bd(p), NN, passes) for t, p in zip(t_inv, pw)]
    wta = []
    for i, (d, h) in enumerate(units):
        t = t_inv[i // 2]
        kt_s = dirs[d]["kt"][:, slab(h)]
        if h % 2 == 0:
            wta.append(_mm(t[:, 0:C], jnp.where(lo, kt_s, bot[i]), NN, passes))
        else:
            rhs = jnp.where(lo, bot[i], kt_s)
            wta.append(_mm(jnp.where(lo, 0.0, t), jnp.concatenate([jnp.zeros_like(rhs), rhs], axis=0),
                           NN, passes))
    st = [state_ref[i] for i in range(len(pairs))]
    vp = [dirs[d]["v"][:, j * LANE:(j + 1) * LANE] for d, j in pairs]
    sv = [(jnp.concatenate([s, v], axis=0), jnp.concatenate([v, s], axis=0)) for s, v in zip(st, vp)]
    qt_u = [_mm(m, sv[i // 2][i % 2], NN, passes) for i, m in enumerate(wta)]
    qt = [jnp.where(lo, qt_u[2 * i], qt_u[2 * i + 1]) for i in range(len(pairs))]
    qv = [(jnp.concatenate([q, v], axis=0), jnp.concatenate([v, q], axis=0)) for q, v in zip(qt, vp)]
    y_u = []
    for i, (d, h) in enumerate(units):
        p = i // 2
        y_l = jnp.concatenate([top[i], lhs_m[i][0:C]], axis=1)
        y_r = jnp.concatenate([qv[p][h % 2], st[p], st[p]], axis=0)
        y_u.append(_mm(y_l, y_r, NN, passes))
    for p, (d, j) in enumerate(pairs):
        sl = slice(j * LANE, (j + 1) * LANE)
        dirs[d]["y_ref"][dirs[d]["row"], :, sl] = jnp.where(lo, y_u[2 * p], y_u[2 * p + 1])
        nk = dirs[d]["nkt"][sl]
        upd = _mm(nk[:, 0:2 * C], qv[p][0], NN, passes)
        dec = nk[:, 2 * C:] * jnp.concatenate([st[p], st[p]], axis=0)
        state_ref[p] = jnp.where(lo, (dec + upd)[0:A_HEAD], (dec + upd)[A_HEAD:])


def _scan_call(rvk, dkn, lw, ctx_len, passes):
    B, S, _ = rvk.shape
    C = CHUNK
    R = SCAN_ROWS
    nc = S // C
    lc = ctx_len // C
    rev = lambda c: jnp.where(c < lc, lc - 1 - c, nc - 1 - (c - lc))
    f3 = lambda w: pl.BlockSpec((R, C, w), lambda b, c: (b, c, 0))
    b3 = lambda w: pl.BlockSpec((R, C, w), lambda b, c: (b, rev(c), 0))
    f4 = lambda w: pl.BlockSpec((1, R, C, w), lambda b, c: (0, b, c, 0))
    b4 = lambda w: pl.BlockSpec((1, R, C, w), lambda b, c: (1, b, rev(c), 0))
    out = jax.ShapeDtypeStruct((B, S, BW), F32)
    return pl.pallas_call(
        functools.partial(_scan_kernel, passes),
        out_shape=(out, out),
        grid=(B // R, nc),
        in_specs=[f3(3 * BW), f4(2 * BW), f4(BW), b3(3 * BW), b4(2 * BW), b4(BW)],
        out_specs=(f3(BW), b3(BW)),
        scratch_shapes=[pltpu.VMEM((R * A_HEADS, A_HEAD, LANE), F32)],
        compiler_params=_params(("parallel", "arbitrary")),
        name="scan",
    )(rvk, dkn, lw, rvk, dkn, lw)


def _rope(x, tab_ref, shift):
    return (x * tab_ref[0] + pltpu.roll(x, shift, 1) * tab_ref[1]
            + pltpu.roll(x, LANE - shift, 1) * tab_ref[2])


def _bcproj_kernel(zb_ref, zc_ref, tbq_ref, tbk_ref, tcq_ref, tck_ref, qln_ref, kvln_ref,
                   wuq_ref, wuk_ref, wuv_ref, onesb_ref, onesc_ref,
                   qb_out, kb_out, vb_out, qc_out, kc_out, vc_out):
    ones_b = onesb_ref[...]
    ones_c = onesc_ref[...]
    zb = zb_ref[0]
    cq = zb[:, 0:B_Q_LORA]
    ckv = zb[:, B_Q_LORA:B_Q_LORA + B_KV_LORA]
    kr_slab = zb[:, B_Q_LORA + B_KV_LORA:B_COLS_P]
    cqn = cq * lax.rsqrt(jnp.mean(cq * cq, axis=-1, keepdims=True) + NORM_EPS) * qln_ref[...]
    ckvn = ckv * lax.rsqrt(jnp.mean(ckv * ckv, axis=-1, keepdims=True) + NORM_EPS) * kvln_ref[...]
    q_all = _dot(cqn.astype(BF16), wuq_ref[...])
    k_all = _dot(ckvn.astype(BF16), wuk_ref[...])
    vb_out[0] = _dot(ckvn.astype(BF16), wuv_ref[...]).astype(BF16)
    for h in range(B_HEADS):
        sl = slice(h * LANE, (h + 1) * LANE)
        q = q_all[:, sl]
        rs = lax.rsqrt(_mm_exact_rhs(q * q, ones_b) * (1.0 / B_QK) + NORM_EPS)
        qb_out[0, :, sl] = (_rope(q, tbq_ref, B_ROPE // 4) * rs).astype(BF16)
        k = k_all[:, sl] + kr_slab
        rs = lax.rsqrt(_mm_exact_rhs(k * k, ones_b) * (1.0 / B_QK) + NORM_EPS)
        kb_out[0, :, sl] = (_rope(k, tbk_ref, B_ROPE // 4) * rs).astype(BF16)

    zc = zc_ref[0]
    lane = lax.broadcasted_iota(jnp.int32, (1, LANE), 1)
    lo = lane < C_HEAD

    def head_norm(x, tab_ref):
        ms = _mm_exact_rhs(x * x, ones_c) * (1.0 / C_HEAD)
        return _rope(x, tab_ref, C_HEAD // 4) * lax.rsqrt(ms + NORM_EPS)

    for j in range(C_HEADS // 2):
        x = head_norm(zc[:, j * LANE:(j + 1) * LANE], tcq_ref)
        xr = pltpu.roll(x, C_HEAD, 1)
        g = (2 * j) // C_GROUP
        for half in range(2):
            h = 2 * j + half
            src = x if half == g else xr
            keep = lo if g == 0 else jnp.logical_not(lo)
            qc_out[0, :, h * LANE:(h + 1) * LANE] = jnp.where(keep, src, 0.0).astype(BF16)
    kc_out[0] = head_norm(zc[:, BW:BW + C_KV_W], tck_ref).astype(BF16)
    vv = zc[:, BW + C_KV_W:BW + 2 * C_KV_W]
    vr = pltpu.roll(vv, C_HEAD, 1)
    vc_out[0, :, 0:LANE] = jnp.where(lo, vv, vr).astype(BF16)
    vc_out[0, :, LANE:2 * LANE] = jnp.where(lo, vr, vv).astype(BF16)


def _bcproj_call(zb, zc, lp):
    B, S, _ = zb.shape
    tt = TOK_TILE
    full = lambda shape: pl.BlockSpec(shape, lambda t, b: (0,) * len(shape))
    tok = lambda w: pl.BlockSpec((1, tt, w), lambda t, b: (b, t, 0))
    tab = pl.BlockSpec((3, tt, LANE), lambda t, b: (0, t, 0))
    shp = lambda w: jax.ShapeDtypeStruct((B, S, w), BF16)
    return pl.pallas_call(
        _bcproj_kernel,
        out_shape=(shp(B_HEADS * LANE), shp(B_HEADS * LANE), shp(BW),
                   shp(C_HEADS * LANE), shp(C_KV_W), shp(2 * LANE)),
        grid=(S // tt, B),
        in_specs=[tok(B_COLS_P), tok(C_COLS), tab, tab, tab, tab,
                  full((1, B_Q_LORA)), full((1, B_KV_LORA)),
                  full((B_Q_LORA, B_HEADS * LANE)), full((B_KV_LORA, B_HEADS * LANE)),
                  full((B_KV_LORA, BW)), full((LANE, LANE)), full((LANE, LANE))],
        out_specs=(tok(B_HEADS * LANE), tok(B_HEADS * LANE), tok(BW),
                   tok(C_HEADS * LANE), tok(C_KV_W), tok(2 * LANE)),
        compiler_params=_params(("parallel", "arbitrary")),
        name="bcproj",
    )(zb, zc, lp["tab_bq"], lp["tab_bk"], lp["tab_cq"], lp["tab_ck"],
      lp["q_ln"], lp["kv_ln"], lp["w_uq"], lp["w_uk"], lp["w_uv"],
      _block_ones(LANE, LANE), _block_ones(LANE, C_HEAD))


def _attb_kernel(q_ref, k_ref, v_ref, o_ref):
    lane = lax.broadcasted_iota(jnp.int32, (1, LANE), 1)
    lo = lane < B_V
    for j in range(B_HEADS // 2):
        outs = []
        for h in (2 * j, 2 * j + 1):
            sl = slice(h * LANE, (h + 1) * LANE)
            s = _dot(q_ref[:, sl], k_ref[0, :, sl], NT)
            m = jnp.max(s, axis=-1, keepdims=True)
            p = jnp.exp2(s - m)
            den = jnp.sum(p, axis=-1, keepdims=True)
            o = _dot(p.astype(BF16), v_ref[0, :, j * LANE:(j + 1) * LANE])
            outs.append(o / den)
        o_ref[:, j * LANE:(j + 1) * LANE] = jnp.where(lo, outs[0], outs[1])


def _row_block(n_rows, width, seq_len, row0):
    return pl.BlockSpec(
        (pl.Element(n_rows), pl.Element(width)),
        lambda b, t: (pl.multiple_of(b * seq_len + row0 + t * n_rows, TOK_TILE), 0))


def _attb_call(qb, kb, vb, row0, n_rows, tq, n_keys):
    B, S, _ = qb.shape
    width = B_HEADS * LANE
    tiles = n_rows // tq
    return pl.pallas_call(
        _attb_kernel,
        out_shape=jax.ShapeDtypeStruct((B * n_rows, BW), F32),
        grid=(B, tiles),
        in_specs=[
            _row_block(tq, width, S, row0),
            pl.BlockSpec((1, n_keys, width), lambda b, t: (b, 0, 0)),
            pl.BlockSpec((1, n_keys, BW), lambda b, t: (b, 0, 0)),
        ],
        out_specs=pl.BlockSpec((tq, BW), lambda b, t: (b * tiles + t, 0)),
        compiler_params=_params(("parallel", "arbitrary")),
        name="attb",
    )(qb.reshape(B * S, width), kb, vb).reshape(B, n_rows, BW)


def _attc_kernel(tile0, ctx_len, n_lat, sink_ref, q_ref, k_ref, v_ref, *rest):
    valid_refs, o_ref = rest[:-1], rest[-1]
    W = WINDOW
    ctx_tiles = ctx_len // W
    n_blk = n_lat // W
    lane = lax.broadcasted_iota(jnp.int32, (1, LANE), 1)
    lo = lane < C_HEAD
    rows4 = lax.broadcasted_iota(jnp.int32, (C_GROUP * W, 1), 0)

    def attend(rows, k_cat, v_cat, mask):
        for g in range(C_KV_HEADS):
            q_st = jnp.concatenate(
                [q_ref[0, rows, h * LANE:(h + 1) * LANE] for h in range(g * C_GROUP, (g + 1) * C_GROUP)],
                axis=0)
            s = _dot(q_st, k_cat, NT)
            if mask is not None:
                s = jnp.where(jnp.concatenate([mask] * C_GROUP, axis=0) > 0.5, s, NEG)
            sk = jnp.zeros((C_GROUP * W, 1), F32)
            for i in range(C_GROUP):
                sk = jnp.where(rows4 // W == i, sink_ref[g * C_GROUP + i], sk)
            m = jnp.maximum(jnp.max(s, axis=-1, keepdims=True), sk)
            p = jnp.exp2(s - m)
            den = jnp.sum(p, axis=-1, keepdims=True) + jnp.exp2(sk - m)
            o = _dot(p.astype(BF16), v_cat[:, g * LANE:(g + 1) * LANE]) / den
            for jj in range(C_GROUP // 2):
                o_lo = o[(2 * jj) * W:(2 * jj + 1) * W]
                o_hi = o[(2 * jj + 1) * W:(2 * jj + 2) * W]
                col = (g * C_GROUP // 2 + jj) * LANE
                o_ref[0, rows, col:col + LANE] = jnp.where(lo, o_lo, o_hi)

    def latent(i, rows):
        j = (pl.program_id(1) + tile0) * ATTC_BLOCKS + i - ctx_tiles
        bm = jnp.maximum(j - 1, 0)
        bp = jnp.minimum(j + 1, n_blk - 1)

        def blk(ref, b):
            return ref[0, pl.ds(pl.multiple_of(ctx_len + b * W, W), W), :]

        k_cat = jnp.concatenate([k_ref[0, 0:ctx_len, :], blk(k_ref, bm), blk(k_ref, j), blk(k_ref, bp)], axis=0)
        v_cat = jnp.concatenate([v_ref[0, 0:ctx_len, :], blk(v_ref, bm), blk(v_ref, j), blk(v_ref, bp)], axis=0)
        attend(rows, k_cat, v_cat, valid_refs[i][0])

    def context(rows):
        attend(rows, k_ref[0, 0:ctx_len, :], v_ref[0, 0:ctx_len, :], None)

    t = pl.program_id(1) + tile0
    ctx_steps = ctx_tiles // ATTC_BLOCKS
    blocks = [(i, slice(i * W, (i + 1) * W)) for i in range(ATTC_BLOCKS)]

    def all_context():
        for _, rows in blocks:
            context(rows)

    def all_latent():
        for i, rows in blocks:
            latent(i, rows)

    if tile0 < ctx_steps:
        pl.when(t < ctx_steps)(all_context)
        pl.when(t >= ctx_steps)(all_latent)
    else:
        all_latent()


def _attc_call(qc, kc, vc, sink, ctx_len, with_ctx):
    B, S, _ = qc.shape
    W = WINDOW
    nb = ATTC_BLOCKS
    ctx_steps = ctx_len // (nb * W)
    tile0 = 0 if with_ctx else ctx_steps
    n_steps = S // (nb * W) - tile0
    n_blk = (S - ctx_len) // W
    valid = _band_valid(ctx_len)

    def variant(t, i):
        j = (t + tile0) * nb + i - ctx_len // W
        return jnp.where(j == 0, 1, 0) + jnp.where(j == n_blk - 1, 2, 0)

    return pl.pallas_call(
        functools.partial(_attc_kernel, tile0, ctx_len, S - ctx_len),
        out_shape=jax.ShapeDtypeStruct((B, n_steps * nb * W, BW), F32),
        grid=(B, n_steps),
        in_specs=[
            pl.BlockSpec(memory_space=pltpu.SMEM),
            pl.BlockSpec((1, nb * W, C_HEADS * LANE), lambda b, t: (b, t + tile0, 0)),
            pl.BlockSpec((1, S, C_KV_W), lambda b, t: (b, 0, 0)),
            pl.BlockSpec((1, S, 2 * LANE), lambda b, t: (b, 0, 0)),
        ] + [pl.BlockSpec((1, W, ctx_len + 3 * W), lambda b, t, i=i: (variant(t, i), 0, 0)) for i in range(nb)],
        out_specs=pl.BlockSpec((1, nb * W, BW), lambda b, t: (b, t, 0)),
        compiler_params=_params(("parallel", "arbitrary")),
        name="attc",
    )(sink, qc, kc, vc, *([valid] * nb))


def _band_valid(ctx_len):
    W = WINDOW
    qi = np.arange(W)[:, None]
    cb = np.arange(ctx_len + 3 * W)[None, :] - ctx_len
    base = (cb < 0) | (np.abs(cb - W - qi) <= WINDOW)
    out = []
    for variant in range(4):
        first, last = bool(variant & 1), bool(variant & 2)
        ok = base.copy()
        if first:
            ok &= ~((cb >= 0) & (cb < W))
        if last:
            ok &= ~(cb >= 2 * W)
        out.append(ok)
    return jnp.asarray(np.stack(out), dtype=F32)


def _merge_kernel(tile0, ctx_tiles, y0_ref, y1_ref, bonus_ref, ybc_ref, ybl_ref, yc_ref, x_ref,
                  shift_ref, scale_ref, gate_ref,
                  g_ref, wg_ref, gng_ref, gnb_ref, ones_ref, wbo_ref, wout_ref, o_ref):
    x = x_ref[0]
    hb = _modulated_norm(x, g_ref[...], shift_ref[0], scale_ref[0]).astype(BF16)
    ones = ones_ref[...]
    y = y0_ref[0] + y1_ref[0]
    mu = _head_sums(y, ones) * (1.0 / A_HEAD)
    dlt = y - mu
    var = _head_sums(dlt * dlt, ones) * (1.0 / A_HEAD)
    ya = dlt * lax.rsqrt(var + A_GN_EPS) * gng_ref[...] + gnb_ref[...] + bonus_ref[0]
    m = None
    zg0 = N_BRANCH * BW
    yb = jnp.where(pl.program_id(1) + tile0 < ctx_tiles, ybc_ref[0], ybl_ref[0])
    for n, y_n in enumerate((ya, yb, yc_ref[0])):
        g = _dot(hb, wg_ref[:, n * BW:(n + 1) * BW])
        u = (y_n * _silu(g)).astype(BF16)
        zg = _dot(hb, wg_ref[:, zg0 + n * D_MODEL:zg0 + (n + 1) * D_MODEL])
        term = _sigmoid(zg) * _dot(u, wbo_ref[n])
        m = term if m is None else m + term
    o_ref[0] = x + gate_ref[0] * _dot(m.astype(BF16), wout_ref[...])


def _merge_call(y0, y1, bonus, yb_ctx, yb_lat, yc, xall, modl, norm_g, lp, ones_a, ctx_len, with_ctx):
    B, S, _ = xall.shape
    tt = TOK_TILE
    ctx_tiles = ctx_len // tt
    tile0 = 0 if with_ctx else ctx_tiles
    n_tiles = S // tt - tile0
    row = _mod_row(ctx_tiles, B)
    mod3 = modl.reshape(modl.shape[0], 1, 3 * D_MODEL)
    mod = lambda col: pl.BlockSpec((1, 1, D_MODEL), lambda b, t: (row(b, t + tile0), 0, col))
    once = lambda shape: pl.BlockSpec(shape, lambda b, t: (0,) * len(shape), pipeline_mode=pl.Buffered(1))
    tok = lambda w: pl.BlockSpec((1, tt, w), lambda b, t: (b, t + tile0, 0))
    att = lambda w: pl.BlockSpec((1, tt, w), lambda b, t: (b, t, 0))
    n_gate = sum(w for _, w in IN_SEGS_GATE)
    ybc = pl.BlockSpec((1, tt, BW), lambda b, t: (b, jnp.minimum(t + tile0, ctx_tiles - 1), 0))
    ybl = pl.BlockSpec((1, tt, BW), lambda b, t: (b, jnp.maximum(t + tile0 - ctx_tiles, 0), 0))
    return pl.pallas_call(
        functools.partial(_merge_kernel, tile0, ctx_tiles),
        out_shape=jax.ShapeDtypeStruct((B, n_tiles * tt, D_MODEL), F32),
        grid=(B, n_tiles),
        in_specs=[tok(BW), tok(BW), tok(BW), ybc, ybl, att(BW), tok(D_MODEL),
                  mod(0), mod(1), mod(2),
                  once((1, D_MODEL)), once((D_MODEL, n_gate)),
                  once((1, BW)), once((1, BW)), once((MXU_W, MXU_W)),
                  once((N_BRANCH, BW, D_MODEL)), once((D_MODEL, D_MODEL))],
        out_specs=pl.BlockSpec((1, tt, D_MODEL), lambda b, t: (b, t, 0)),
        compiler_params=_params(("parallel", "arbitrary")),
        name="merge",
    )(y0, y1, bonus, yb_ctx, yb_lat, yc, xall, mod3, mod3, mod3, norm_g.reshape(1, D_MODEL), lp["w_gate"],
      lp["gn_g"], lp["gn_b"], ones_a, lp["w_bo"], lp["w_out"])


def _rope_tables(n_ctx, n_lat, rot_dim, head_w, lane0):
    rows = n_lat // GRID_W
    row = jnp.repeat(jnp.arange(rows), GRID_W).astype(F32)
    colp = jnp.tile(jnp.arange(GRID_W), rows).astype(F32)
    q = rot_dim // 4
    inv = ROPE_THETA ** (-(2.0 * jnp.arange(q, dtype=F32)) / (rot_dim // 2))
    ang = jnp.concatenate([row[:, None] * inv, colp[:, None] * inv], axis=-1)
    cos, sin = jnp.cos(ang), jnp.sin(ang)
    zero = jnp.zeros_like(sin[:, :q])
    cos_g = jnp.concatenate([cos[:, :q], cos[:, :q], cos[:, q:], cos[:, q:]], axis=-1)
    sp_g = jnp.concatenate([zero, sin[:, :q], zero, sin[:, q:]], axis=-1)
    sm_g = jnp.concatenate([-sin[:, :q], zero, -sin[:, q:], zero], axis=-1)

    def place(t, fill):
        pad_l = jnp.full((n_lat, lane0), fill, F32)
        pad_r = jnp.full((n_lat, head_w - lane0 - rot_dim), fill, F32)
        grp = jnp.concatenate([pad_l, t, pad_r], axis=-1)
        lat = jnp.tile(grp, (1, LANE // head_w))
        return jnp.concatenate([jnp.full((n_ctx, LANE), fill, F32), lat], axis=0)

    return jnp.stack([place(cos_g, 1.0), place(sp_g, 0.0), place(sm_g, 0.0)])


def _block_ones(n, blk):
    i = np.arange(n) // blk
    return jnp.asarray(i[:, None] == i[None, :], dtype=BF16)


def _pad_lanes(g, width):
    return jnp.pad(g, (0, width - g.shape[0]))


def _fold_gain(tab, gain, shift, scale):
    g = jnp.stack([gain, jnp.roll(gain, shift), jnp.roll(gain, -shift)]) * scale
    return tab * g[:, None, :]


def _layer_params(i, tab_b, tab_c, w_in, a_mu_prev, a_mu_next, a_w0, a_w_up, a_a0, a_a_up, a_k_k, a_k_a, a_r_k,
                  a_gn_g, a_gn_b, b_q_ln, b_kv_ln, b_w_uq, b_w_ukv, b_qn_g, b_kn_g,
                  c_qn_g, c_kn_g, c_sink, w_branch_out, w_out):
    w = w_in[i].astype(BF16)
    c0 = 0
    segs = {}
    for name, width in (("za", A_COLS), ("ga", BW), ("zb", B_COLS), ("gb", BW),
                        ("zc", C_COLS), ("gc", BW), ("zg", N_BRANCH * D_MODEL)):
        segs[name] = w[:, c0:c0 + width]
        c0 += width
    zb = segs["zb"]
    zpad = lambda n: jnp.zeros((D_MODEL, n), w.dtype)
    segs["zb"] = jnp.concatenate(
        [zb[:, :B_Q_LORA + B_KV_LORA], zpad(B_NOPE), zb[:, B_Q_LORA + B_KV_LORA:], zpad(LANE - B_NOPE - B_ROPE)],
        axis=-1)
    w_mix = jnp.concatenate([segs[n] for n, _ in IN_SEGS_MIX], axis=-1).astype(BF16)
    w_gate = jnp.concatenate([segs[n] for n, _ in IN_SEGS_GATE], axis=-1).astype(BF16)

    uq = b_w_uq[i].reshape(B_Q_LORA, B_HEADS, B_QK)
    uq = jnp.pad(uq, ((0, 0), (0, 0), (0, LANE - B_QK))).reshape(B_Q_LORA, B_HEADS * LANE)
    ukv = b_w_ukv[i].reshape(B_KV_LORA, B_HEADS, B_NOPE + B_V)
    uk = jnp.pad(ukv[:, :, :B_NOPE], ((0, 0), (0, 0), (0, LANE - B_NOPE))).reshape(B_KV_LORA, B_HEADS * LANE)
    uv = ukv[:, :, B_NOPE:].reshape(B_KV_LORA, BW)
    row = lambda t: t.reshape(1, -1)
    return dict(
        w_mix=w_mix, w_gate=w_gate,
        mu_prev=row(a_mu_prev[i]), mu_next=row(a_mu_next[i]),
        w0=a_w0[i], w_up=a_w_up[i], a0=a_a0[i], a_up=a_a_up[i],
        k_k=row(a_k_k[i]), k_a=row(a_k_a[i]), r_k=row(a_r_k[i]),
        gn_g=row(a_gn_g[i]), gn_b=row(a_gn_b[i]),
        q_ln=row(b_q_ln[i]), kv_ln=row(b_kv_ln[i]),
        w_uq=uq.astype(BF16), w_uk=uk.astype(BF16), w_uv=uv.astype(BF16),
        tab_bq=_fold_gain(tab_b, _pad_lanes(b_qn_g[i], LANE), B_ROPE // 4, B_QK ** -0.5 * LOG2E),
        tab_bk=_fold_gain(tab_b, _pad_lanes(b_kn_g[i], LANE), B_ROPE // 4, 1.0),
        tab_cq=_fold_gain(tab_c, jnp.tile(c_qn_g[i], 2), C_HEAD // 4, C_HEAD ** -0.5 * LOG2E),
        tab_ck=_fold_gain(tab_c, jnp.tile(c_kn_g[i], 2), C_HEAD // 4, 1.0),
        sink=c_sink[i] * LOG2E,
        w_bo=w_branch_out[i].astype(BF16), w_out=w_out[i].astype(BF16),
    )


SCAN_PASSES = 1


def kernel(x, c, ctx, c_ctx, ada_w, ada_b, norm_g, w_in, a_mu_prev, a_mu_next, a_w0, a_w_up, a_a0, a_a_up, a_k_k, a_k_a, a_r_k, a_gn_g, a_gn_b, b_q_ln, b_kv_ln, b_w_uq, b_w_ukv, b_qn_g, b_kn_g, c_qn_g, c_kn_g, c_sink, w_branch_out, w_out):
    B, T, D = x.shape
    L = ctx.shape[1]
    depth = ada_w.shape[0]
    assert D == D_MODEL and L % TOK_TILE == 0 and T % TOK_TILE == 0 and T % GRID_W == 0
    assert T % ATTB_TQ == 0 and B % SCAN_ROWS == 0 and L % (ATTC_BLOCKS * WINDOW) == 0

    mod_rows = -(-(B + 1) // 8) * 8
    cs = jnp.concatenate([c, c_ctx[None, :], jnp.zeros((mod_rows - B - 1, D), c.dtype)], axis=0)
    mod = _mod_call(cs, ada_w, ada_b)

    tab_b = _rope_tables(L, T, B_ROPE, LANE, B_NOPE)
    tab_c = _rope_tables(L, T, C_HEAD, C_HEAD, 0)
    ones_a = _block_ones(MXU_W, A_HEAD)

    xall = jnp.concatenate([ctx, x], axis=1)
    for i in range(depth):
        with_ctx = i < depth - 1
        lp = _layer_params(i, tab_b, tab_c, w_in, a_mu_prev, a_mu_next, a_w0, a_w_up, a_a0, a_a_up, a_k_k, a_k_a,
                           a_r_k, a_gn_g, a_gn_b, b_q_ln, b_kv_ln, b_w_uq, b_w_ukv, b_qn_g, b_kn_g,
                           c_qn_g, c_kn_g, c_sink, w_branch_out, w_out)
        zb, zc, rvk, dkn, lw, bonus = _inprep_call(xall, mod[i], norm_g[i], lp, ones_a, L)
        y0, y1 = _scan_call(rvk, dkn, lw, L, SCAN_PASSES)
        qb, kb, vb, qc, kc, vc = _bcproj_call(zb, zc, lp)
        yb_lat = _attb_call(qb, kb, vb, L, T, ATTB_TQ, L + T)
        yb_ctx = _attb_call(qb, kb, vb, 0, L, TOK_TILE, L) if with_ctx else yb_lat
        yc = _attc_call(qc, kc, vc, lp["sink"], L, with_ctx)
        xall = _merge_call(y0, y1, bonus, yb_ctx, yb_lat, yc, xall, mod[i], norm_g[i], lp, ones_a, L,
                           with_ctx)
    return xall
```

```python
import functools

import numpy as np
import jax
import jax.numpy as jnp
from jax import lax
from jax.experimental import pallas as pl
from jax.experimental.pallas import tpu as pltpu

F32 = jnp.float32
BF16 = jnp.bfloat16

D_MODEL = 1024
GRID_W = 64
ROPE_THETA = 10000.0
NORM_EPS = 1e-6
NEG = -1e30
BW = 512
N_BRANCH = 3

A_HEAD = 64
A_HEADS = BW // A_HEAD
A_LORA = 64
A_GN_EPS = 64e-5
A_COLS = 3 * BW + 4 * A_LORA

B_HEADS = 8
B_NOPE = 64
B_ROPE = 32
B_QK = B_NOPE + B_ROPE
B_V = BW // B_HEADS
B_Q_LORA = 256
B_KV_LORA = 128
B_COLS = B_Q_LORA + B_KV_LORA + B_ROPE
B_COLS_P = 512

C_HEAD = 64
C_HEADS = BW // C_HEAD
C_KV_HEADS = 2
C_GROUP = C_HEADS // C_KV_HEADS
C_KV_W = C_KV_HEADS * C_HEAD
C_COLS = BW + 2 * C_KV_W
WINDOW = 128

LANE = 128
MXU_W = 256
CHUNK = 64
SCAN_ROWS = 4
TOK_TILE = 256
VMEM_LIMIT = 56 * 1024 * 1024

ATTB_TQ = 512
ATTC_BLOCKS = 2
LOG2E = 1.4426950408889634

IN_SEGS_MIX = (("za", A_COLS), ("zb", B_COLS_P), ("zc", C_COLS))
IN_SEGS_GATE = (("ga", BW), ("gb", BW), ("gc", BW), ("zg", N_BRANCH * D_MODEL))

NN = (((1,), (0,)), ((), ()))
NT = (((1,), (1,)), ((), ()))


def _dot(a, b, dims=NN):
    return lax.dot_general(a, b, dims, preferred_element_type=F32)


def _split(x):
    hi = x.astype(BF16)
    lo = (x - hi.astype(F32)).astype(BF16)
    return hi, lo


def _mm(a, b, dims=NN, passes=1):
    if passes == 1:
        return _dot(a.astype(BF16), b.astype(BF16), dims)
    ah, al = _split(a)
    bh, bl = _split(b)
    return _dot(ah, bh, dims) + (_dot(ah, bl, dims) + _dot(al, bh, dims))


def _mm_exact_rhs(a, b_bf16, parts=2):
    acc = None
    rem = a
    for _ in range(parts):
        p = rem.astype(BF16)
        t = _dot(p, b_bf16)
        acc = t if acc is None else acc + t
        rem = rem - p.astype(F32)
    return acc


def _head_sums(a, ones_blk):
    w = ones_blk.shape[0]
    return jnp.concatenate(
        [_mm_exact_rhs(a[:, c:c + w], ones_blk) for c in range(0, a.shape[1], w)], axis=1)


def _mm_exact_lhs(a_bf16, b, parts=3):
    acc = None
    rem = b
    for _ in range(parts):
        p = rem.astype(BF16)
        t = _dot(a_bf16, p)
        acc = t if acc is None else acc + t
        rem = rem - p.astype(F32)
    return acc


def _sigmoid(x):
    return 1.0 / (1.0 + jnp.exp(-x))


def _silu(x):
    return x * _sigmoid(x)


def _modulated_norm(x, g, shift, scale):
    ms = jnp.mean(x * x, axis=-1, keepdims=True)
    return (x * lax.rsqrt(ms + NORM_EPS) * g) * (1.0 + scale) + shift


def _params(sem):
    return pltpu.CompilerParams(dimension_semantics=sem, vmem_limit_bytes=VMEM_LIMIT)


def _mod_kernel(c_ref, w_ref, b_ref, o_ref):
    s = _silu(c_ref[...])
    o_ref[0] = _mm(s, w_ref[0], passes=3) + b_ref[0]


def _mod_call(cs, ada_w, ada_b):
    n_layers = ada_w.shape[0]
    rows = cs.shape[0]
    return pl.pallas_call(
        _mod_kernel,
        out_shape=jax.ShapeDtypeStruct((n_layers, rows, 3 * D_MODEL), F32),
        grid=(n_layers, 3),
        in_specs=[
            pl.BlockSpec((rows, D_MODEL), lambda l, n: (0, 0)),
            pl.BlockSpec((1, D_MODEL, D_MODEL), lambda l, n: (l, 0, n)),
            pl.BlockSpec((1, 1, D_MODEL), lambda l, n: (l, 0, n)),
        ],
        out_specs=pl.BlockSpec((1, rows, D_MODEL), lambda l, n: (l, 0, n)),
        compiler_params=_params(("arbitrary", "arbitrary")),
        name="mod",
    )(cs, ada_w, ada_b.reshape(n_layers, 1, 3 * D_MODEL))


def _mod_row(ctx_tiles, n_batch_rows):
    return lambda b, t: jnp.where(t < ctx_tiles, n_batch_rows, b)


def _inprep_kernel(ctx_tiles, n_tiles,
                   x_ref, xp_ref, xn_ref, shift_ref, scale_ref, g_ref, w_ref,
                   mup_ref, mun_ref, w0_ref, wup_ref, a0_ref, aup_ref, kk_ref, ka_ref, rk_ref, ones_ref,
                   zb_out, zc_out, rvk_out, dkn_out, lw_out, bonus_out, hb_ref, za_ref):
    t = pl.program_id(1)
    tt = TOK_TILE
    first = jnp.logical_or(t == 0, t == ctx_tiles)
    last = jnp.logical_or(t == ctx_tiles - 1, t == n_tiles - 1)
    g, shift, scale = g_ref[...], shift_ref[0], scale_ref[0]
    hb_ref[0:tt, :] = _modulated_norm(x_ref[0], g, shift, scale).astype(BF16)
    hb_ref[tt:tt + 8, :] = _modulated_norm(xp_ref[0], g, shift, scale).astype(BF16)
    hb_ref[tt + 8:tt + 16, :] = _modulated_norm(xn_ref[0], g, shift, scale).astype(BF16)
    hb = hb_ref[...]
    for c0 in range(0, A_COLS, MXU_W):
        za_ref[:, c0:c0 + MXU_W] = _dot(hb, w_ref[:, c0:c0 + MXU_W])
    hb_t = hb[0:tt]
    zb_out[0] = _dot(hb_t, w_ref[:, A_COLS:A_COLS + B_COLS_P])
    for c0 in range(0, C_COLS, MXU_W):
        col = A_COLS + B_COLS_P + c0
        zc_out[0, :, c0:c0 + MXU_W] = _dot(hb_t, w_ref[:, col:col + MXU_W])

    z = za_ref[0:tt, :]
    row = lax.broadcasted_iota(jnp.int32, (tt, 1), 0)
    halo_prev = jnp.where(first, 0.0, za_ref[tt + 7:tt + 8, :])
    halo_next = jnp.where(last, 0.0, za_ref[tt + 8:tt + 9, :])
    prev = jnp.where(row == 0, halo_prev, pltpu.roll(z, 1, 0))
    nxt = jnp.where(row == tt - 1, halo_next, pltpu.roll(z, tt - 1, 0))
    zsh = z + mup_ref[...] * (prev - z) + mun_ref[...] * (nxt - z)

    r = zsh[:, 0:BW]
    k = zsh[:, BW:2 * BW]
    v = zsh[:, 2 * BW:3 * BW]
    ones = ones_ref[...]
    kk = k * kk_ref[...]
    ss = _head_sums(kk * kk, ones)
    kk = kk * lax.rsqrt(jnp.maximum(ss, 1e-24))
    rvk_out[0, :, 0:BW] = r
    rvk_out[0, :, BW:2 * BW] = v
    rvk_out[0, :, 2 * BW:3 * BW] = kk
    kd_sum = None
    for d in range(2):
        wd = zsh[:, 3 * BW + d * A_LORA:3 * BW + (d + 1) * A_LORA]
        ad = zsh[:, 3 * BW + 2 * A_LORA + d * A_LORA:3 * BW + 2 * A_LORA + (d + 1) * A_LORA]
        u = -(w0_ref[d:d + 1, :] + _mm(jnp.tanh(wd), wup_ref[d], passes=3))
        softplus = jnp.maximum(u, 0.0) + jnp.log(1.0 + jnp.exp(-jnp.abs(u)))
        w_log = -softplus - 0.5
        lw = -jnp.exp(w_log)
        a = _sigmoid(a0_ref[d:d + 1, :] + _mm(ad, aup_ref[d], passes=3))
        kd = k * (1.0 + (a - 1.0) * ka_ref[...])
        lw_out[d, 0] = lw
        dkn_out[d, 0, :, 0:BW] = kd
        dkn_out[d, 0, :, BW:2 * BW] = -(kk * a)
        kd_sum = kd if kd_sum is None else kd_sum + kd
    bonus_out[0] = _head_sums(r * kd_sum * rk_ref[...], ones) * v


def _inprep_call(xall, modl, norm_g, lp, ones_a, ctx_len):
    B, S, _ = xall.shape
    tt = TOK_TILE
    n_tiles = S // tt
    hb = tt // 8
    n8 = S // 8
    row = _mod_row(ctx_len // tt, B)
    mod3 = modl.reshape(modl.shape[0], 1, 3 * D_MODEL)
    mod = lambda col: pl.BlockSpec((1, 1, D_MODEL), lambda b, t: (row(b, t), 0, col))
    once = lambda shape: pl.BlockSpec(shape, lambda b, t: (0,) * len(shape), pipeline_mode=pl.Buffered(1))
    s3 = lambda w: pl.BlockSpec((1, tt, w), lambda b, t: (b, t, 0))
    s4 = lambda w: pl.BlockSpec((2, 1, tt, w), lambda b, t: (0, b, t, 0))
    o3 = lambda w, dt: jax.ShapeDtypeStruct((B, S, w), dt)
    o4 = lambda w, dt: jax.ShapeDtypeStruct((2, B, S, w), dt)
    n_mix = sum(w for _, w in IN_SEGS_MIX)
    return pl.pallas_call(
        functools.partial(_inprep_kernel, ctx_len // tt, n_tiles),
        out_shape=(o3(B_COLS_P, F32), o3(C_COLS, F32), o3(3 * BW, F32), o4(2 * BW, F32),
                   o4(BW, F32), o3(BW, F32)),
        grid=(B, n_tiles),
        in_specs=[
            s3(D_MODEL),
            pl.BlockSpec((1, 8, D_MODEL), lambda b, t: (b, jnp.maximum(t * hb - 1, 0), 0)),
            pl.BlockSpec((1, 8, D_MODEL), lambda b, t: (b, jnp.minimum((t + 1) * hb, n8 - 1), 0)),
            mod(0), mod(1), once((1, D_MODEL)), once((D_MODEL, n_mix)),
            once((1, A_COLS)), once((1, A_COLS)),
            once((2, BW)), once((2, A_LORA, BW)), once((2, BW)), once((2, A_LORA, BW)),
            once((1, BW)), once((1, BW)), once((1, BW)), once((MXU_W, MXU_W)),
        ],
        out_specs=(s3(B_COLS_P), s3(C_COLS), s3(3 * BW), s4(2 * BW), s4(BW), s3(BW)),
        scratch_shapes=[pltpu.VMEM((tt + 16, D_MODEL), BF16), pltpu.VMEM((tt + 16, A_COLS), F32)],
        compiler_params=_params(("parallel", "arbitrary")),
        name="inprep",
    )(xall, xall, xall, mod3, mod3, norm_g.reshape(1, D_MODEL), lp["w_mix"],
      lp["mu_prev"], lp["mu_next"], lp["w0"], lp["w_up"], lp["a0"], lp["a_up"],
      lp["k_k"], lp["k_a"], lp["r_k"], ones_a)


def _scan_kernel(passes, f_rvk, f_dkn, f_lw, b_rvk, b_dkn, b_lw, yf_ref, yb_ref, state_ref):
    C = CHUNK

    @pl.when(pl.program_id(1) == 0)
    def _():
        state_ref[...] = jnp.zeros_like(state_ref)

    lane = lax.broadcasted_iota(jnp.int32, (1, LANE), 1)
    lo = lane < A_HEAD
    row2 = lax.broadcasted_iota(jnp.int32, (2 * C, 2 * C), 0)
    col2 = lax.broadcasted_iota(jnp.int32, (2 * C, 2 * C), 1)
    rows = lax.broadcasted_iota(jnp.int32, (C, C), 0)
    cols = lax.broadcasted_iota(jnp.int32, (C, C), 1)
    n_pairs = A_HEADS // 2

    n_rows = f_rvk.shape[0]
    dirs = []
    for row, d in [(row, d) for row in range(n_rows) for d in range(2)]:
        rvk_ref, dkn_ref, lw_ref = (f_rvk, f_dkn, f_lw) if d == 0 else (b_rvk, b_dkn, b_lw)
        order = (rows - cols) if d == 0 else (cols - rows)
        lw = lw_ref[0, row]
        kd = dkn_ref[0, row, :, 0:BW]
        nb = dkn_ref[0, row, :, BW:2 * BW]
        cum = _mm_exact_lhs((order >= 0).astype(BF16), lw)
        tot = cum[C - 1:C, :] if d == 0 else cum[0:1, :]
        p_inv = jnp.exp(-cum)
        p_tot = jnp.exp(tot - cum)
        kt = rvk_ref[row, :, 2 * BW:3 * BW] * jnp.exp(cum - lw)
        order2 = (row2 % C - col2 % C) if d == 0 else (col2 % C - row2 % C)
        keep = order2 >= jnp.where(row2 < C, 0, 1)
        e_rows = jnp.broadcast_to(jnp.exp(tot), (LANE, BW))
        dirs.append(dict(
            keep=keep, kt=kt, v=rvk_ref[row, :, BW:2 * BW], fwd=(d == 0),
            y_ref=yf_ref if d == 0 else yb_ref, row=row,
            lhs=jnp.concatenate([rvk_ref[row, :, 0:BW] * jnp.exp(cum), kt], axis=0),
            rhs_e=jnp.concatenate([nb * p_inv, kd * p_inv], axis=0),
            rhs_o=jnp.concatenate([kd * p_inv, nb * p_inv], axis=0),
            nkt=jnp.concatenate([nb * p_tot, kd * p_tot, e_rows], axis=0).T))

    units = [(d, h) for d in range(len(dirs)) for h in range(A_HEADS)]
    slab = lambda h: slice((h // 2) * LANE, (h // 2 + 1) * LANE)
    own = lambda h: lo if h % 2 == 0 else jnp.logical_not(lo)

    lhs_m, top, bot = [], [], []
    for d, h in units:
        x = dirs[d]
        lm = jnp.where(own(h), x["lhs"][:, slab(h)], 0.0)
        rhs = x["rhs_o" if h % 2 else "rhs_e"][:, slab(h)]
        qk = jnp.where(x["keep"], _mm(lm, rhs, NT, passes), 0.0)
        lhs_m.append(lm)
        top.append(qk[0:C])
        bot.append(qk[C:2 * C])
    pairs = [(d, j) for d in range(len(dirs)) for j in range(n_pairs)]
    r_i = lax.broadcasted_iota(jnp.int32, (C, 2 * C), 0)
    c_i = lax.broadcasted_iota(jnp.int32, (C, 2 * C), 1) % C
    eye2 = (r_i == c_i).astype(F32)

    def couple(b, fwd):
        late, early = (r_i, c_i) if fwd else (c_i, r_i)
        return (r_i // (2 * b) == c_i // (2 * b)) & ((late // b) % 2 == 1) & ((early // b) % 2 == 0)

    bd = lambda m: jnp.concatenate([jnp.where(lo, m, 0.0), jnp.where(lo, 0.0, m)], axis=0)
    n_pair = [jnp.where(lo, bot[2 * p], bot[2 * p + 1]) for p in range(len(pairs))]
    fwd_of = [dirs[d]["fwd"] for d, _ in pairs]
    t_inv = [eye2 + jnp.where(couple(1, f), n, 0.0) for n, f in zip(n_pair, fwd_of)]
    b = 2
    while b < C:
        et = [_mm(jnp.where(couple(b, f), n, 0.0), bd(t), NN, passes)
              for n, t, f in zip(n_pair, t_inv, fwd_of)]
        t_inv = [t + _mm(t, bd(m), NN, passes) for t, m in zip(t_inv, et)]
        b *= 2
    wta = []
    for i, (d, h) in enumerate(units):
        t = t_inv[i // 2]
        kt_s = dirs[d]["kt"][:, slab(h)]
        if h % 2 == 0:
            wta.append(_mm(t[:, 0:C], jnp.where(lo, kt_s, bot[i]), NN, passes))
        else:
            rhs = jnp.where(lo, bot[i], kt_s)
            wta.append(_mm(jnp.where(lo, 0.0, t), jnp.concatenate([jnp.zeros_like(rhs), rhs], axis=0),
                           NN, passes))
    st = [state_ref[i] for i in range(len(pairs))]
    vp = [dirs[d]["v"][:, j * LANE:(j + 1) * LANE] for d, j in pairs]
    sv = [(jnp.concatenate([s, v], axis=0), jnp.concatenate([v, s], axis=0)) for s, v in zip(st, vp)]
    qt_u = [_mm(m, sv[i // 2][i % 2], NN, passes) for i, m in enumerate(wta)]
    qt = [jnp.where(lo, qt_u[2 * i], qt_u[2 * i + 1]) for i in range(len(pairs))]
    qv = [(jnp.concatenate([q, v], axis=0), jnp.concatenate([v, q], axis=0)) for q, v in zip(qt, vp)]
    y_u = []
    for i, (d, h) in enumerate(units):
        p = i // 2
        y_l = jnp.concatenate([top[i], lhs_m[i][0:C]], axis=1)
        y_r = jnp.concatenate([qv[p][h % 2], st[p], st[p]], axis=0)
        y_u.append(_mm(y_l, y_r, NN, passes))
    for p, (d, j) in enumerate(pairs):
        sl = slice(j * LANE, (j + 1) * LANE)
        dirs[d]["y_ref"][dirs[d]["row"], :, sl] = jnp.where(lo, y_u[2 * p], y_u[2 * p + 1])
        nk = dirs[d]["nkt"][sl]
        upd = _mm(nk[:, 0:2 * C], qv[p][0], NN, passes)
        dec = nk[:, 2 * C:] * jnp.concatenate([st[p], st[p]], axis=0)
        state_ref[p] = jnp.where(lo, (dec + upd)[0:A_HEAD], (dec + upd)[A_HEAD:])


def _scan_call(rvk, dkn, lw, ctx_len, passes):
    B, S, _ = rvk.shape
    C = CHUNK
    R = SCAN_ROWS
    nc = S // C
    lc = ctx_len // C
    rev = lambda c: jnp.where(c < lc, lc - 1 - c, nc - 1 - (c - lc))
    f3 = lambda w: pl.BlockSpec((R, C, w), lambda b, c: (b, c, 0))
    b3 = lambda w: pl.BlockSpec((R, C, w), lambda b, c: (b, rev(c), 0))
    f4 = lambda w: pl.BlockSpec((1, R, C, w), lambda b, c: (0, b, c, 0))
    b4 = lambda w: pl.BlockSpec((1, R, C, w), lambda b, c: (1, b, rev(c), 0))
    out = jax.ShapeDtypeStruct((B, S, BW), F32)
    return pl.pallas_call(
        functools.partial(_scan_kernel, passes),
        out_shape=(out, out),
        grid=(B // R, nc),
        in_specs=[f3(3 * BW), f4(2 * BW), f4(BW), b3(3 * BW), b4(2 * BW), b4(BW)],
        out_specs=(f3(BW), b3(BW)),
        scratch_shapes=[pltpu.VMEM((R * A_HEADS, A_HEAD, LANE), F32)],
        compiler_params=_params(("parallel", "arbitrary")),
        name="scan",
    )(rvk, dkn, lw, rvk, dkn, lw)


def _rope(x, tab_ref, shift):
    return (x * tab_ref[0] + pltpu.roll(x, shift, 1) * tab_ref[1]
            + pltpu.roll(x, LANE - shift, 1) * tab_ref[2])


def _bcproj_kernel(zb_ref, zc_ref, tbq_ref, tbk_ref, tcq_ref, tck_ref, qln_ref, kvln_ref,
                   wuq_ref, wuk_ref, wuv_ref, onesb_ref, onesc_ref,
                   qb_out, kb_out, vb_out, qc_out, kc_out, vc_out):
    ones_b = onesb_ref[...]
    ones_c = onesc_ref[...]
    zb = zb_ref[0]
    cq = zb[:, 0:B_Q_LORA]
    ckv = zb[:, B_Q_LORA:B_Q_LORA + B_KV_LORA]
    kr_slab = zb[:, B_Q_LORA + B_KV_LORA:B_COLS_P]
    cqn = cq * lax.rsqrt(jnp.mean(cq * cq, axis=-1, keepdims=True) + NORM_EPS) * qln_ref[...]
    ckvn = ckv * lax.rsqrt(jnp.mean(ckv * ckv, axis=-1, keepdims=True) + NORM_EPS) * kvln_ref[...]
    q_all = _dot(cqn.astype(BF16), wuq_ref[...])
    k_all = _dot(ckvn.astype(BF16), wuk_ref[...])
    vb_out[0] = _dot(ckvn.astype(BF16), wuv_ref[...]).astype(BF16)
    for h in range(B_HEADS):
        sl = slice(h * LANE, (h + 1) * LANE)
        q = q_all[:, sl]
        rs = lax.rsqrt(_mm_exact_rhs(q * q, ones_b) * (1.0 / B_QK) + NORM_EPS)
        qb_out[0, :, sl] = (_rope(q, tbq_ref, B_ROPE // 4) * rs).astype(BF16)
        k = k_all[:, sl] + kr_slab
        rs = lax.rsqrt(_mm_exact_rhs(k * k, ones_b) * (1.0 / B_QK) + NORM_EPS)
        kb_out[0, :, sl] = (_rope(k, tbk_ref, B_ROPE // 4) * rs).astype(BF16)

    zc = zc_ref[0]
    lane = lax.broadcasted_iota(jnp.int32, (1, LANE), 1)
    lo = lane < C_HEAD

    def head_norm(x, tab_ref):
        ms = _mm_exact_rhs(x * x, ones_c) * (1.0 / C_HEAD)
        return _rope(x, tab_ref, C_HEAD // 4) * lax.rsqrt(ms + NORM_EPS)

    for j in range(C_HEADS // 2):
        x = head_norm(zc[:, j * LANE:(j + 1) * LANE], tcq_ref)
        xr = pltpu.roll(x, C_HEAD, 1)
        g = (2 * j) // C_GROUP
        for half in range(2):
            h = 2 * j + half
            src = x if half == g else xr
            keep = lo if g == 0 else jnp.logical_not(lo)
            qc_out[0, :, h * LANE:(h + 1) * LANE] = jnp.where(keep, src, 0.0).astype(BF16)
    kc_out[0] = head_norm(zc[:, BW:BW + C_KV_W], tck_ref).astype(BF16)
    vv = zc[:, BW + C_KV_W:BW + 2 * C_KV_W]
    vr = pltpu.roll(vv, C_HEAD, 1)
    vc_out[0, :, 0:LANE] = jnp.where(lo, vv, vr).astype(BF16)
    vc_out[0, :, LANE:2 * LANE] = jnp.where(lo, vr, vv).astype(BF16)


def _bcproj_call(zb, zc, lp):
    B, S, _ = zb.shape
    tt = TOK_TILE
    full = lambda shape: pl.BlockSpec(shape, lambda t, b: (0,) * len(shape))
    tok = lambda w: pl.BlockSpec((1, tt, w), lambda t, b: (b, t, 0))
    tab = pl.BlockSpec((3, tt, LANE), lambda t, b: (0, t, 0))
    shp = lambda w: jax.ShapeDtypeStruct((B, S, w), BF16)
    return pl.pallas_call(
        _bcproj_kernel,
        out_shape=(shp(B_HEADS * LANE), shp(B_HEADS * LANE), shp(BW),
                   shp(C_HEADS * LANE), shp(C_KV_W), shp(2 * LANE)),
        grid=(S // tt, B),
        in_specs=[tok(B_COLS_P), tok(C_COLS), tab, tab, tab, tab,
                  full((1, B_Q_LORA)), full((1, B_KV_LORA)),
                  full((B_Q_LORA, B_HEADS * LANE)), full((B_KV_LORA, B_HEADS * LANE)),
                  full((B_KV_LORA, BW)), full((LANE, LANE)), full((LANE, LANE))],
        out_specs=(tok(B_HEADS * LANE), tok(B_HEADS * LANE), tok(BW),
                   tok(C_HEADS * LANE), tok(C_KV_W), tok(2 * LANE)),
        compiler_params=_params(("parallel", "arbitrary")),
        name="bcproj",
    )(zb, zc, lp["tab_bq"], lp["tab_bk"], lp["tab_cq"], lp["tab_ck"],
      lp["q_ln"], lp["kv_ln"], lp["w_uq"], lp["w_uk"], lp["w_uv"],
      _block_ones(LANE, LANE), _block_ones(LANE, C_HEAD))


def _attb_kernel(q_ref, k_ref, v_ref, o_ref):
    lane = lax.broadcasted_iota(jnp.int32, (1, LANE), 1)
    lo = lane < B_V
    for j in range(B_HEADS // 2):
        outs = []
        for h in (2 * j, 2 * j + 1):
            sl = slice(h * LANE, (h + 1) * LANE)
            s = _dot(q_ref[:, sl], k_ref[0, :, sl], NT)
            m = jnp.max(s, axis=-1, keepdims=True)
            p = jnp.exp2(s - m)
            den = jnp.sum(p, axis=-1, keepdims=True)
            o = _dot(p.astype(BF16), v_ref[0, :, j * LANE:(j + 1) * LANE])
            outs.append(o / den)
        o_ref[:, j * LANE:(j + 1) * LANE] = jnp.where(lo, outs[0], outs[1])


def _row_block(n_rows, width, seq_len, row0):
    return pl.BlockSpec(
        (pl.Element(n_rows), pl.Element(width)),
        lambda b, t: (pl.multiple_of(b * seq_len + row0 + t * n_rows, TOK_TILE), 0))


def _attb_call(qb, kb, vb, row0, n_rows, tq, n_keys):
    B, S, _ = qb.shape
    width = B_HEADS * LANE
    tiles = n_rows // tq
    return pl.pallas_call(
        _attb_kernel,
        out_shape=jax.ShapeDtypeStruct((B * n_rows, BW), F32),
        grid=(B, tiles),
        in_specs=[
            _row_block(tq, width, S, row0),
            pl.BlockSpec((1, n_keys, width), lambda b, t: (b, 0, 0)),
            pl.BlockSpec((1, n_keys, BW), lambda b, t: (b, 0, 0)),
        ],
        out_specs=pl.BlockSpec((tq, BW), lambda b, t: (b * tiles + t, 0)),
        compiler_params=_params(("parallel", "arbitrary")),
        name="attb",
    )(qb.reshape(B * S, width), kb, vb).reshape(B, n_rows, BW)


def _attc_kernel(tile0, ctx_len, n_lat, sink_ref, q_ref, k_ref, v_ref, *rest):
    valid_refs, o_ref = rest[:-1], rest[-1]
    W = WINDOW
    ctx_tiles = ctx_len // W
    n_blk = n_lat // W
    lane = lax.broadcasted_iota(jnp.int32, (1, LANE), 1)
    lo = lane < C_HEAD
    rows4 = lax.broadcasted_iota(jnp.int32, (C_GROUP * W, 1), 0)

    def attend(rows, k_cat, v_cat, mask):
        for g in range(C_KV_HEADS):
            q_st = jnp.concatenate(
                [q_ref[0, rows, h * LANE:(h + 1) * LANE] for h in range(g * C_GROUP, (g + 1) * C_GROUP)],
                axis=0)
            s = _dot(q_st, k_cat, NT)
            if mask is not None:
                s = jnp.where(jnp.concatenate([mask] * C_GROUP, axis=0) > 0.5, s, NEG)
            sk = jnp.zeros((C_GROUP * W, 1), F32)
            for i in range(C_GROUP):
                sk = jnp.where(rows4 // W == i, sink_ref[g * C_GROUP + i], sk)
            m = jnp.maximum(jnp.max(s, axis=-1, keepdims=True), sk)
            p = jnp.exp2(s - m)
            den = jnp.sum(p, axis=-1, keepdims=True) + jnp.exp2(sk - m)
            o = _dot(p.astype(BF16), v_cat[:, g * LANE:(g + 1) * LANE]) / den
            for jj in range(C_GROUP // 2):
                o_lo = o[(2 * jj) * W:(2 * jj + 1) * W]
                o_hi = o[(2 * jj + 1) * W:(2 * jj + 2) * W]
                col = (g * C_GROUP // 2 + jj) * LANE
                o_ref[0, rows, col:col + LANE] = jnp.where(lo, o_lo, o_hi)

    def latent(i, rows):
        j = (pl.program_id(1) + tile0) * ATTC_BLOCKS + i - ctx_tiles
        bm = jnp.maximum(j - 1, 0)
        bp = jnp.minimum(j + 1, n_blk - 1)

        def blk(ref, b):
            return ref[0, pl.ds(pl.multiple_of(ctx_len + b * W, W), W), :]

        k_cat = jnp.concatenate([k_ref[0, 0:ctx_len, :], blk(k_ref, bm), blk(k_ref, j), blk(k_ref, bp)], axis=0)
        v_cat = jnp.concatenate([v_ref[0, 0:ctx_len, :], blk(v_ref, bm), blk(v_ref, j), blk(v_ref, bp)], axis=0)
        attend(rows, k_cat, v_cat, valid_refs[i][0])

    def context(rows):
        attend(rows, k_ref[0, 0:ctx_len, :], v_ref[0, 0:ctx_len, :], None)

    t = pl.program_id(1) + tile0
    ctx_steps = ctx_tiles // ATTC_BLOCKS
    blocks = [(i, slice(i * W, (i + 1) * W)) for i in range(ATTC_BLOCKS)]

    def all_context():
        for _, rows in blocks:
            context(rows)

    def all_latent():
        for i, rows in blocks:
            latent(i, rows)

    if tile0 < ctx_steps:
        pl.when(t < ctx_steps)(all_context)
        pl.when(t >= ctx_steps)(all_latent)
    else:
        all_latent()


def _attc_call(qc, kc, vc, sink, ctx_len, with_ctx):
    B, S, _ = qc.shape
    W = WINDOW
    nb = ATTC_BLOCKS
    ctx_steps = ctx_len // (nb * W)
    tile0 = 0 if with_ctx else ctx_steps
    n_steps = S // (nb * W) - tile0
    n_blk = (S - ctx_len) // W
    valid = _band_valid(ctx_len)

    def variant(t, i):
        j = (t + tile0) * nb + i - ctx_len // W
        return jnp.where(j == 0, 1, 0) + jnp.where(j == n_blk - 1, 2, 0)

    return pl.pallas_call(
        functools.partial(_attc_kernel, tile0, ctx_len, S - ctx_len),
        out_shape=jax.ShapeDtypeStruct((B, n_steps * nb * W, BW), F32),
        grid=(B, n_steps),
        in_specs=[
            pl.BlockSpec(memory_space=pltpu.SMEM),
            pl.BlockSpec((1, nb * W, C_HEADS * LANE), lambda b, t: (b, t + tile0, 0)),
            pl.BlockSpec((1, S, C_KV_W), lambda b, t: (b, 0, 0)),
            pl.BlockSpec((1, S, 2 * LANE), lambda b, t: (b, 0, 0)),
        ] + [pl.BlockSpec((1, W, ctx_len + 3 * W), lambda b, t, i=i: (variant(t, i), 0, 0)) for i in range(nb)],
        out_specs=pl.BlockSpec((1, nb * W, BW), lambda b, t: (b, t, 0)),
        compiler_params=_params(("parallel", "arbitrary")),
        name="attc",
    )(sink, qc, kc, vc, *([valid] * nb))


def _band_valid(ctx_len):
    W = WINDOW
    qi = np.arange(W)[:, None]
    cb = np.arange(ctx_len + 3 * W)[None, :] - ctx_len
    base = (cb < 0) | (np.abs(cb - W - qi) <= WINDOW)
    out = []
    for variant in range(4):
        first, last = bool(variant & 1), bool(variant & 2)
        ok = base.copy()
        if first:
            ok &= ~((cb >= 0) & (cb < W))
        if last:
            ok &= ~(cb >= 2 * W)
        out.append(ok)
    return jnp.asarray(np.stack(out), dtype=F32)


def _merge_kernel(tile0, ctx_tiles, y0_ref, y1_ref, bonus_ref, ybc_ref, ybl_ref, yc_ref, x_ref,
                  shift_ref, scale_ref, gate_ref,
                  g_ref, wg_ref, gng_ref, gnb_ref, ones_ref, wbo_ref, wout_ref, o_ref):
    x = x_ref[0]
    hb = _modulated_norm(x, g_ref[...], shift_ref[0], scale_ref[0]).astype(BF16)
    ones = ones_ref[...]
    y = y0_ref[0] + y1_ref[0]
    mu = _head_sums(y, ones) * (1.0 / A_HEAD)
    dlt = y - mu
    var = _head_sums(dlt * dlt, ones) * (1.0 / A_HEAD)
    ya = dlt * lax.rsqrt(var + A_GN_EPS) * gng_ref[...] + gnb_ref[...] + bonus_ref[0]
    m = None
    zg0 = N_BRANCH * BW
    yb = jnp.where(pl.program_id(1) + tile0 < ctx_tiles, ybc_ref[0], ybl_ref[0])
    for n, y_n in enumerate((ya, yb, yc_ref[0])):
        g = _dot(hb, wg_ref[:, n * BW:(n + 1) * BW])
        u = (y_n * _silu(g)).astype(BF16)
        zg = _dot(hb, wg_ref[:, zg0 + n * D_MODEL:zg0 + (n + 1) * D_MODEL])
        term = _sigmoid(zg) * _dot(u, wbo_ref[n])
        m = term if m is None else m + term
    o_ref[0] = x + gate_ref[0] * _dot(m.astype(BF16), wout_ref[...])


def _merge_call(y0, y1, bonus, yb_ctx, yb_lat, yc, xall, modl, norm_g, lp, ones_a, ctx_len, with_ctx):
    B, S, _ = xall.shape
    tt = TOK_TILE
    ctx_tiles = ctx_len // tt
    tile0 = 0 if with_ctx else ctx_tiles
    n_tiles = S // tt - tile0
    row = _mod_row(ctx_tiles, B)
    mod3 = modl.reshape(modl.shape[0], 1, 3 * D_MODEL)
    mod = lambda col: pl.BlockSpec((1, 1, D_MODEL), lambda b, t: (row(b, t + tile0), 0, col))
    once = lambda shape: pl.BlockSpec(shape, lambda b, t: (0,) * len(shape), pipeline_mode=pl.Buffered(1))
    tok = lambda w: pl.BlockSpec((1, tt, w), lambda b, t: (b, t + tile0, 0))
    att = lambda w: pl.BlockSpec((1, tt, w), lambda b, t: (b, t, 0))
    n_gate = sum(w for _, w in IN_SEGS_GATE)
    ybc = pl.BlockSpec((1, tt, BW), lambda b, t: (b, jnp.minimum(t + tile0, ctx_tiles - 1), 0))
    ybl = pl.BlockSpec((1, tt, BW), lambda b, t: (b, jnp.maximum(t + tile0 - ctx_tiles, 0), 0))
    return pl.pallas_call(
        functools.partial(_merge_kernel, tile0, ctx_tiles),
        out_shape=jax.ShapeDtypeStruct((B, n_tiles * tt, D_MODEL), F32),
        grid=(B, n_tiles),
        in_specs=[tok(BW), tok(BW), tok(BW), ybc, ybl, att(BW), tok(D_MODEL),
                  mod(0), mod(1), mod(2),
                  once((1, D_MODEL)), once((D_MODEL, n_gate)),
                  once((1, BW)), once((1, BW)), once((MXU_W, MXU_W)),
                  once((N_BRANCH, BW, D_MODEL)), once((D_MODEL, D_MODEL))],
        out_specs=pl.BlockSpec((1, tt, D_MODEL), lambda b, t: (b, t, 0)),
        compiler_params=_params(("parallel", "arbitrary")),
        name="merge",
    )(y0, y1, bonus, yb_ctx, yb_lat, yc, xall, mod3, mod3, mod3, norm_g.reshape(1, D_MODEL), lp["w_gate"],
      lp["gn_g"], lp["gn_b"], ones_a, lp["w_bo"], lp["w_out"])


def _rope_tables(n_ctx, n_lat, rot_dim, head_w, lane0):
    rows = n_lat // GRID_W
    row = jnp.repeat(jnp.arange(rows), GRID_W).astype(F32)
    colp = jnp.tile(jnp.arange(GRID_W), rows).astype(F32)
    q = rot_dim // 4
    inv = ROPE_THETA ** (-(2.0 * jnp.arange(q, dtype=F32)) / (rot_dim // 2))
    ang = jnp.concatenate([row[:, None] * inv, colp[:, None] * inv], axis=-1)
    cos, sin = jnp.cos(ang), jnp.sin(ang)
    zero = jnp.zeros_like(sin[:, :q])
    cos_g = jnp.concatenate([cos[:, :q], cos[:, :q], cos[:, q:], cos[:, q:]], axis=-1)
    sp_g = jnp.concatenate([zero, sin[:, :q], zero, sin[:, q:]], axis=-1)
    sm_g = jnp.concatenate([-sin[:, :q], zero, -sin[:, q:], zero], axis=-1)

    def place(t, fill):
        pad_l = jnp.full((n_lat, lane0), fill, F32)
        pad_r = jnp.full((n_lat, head_w - lane0 - rot_dim), fill, F32)
        grp = jnp.concatenate([pad_l, t, pad_r], axis=-1)
        lat = jnp.tile(grp, (1, LANE // head_w))
        return jnp.concatenate([jnp.full((n_ctx, LANE), fill, F32), lat], axis=0)

    return jnp.stack([place(cos_g, 1.0), place(sp_g, 0.0), place(sm_g, 0.0)])


def _block_ones(n, blk):
    i = np.arange(n) // blk
    return jnp.asarray(i[:, None] == i[None, :], dtype=BF16)


def _pad_lanes(g, width):
    return jnp.pad(g, (0, width - g.shape[0]))


def _fold_gain(tab, gain, shift, scale):
    g = jnp.stack([gain, jnp.roll(gain, shift), jnp.roll(gain, -shift)]) * scale
    return tab * g[:, None, :]


def _layer_params(i, tab_b, tab_c, w_in, a_mu_prev, a_mu_next, a_w0, a_w_up, a_a0, a_a_up, a_k_k, a_k_a, a_r_k,
                  a_gn_g, a_gn_b, b_q_ln, b_kv_ln, b_w_uq, b_w_ukv, b_qn_g, b_kn_g,
                  c_qn_g, c_kn_g, c_sink, w_branch_out, w_out):
    w = w_in[i].astype(BF16)
    c0 = 0
    segs = {}
    for name, width in (("za", A_COLS), ("ga", BW), ("zb", B_COLS), ("gb", BW),
                        ("zc", C_COLS), ("gc", BW), ("zg", N_BRANCH * D_MODEL)):
        segs[name] = w[:, c0:c0 + width]
        c0 += width
    zb = segs["zb"]
    zpad = lambda n: jnp.zeros((D_MODEL, n), w.dtype)
    segs["zb"] = jnp.concatenate(
        [zb[:, :B_Q_LORA + B_KV_LORA], zpad(B_NOPE), zb[:, B_Q_LORA + B_KV_LORA:], zpad(LANE - B_NOPE - B_ROPE)],
        axis=-1)
    w_mix = jnp.concatenate([segs[n] for n, _ in IN_SEGS_MIX], axis=-1).astype(BF16)
    w_gate = jnp.concatenate([segs[n] for n, _ in IN_SEGS_GATE], axis=-1).astype(BF16)

    uq = b_w_uq[i].reshape(B_Q_LORA, B_HEADS, B_QK)
    uq = jnp.pad(uq, ((0, 0), (0, 0), (0, LANE - B_QK))).reshape(B_Q_LORA, B_HEADS * LANE)
    ukv = b_w_ukv[i].reshape(B_KV_LORA, B_HEADS, B_NOPE + B_V)
    uk = jnp.pad(ukv[:, :, :B_NOPE], ((0, 0), (0, 0), (0, LANE - B_NOPE))).reshape(B_KV_LORA, B_HEADS * LANE)
    uv = ukv[:, :, B_NOPE:].reshape(B_KV_LORA, BW)
    row = lambda t: t.reshape(1, -1)
    return dict(
        w_mix=w_mix, w_gate=w_gate,
        mu_prev=row(a_mu_prev[i]), mu_next=row(a_mu_next[i]),
        w0=a_w0[i], w_up=a_w_up[i], a0=a_a0[i], a_up=a_a_up[i],
        k_k=row(a_k_k[i]), k_a=row(a_k_a[i]), r_k=row(a_r_k[i]),
        gn_g=row(a_gn_g[i]), gn_b=row(a_gn_b[i]),
        q_ln=row(b_q_ln[i]), kv_ln=row(b_kv_ln[i]),
        w_uq=uq.astype(BF16), w_uk=uk.astype(BF16), w_uv=uv.astype(BF16),
        tab_bq=_fold_gain(tab_b, _pad_lanes(b_qn_g[i], LANE), B_ROPE // 4, B_QK ** -0.5 * LOG2E),
        tab_bk=_fold_gain(tab_b, _pad_lanes(b_kn_g[i], LANE), B_ROPE // 4, 1.0),
        tab_cq=_fold_gain(tab_c, jnp.tile(c_qn_g[i], 2), C_HEAD // 4, C_HEAD ** -0.5 * LOG2E),
        tab_ck=_fold_gain(tab_c, jnp.tile(c_kn_g[i], 2), C_HEAD // 4, 1.0),
        sink=c_sink[i] * LOG2E,
        w_bo=w_branch_out[i].astype(BF16), w_out=w_out[i].astype(BF16),
    )


SCAN_PASSES = 1


def kernel(x, c, ctx, c_ctx, ada_w, ada_b, norm_g, w_in, a_mu_prev, a_mu_next, a_w0, a_w_up, a_a0, a_a_up, a_k_k, a_k_a, a_r_k, a_gn_g, a_gn_b, b_q_ln, b_kv_ln, b_w_uq, b_w_ukv, b_qn_g, b_kn_g, c_qn_g, c_kn_g, c_sink, w_branch_out, w_out):
    B, T, D = x.shape
    L = ctx.shape[1]
    depth = ada_w.shape[0]
    assert D == D_MODEL and L % TOK_TILE == 0 and T % TOK_TILE == 0 and T % GRID_W == 0
    assert T % ATTB_TQ == 0 and B % SCAN_ROWS == 0 and L % (ATTC_BLOCKS * WINDOW) == 0

    mod_rows = -(-(B + 1) // 8) * 8
    cs = jnp.concatenate([c, c_ctx[None, :], jnp.zeros((mod_rows - B - 1, D), c.dtype)], axis=0)
    mod = _mod_call(cs, ada_w, ada_b)

    tab_b = _rope_tables(L, T, B_ROPE, LANE, B_NOPE)
    tab_c = _rope_tables(L, T, C_HEAD, C_HEAD, 0)
    ones_a = _block_ones(MXU_W, A_HEAD)

    xall = jnp.concatenate([ctx, x], axis=1)
    for i in range(depth):
        with_ctx = i < depth - 1
        lp = _layer_params(i, tab_b, tab_c, w_in, a_mu_prev, a_mu_next, a_w0, a_w_up, a_a0, a_a_up, a_k_k, a_k_a,
                           a_r_k, a_gn_g, a_gn_b, b_q_ln, b_kv_ln, b_w_uq, b_w_ukv, b_qn_g, b_kn_g,
                           c_qn_g, c_kn_g, c_sink, w_branch_out, w_out)
        zb, zc, rvk, dkn, lw, bonus = _inprep_call(xall, mod[i], norm_g[i], lp, ones_a, L)
        y0, y1 = _scan_call(rvk, dkn, lw, L, SCAN_PASSES)
        qb, kb, vb, qc, kc, vc = _bcproj_call(zb, zc, lp)
        yb_lat = _attb_call(qb, kb, vb, L, T, ATTB_TQ, L + T)
        yb_ctx = _attb_call(qb, kb, vb, 0, L, TOK_TILE, L) if with_ctx else yb_lat
        yc = _attc_call(qc, kc, vc, lp["sink"], L, with_ctx)
        xall = _merge_call(y0, y1, bonus, yb_ctx, yb_lat, yc, xall, mod[i], norm_g[i], lp, ones_a, L,
                           with_ctx)
    return xall
```

```python
import functools

import numpy as np
import jax
import jax.numpy as jnp
from jax import lax
from jax.experimental import pallas as pl
from jax.experimental.pallas import tpu as pltpu

F32 = jnp.float32
BF16 = jnp.bfloat16

D_MODEL = 1024
GRID_W = 64
ROPE_THETA = 10000.0
NORM_EPS = 1e-6
NEG = -1e30
BW = 512
N_BRANCH = 3

A_HEAD = 64
A_HEADS = BW // A_HEAD
A_LORA = 64
A_GN_EPS = 64e-5
A_COLS = 3 * BW + 4 * A_LORA

B_HEADS = 8
B_NOPE = 64
B_ROPE = 32
B_QK = B_NOPE + B_ROPE
B_V = BW // B_HEADS
B_Q_LORA = 256
B_KV_LORA = 128
B_COLS = B_Q_LORA + B_KV_LORA + B_ROPE
B_COLS_P = 512

C_HEAD = 64
C_HEADS = BW // C_HEAD
C_KV_HEADS = 2
C_GROUP = C_HEADS // C_KV_HEADS
C_KV_W = C_KV_HEADS * C_HEAD
C_COLS = BW + 2 * C_KV_W
WINDOW = 128

LANE = 128
MXU_W = 256
CHUNK = 64
SCAN_ROWS = 8
TOK_TILE = 256
VMEM_LIMIT = 56 * 1024 * 1024

ATTB_TQ = 512
ATTC_BLOCKS = 2
LOG2E = 1.4426950408889634

IN_SEGS_MIX = (("za", A_COLS), ("zb", B_COLS_P), ("zc", C_COLS))
IN_SEGS_GATE = (("ga", BW), ("gb", BW), ("gc", BW), ("zg", N_BRANCH * D_MODEL))

NN = (((1,), (0,)), ((), ()))
NT = (((1,), (1,)), ((), ()))


def _dot(a, b, dims=NN):
    return lax.dot_general(a, b, dims, preferred_element_type=F32)


def _split(x):
    hi = x.astype(BF16)
    lo = (x - hi.astype(F32)).astype(BF16)
    return hi, lo


def _mm(a, b, dims=NN, passes=1):
    if passes == 1:
        return _dot(a.astype(BF16), b.astype(BF16), dims)
    ah, al = _split(a)
    bh, bl = _split(b)
    return _dot(ah, bh, dims) + (_dot(ah, bl, dims) + _dot(al, bh, dims))


def _mm_exact_rhs(a, b_bf16, parts=2):
    acc = None
    rem = a
    for _ in range(parts):
        p = rem.astype(BF16)
        t = _dot(p, b_bf16)
        acc = t if acc is None else acc + t
        rem = rem - p.astype(F32)
    return acc


def _head_sums(a, ones_blk):
    w = ones_blk.shape[0]
    return jnp.concatenate(
        [_mm_exact_rhs(a[:, c:c + w], ones_blk) for c in range(0, a.shape[1], w)], axis=1)


def _mm_exact_lhs(a_bf16, b, parts=3):
    acc = None
    rem = b
    for _ in range(parts):
        p = rem.astype(BF16)
        t = _dot(a_bf16, p)
        acc = t if acc is None else acc + t
        rem = rem - p.astype(F32)
    return acc


def _sigmoid(x):
    return 1.0 / (1.0 + jnp.exp(-x))


def _silu(x):
    return x * _sigmoid(x)


def _modulated_norm(x, g, shift, scale):
    ms = jnp.mean(x * x, axis=-1, keepdims=True)
    return (x * lax.rsqrt(ms + NORM_EPS) * g) * (1.0 + scale) + shift


def _params(sem):
    return pltpu.CompilerParams(dimension_semantics=sem, vmem_limit_bytes=VMEM_LIMIT)


def _mod_kernel(c_ref, w_ref, b_ref, o_ref):
    s = _silu(c_ref[...])
    o_ref[0] = _mm(s, w_ref[0], passes=3) + b_ref[0]


def _mod_call(cs, ada_w, ada_b):
    n_layers = ada_w.shape[0]
    rows = cs.shape[0]
    return pl.pallas_call(
        _mod_kernel,
        out_shape=jax.ShapeDtypeStruct((n_layers, rows, 3 * D_MODEL), F32),
        grid=(n_layers, 3),
        in_specs=[
            pl.BlockSpec((rows, D_MODEL), lambda l, n: (0, 0)),
            pl.BlockSpec((1, D_MODEL, D_MODEL), lambda l, n: (l, 0, n)),
            pl.BlockSpec((1, 1, D_MODEL), lambda l, n: (l, 0, n)),
        ],
        out_specs=pl.BlockSpec((1, rows, D_MODEL), lambda l, n: (l, 0, n)),
        compiler_params=_params(("arbitrary", "arbitrary")),
        name="mod",
    )(cs, ada_w, ada_b.reshape(n_layers, 1, 3 * D_MODEL))


def _mod_row(ctx_tiles, n_batch_rows):
    return lambda b, t: jnp.where(t < ctx_tiles, n_batch_rows, b)


def _inprep_kernel(ctx_tiles, n_tiles,
                   x_ref, xp_ref, xn_ref, shift_ref, scale_ref, g_ref, w_ref,
                   mup_ref, mun_ref, w0_ref, wup_ref, a0_ref, aup_ref, kk_ref, ka_ref, rk_ref, ones_ref,
                   zb_out, zc_out, rvk_out, dkn_out, lw_out, bonus_out, hb_ref, za_ref):
    t = pl.program_id(1)
    tt = TOK_TILE
    first = jnp.logical_or(t == 0, t == ctx_tiles)
    last = jnp.logical_or(t == ctx_tiles - 1, t == n_tiles - 1)
    g, shift, scale = g_ref[...], shift_ref[0], scale_ref[0]
    hb_ref[0:tt, :] = _modulated_norm(x_ref[0], g, shift, scale).astype(BF16)
    hb_ref[tt:tt + 8, :] = _modulated_norm(xp_ref[0], g, shift, scale).astype(BF16)
    hb_ref[tt + 8:tt + 16, :] = _modulated_norm(xn_ref[0], g, shift, scale).astype(BF16)
    hb = hb_ref[...]
    for c0 in range(0, A_COLS, MXU_W):
        za_ref[:, c0:c0 + MXU_W] = _dot(hb, w_ref[:, c0:c0 + MXU_W])
    hb_t = hb[0:tt]
    zb_out[0] = _dot(hb_t, w_ref[:, A_COLS:A_COLS + B_COLS_P])
    for c0 in range(0, C_COLS, MXU_W):
        col = A_COLS + B_COLS_P + c0
        zc_out[0, :, c0:c0 + MXU_W] = _dot(hb_t, w_ref[:, col:col + MXU_W])

    z = za_ref[0:tt, :]
    row = lax.broadcasted_iota(jnp.int32, (tt, 1), 0)
    halo_prev = jnp.where(first, 0.0, za_ref[tt + 7:tt + 8, :])
    halo_next = jnp.where(last, 0.0, za_ref[tt + 8:tt + 9, :])
    prev = jnp.where(row == 0, halo_prev, pltpu.roll(z, 1, 0))
    nxt = jnp.where(row == tt - 1, halo_next, pltpu.roll(z, tt - 1, 0))
    zsh = z + mup_ref[...] * (prev - z) + mun_ref[...] * (nxt - z)

    r = zsh[:, 0:BW]
    k = zsh[:, BW:2 * BW]
    v = zsh[:, 2 * BW:3 * BW]
    ones = ones_ref[...]
    kk = k * kk_ref[...]
    ss = _head_sums(kk * kk, ones)
    kk = kk * lax.rsqrt(jnp.maximum(ss, 1e-24))
    rvk_out[0, :, 0:BW] = r
    rvk_out[0, :, BW:2 * BW] = v
    rvk_out[0, :, 2 * BW:3 * BW] = kk
    kd_sum = None
    for d in range(2):
        wd = zsh[:, 3 * BW + d * A_LORA:3 * BW + (d + 1) * A_LORA]
        ad = zsh[:, 3 * BW + 2 * A_LORA + d * A_LORA:3 * BW + 2 * A_LORA + (d + 1) * A_LORA]
        u = -(w0_ref[d:d + 1, :] + _mm(jnp.tanh(wd), wup_ref[d], passes=3))
        softplus = jnp.maximum(u, 0.0) + jnp.log(1.0 + jnp.exp(-jnp.abs(u)))
        w_log = -softplus - 0.5
        lw = -jnp.exp(w_log)
        a = _sigmoid(a0_ref[d:d + 1, :] + _mm(ad, aup_ref[d], passes=3))
        kd = k * (1.0 + (a - 1.0) * ka_ref[...])
        lw_out[d, 0] = lw
        dkn_out[d, 0, :, 0:BW] = kd
        dkn_out[d, 0, :, BW:2 * BW] = -(kk * a)
        kd_sum = kd if kd_sum is None else kd_sum + kd
    bonus_out[0] = _head_sums(r * kd_sum * rk_ref[...], ones) * v


def _inprep_call(xall, modl, norm_g, lp, ones_a, ctx_len):
    B, S, _ = xall.shape
    tt = TOK_TILE
    n_tiles = S // tt
    hb = tt // 8
    n8 = S // 8
    row = _mod_row(ctx_len // tt, B)
    mod3 = modl.reshape(modl.shape[0], 1, 3 * D_MODEL)
    mod = lambda col: pl.BlockSpec((1, 1, D_MODEL), lambda b, t: (row(b, t), 0, col))
    once = lambda shape: pl.BlockSpec(shape, lambda b, t: (0,) * len(shape), pipeline_mode=pl.Buffered(1))
    s3 = lambda w: pl.BlockSpec((1, tt, w), lambda b, t: (b, t, 0))
    s4 = lambda w: pl.BlockSpec((2, 1, tt, w), lambda b, t: (0, b, t, 0))
    o3 = lambda w, dt: jax.ShapeDtypeStruct((B, S, w), dt)
    o4 = lambda w, dt: jax.ShapeDtypeStruct((2, B, S, w), dt)
    n_mix = sum(w for _, w in IN_SEGS_MIX)
    return pl.pallas_call(
        functools.partial(_inprep_kernel, ctx_len // tt, n_tiles),
        out_shape=(o3(B_COLS_P, F32), o3(C_COLS, F32), o3(3 * BW, F32), o4(2 * BW, F32),
                   o4(BW, F32), o3(BW, F32)),
        grid=(B, n_tiles),
        in_specs=[
            s3(D_MODEL),
            pl.BlockSpec((1, 8, D_MODEL), lambda b, t: (b, jnp.maximum(t * hb - 1, 0), 0)),
            pl.BlockSpec((1, 8, D_MODEL), lambda b, t: (b, jnp.minimum((t + 1) * hb, n8 - 1), 0)),
            mod(0), mod(1), once((1, D_MODEL)), once((D_MODEL, n_mix)),
            once((1, A_COLS)), once((1, A_COLS)),
            once((2, BW)), once((2, A_LORA, BW)), once((2, BW)), once((2, A_LORA, BW)),
            once((1, BW)), once((1, BW)), once((1, BW)), once((MXU_W, MXU_W)),
        ],
        out_specs=(s3(B_COLS_P), s3(C_COLS), s3(3 * BW), s4(2 * BW), s4(BW), s3(BW)),
        scratch_shapes=[pltpu.VMEM((tt + 16, D_MODEL), BF16), pltpu.VMEM((tt + 16, A_COLS), F32)],
        compiler_params=_params(("parallel", "arbitrary")),
        name="inprep",
    )(xall, xall, xall, mod3, mod3, norm_g.reshape(1, D_MODEL), lp["w_mix"],
      lp["mu_prev"], lp["mu_next"], lp["w0"], lp["w_up"], lp["a0"], lp["a_up"],
      lp["k_k"], lp["k_a"], lp["r_k"], ones_a)


def _scan_kernel(f_rvk, f_dkn, f_lw, b_rvk, b_dkn, b_lw, yf_ref, yb_ref, state_ref):
    C = CHUNK

    @pl.when(pl.program_id(1) == 0)
    def _():
        state_ref[...] = jnp.zeros_like(state_ref)

    lane = lax.broadcasted_iota(jnp.int32, (1, LANE), 1)
    lo = lane < A_HEAD
    row2 = lax.broadcasted_iota(jnp.int32, (2 * C, 2 * C), 0)
    col2 = lax.broadcasted_iota(jnp.int32, (2 * C, 2 * C), 1)
    rows = lax.broadcasted_iota(jnp.int32, (C, C), 0)
    cols = lax.broadcasted_iota(jnp.int32, (C, C), 1)
    n_pairs = A_HEADS // 2

    n_rows = f_rvk.shape[0]
    dirs = []
    for row, d in [(row, d) for row in range(n_rows) for d in range(2)]:
        rvk_ref, dkn_ref, lw_ref = (f_rvk, f_dkn, f_lw) if d == 0 else (b_rvk, b_dkn, b_lw)
        order = (rows - cols) if d == 0 else (cols - rows)
        lw = lw_ref[0, row]
        kd = dkn_ref[0, row, :, 0:BW]
        nb = dkn_ref[0, row, :, BW:2 * BW]
        cum = _mm_exact_lhs((order >= 0).astype(BF16), lw)
        tot = cum[C - 1:C, :] if d == 0 else cum[0:1, :]
        p_inv = jnp.exp(-cum)
        p_tot = jnp.exp(tot - cum)
        kt = rvk_ref[row, :, 2 * BW:3 * BW] * jnp.exp(cum - lw)
        order2 = (row2 % C - col2 % C) if d == 0 else (col2 % C - row2 % C)
        keep = order2 >= jnp.where(row2 < C, 0, 1)
        e_rows = jnp.broadcast_to(jnp.exp(tot), (LANE, BW))
        dirs.append(dict(
            keep=keep, kt=kt, v=rvk_ref[row, :, BW:2 * BW], fwd=(d == 0),
            y_ref=yf_ref if d == 0 else yb_ref, row=row,
            lhs=jnp.concatenate([rvk_ref[row, :, 0:BW] * jnp.exp(cum), kt], axis=0),
            rhs_e=jnp.concatenate([nb * p_inv, kd * p_inv], axis=0),
            rhs_o=jnp.concatenate([kd * p_inv, nb * p_inv], axis=0),
            nkt=jnp.concatenate([nb * p_tot, kd * p_tot, e_rows], axis=0).T))

    units = [(d, h) for d in range(len(dirs)) for h in range(A_HEADS)]
    slab = lambda h: slice((h // 2) * LANE, (h // 2 + 1) * LANE)
    own = lambda h: lo if h % 2 == 0 else jnp.logical_not(lo)

    lhs_m, top, bot = [], [], []
    for d, h in units:
        x = dirs[d]
        lm = jnp.where(own(h), x["lhs"][:, slab(h)], 0.0)
        rhs = x["rhs_o" if h % 2 else "rhs_e"][:, slab(h)]
        qk = jnp.where(x["keep"], _mm(lm, rhs, NT), 0.0)
        lhs_m.append(lm)
        top.append(qk[0:C])
        bot.append(qk[C:2 * C])
    pairs = [(d, j) for d in range(len(dirs)) for j in range(n_pairs)]
    r_i = lax.broadcasted_iota(jnp.int32, (C, 2 * C), 0)
    c_i = lax.broadcasted_iota(jnp.int32, (C, 2 * C), 1) % C
    eye2 = (r_i == c_i).astype(F32)

    def couple(b, fwd):
        late, early = (r_i, c_i) if fwd else (c_i, r_i)
        return (r_i // (2 * b) == c_i // (2 * b)) & ((late // b) % 2 == 1) & ((early // b) % 2 == 0)

    bd = lambda m: jnp.concatenate([jnp.where(lo, m, 0.0), jnp.where(lo, 0.0, m)], axis=0)
    n_pair = [jnp.where(lo, bot[2 * p], bot[2 * p + 1]) for p in range(len(pairs))]
    fwd_of = [dirs[d]["fwd"] for d, _ in pairs]
    t_inv = [eye2 + jnp.where(couple(1, f), n, 0.0) for n, f in zip(n_pair, fwd_of)]
    b = 2
    while b < C:
        et = [_mm(jnp.where(couple(b, f), n, 0.0), bd(t), NN)
              for n, t, f in zip(n_pair, t_inv, fwd_of)]
        t_inv = [t + _mm(t, bd(m), NN) for t, m in zip(t_inv, et)]
        b *= 2
    wta = []
    for i, (d, h) in enumerate(units):
        t = t_inv[i // 2]
        kt_s = dirs[d]["kt"][:, slab(h)]
        if h % 2 == 0:
            wta.append(_mm(t[:, 0:C], jnp.where(lo, kt_s, bot[i]), NN))
        else:
            rhs = jnp.where(lo, bot[i], kt_s)
            wta.append(_mm(jnp.where(lo, 0.0, t), jnp.concatenate([jnp.zeros_like(rhs), rhs], axis=0),
                           NN))
    st = [state_ref[i] for i in range(len(pairs))]
    vp = [dirs[d]["v"][:, j * LANE:(j + 1) * LANE] for d, j in pairs]
    sv = [(jnp.concatenate([s, v], axis=0), jnp.concatenate([v, s], axis=0)) for s, v in zip(st, vp)]
    qt_u = [_mm(m, sv[i // 2][i % 2], NN) for i, m in enumerate(wta)]
    qt = [jnp.where(lo, qt_u[2 * i], qt_u[2 * i + 1]) for i in range(len(pairs))]
    qv = [(jnp.concatenate([q, v], axis=0), jnp.concatenate([v, q], axis=0)) for q, v in zip(qt, vp)]
    y_u = []
    for i, (d, h) in enumerate(units):
        p = i // 2
        y_l = jnp.concatenate([top[i], lhs_m[i][0:C]], axis=1)
        y_r = jnp.concatenate([qv[p][h % 2], st[p], st[p]], axis=0)
        y_u.append(_mm(y_l, y_r, NN))
    for p, (d, j) in enumerate(pairs):
        sl = slice(j * LANE, (j + 1) * LANE)
        dirs[d]["y_ref"][dirs[d]["row"], :, sl] = jnp.where(lo, y_u[2 * p], y_u[2 * p + 1])
        nk = dirs[d]["nkt"][sl]
        upd = _mm(nk[:, 0:2 * C], qv[p][0], NN)
        dec = nk[:, 2 * C:] * jnp.concatenate([st[p], st[p]], axis=0)
        state_ref[p] = jnp.where(lo, (dec + upd)[0:A_HEAD], (dec + upd)[A_HEAD:])


def _scan_call(rvk, dkn, lw, ctx_len):
    B, S, _ = rvk.shape
    C = CHUNK
    R = SCAN_ROWS
    nc = S // C
    lc = ctx_len // C
    rev = lambda c: jnp.where(c < lc, lc - 1 - c, nc - 1 - (c - lc))
    f3 = lambda w: pl.BlockSpec((R, C, w), lambda b, c: (b, c, 0))
    b3 = lambda w: pl.BlockSpec((R, C, w), lambda b, c: (b, rev(c), 0))
    f4 = lambda w: pl.BlockSpec((1, R, C, w), lambda b, c: (0, b, c, 0))
    b4 = lambda w: pl.BlockSpec((1, R, C, w), lambda b, c: (1, b, rev(c), 0))
    out = jax.ShapeDtypeStruct((B, S, BW), F32)
    return pl.pallas_call(
        _scan_kernel,
        out_shape=(out, out),
        grid=(B // R, nc),
        in_specs=[f3(3 * BW), f4(2 * BW), f4(BW), b3(3 * BW), b4(2 * BW), b4(BW)],
        out_specs=(f3(BW), b3(BW)),
        scratch_shapes=[pltpu.VMEM((R * A_HEADS, A_HEAD, LANE), F32)],
        compiler_params=_params(("parallel", "arbitrary")),
        name="scan",
    )(rvk, dkn, lw, rvk, dkn, lw)


def _rope(x, tab_ref, shift):
    return (x * tab_ref[0] + pltpu.roll(x, shift, 1) * tab_ref[1]
            + pltpu.roll(x, LANE - shift, 1) * tab_ref[2])


def _bcproj_kernel(zb_ref, zc_ref, tbq_ref, tbk_ref, tcq_ref, tck_ref, qln_ref, kvln_ref,
                   wuq_ref, wuk_ref, wuv_ref, onesb_ref, onesc_ref,
                   qb_out, kb_out, vb_out, qc_out, kc_out, vc_out):
    ones_b = onesb_ref[...]
    ones_c = onesc_ref[...]
    zb = zb_ref[0]
    cq = zb[:, 0:B_Q_LORA]
    ckv = zb[:, B_Q_LORA:B_Q_LORA + B_KV_LORA]
    kr_slab = zb[:, B_Q_LORA + B_KV_LORA:B_COLS_P]
    cqn = cq * lax.rsqrt(jnp.mean(cq * cq, axis=-1, keepdims=True) + NORM_EPS) * qln_ref[...]
    ckvn = ckv * lax.rsqrt(jnp.mean(ckv * ckv, axis=-1, keepdims=True) + NORM_EPS) * kvln_ref[...]
    q_all = _dot(cqn.astype(BF16), wuq_ref[...])
    k_all = _dot(ckvn.astype(BF16), wuk_ref[...])
    vb_out[0] = _dot(ckvn.astype(BF16), wuv_ref[...]).astype(BF16)
    for h in range(B_HEADS):
        sl = slice(h * LANE, (h + 1) * LANE)
        q = q_all[:, sl]
        rs = lax.rsqrt(_mm_exact_rhs(q * q, ones_b) * (1.0 / B_QK) + NORM_EPS)
        qb_out[0, :, sl] = (_rope(q, tbq_ref, B_ROPE // 4) * rs).astype(BF16)
        k = k_all[:, sl] + kr_slab
        rs = lax.rsqrt(_mm_exact_rhs(k * k, ones_b) * (1.0 / B_QK) + NORM_EPS)
        kb_out[0, :, sl] = (_rope(k, tbk_ref, B_ROPE // 4) * rs).astype(BF16)

    zc = zc_ref[0]
    lane = lax.broadcasted_iota(jnp.int32, (1, LANE), 1)
    lo = lane < C_HEAD

    def head_norm(x, tab_ref):
        ms = _mm_exact_rhs(x * x, ones_c) * (1.0 / C_HEAD)
        return _rope(x, tab_ref, C_HEAD // 4) * lax.rsqrt(ms + NORM_EPS)

    for j in range(C_HEADS // 2):
        x = head_norm(zc[:, j * LANE:(j + 1) * LANE], tcq_ref)
        xr = pltpu.roll(x, C_HEAD, 1)
        g = (2 * j) // C_GROUP
        for half in range(2):
            h = 2 * j + half
            src = x if half == g else xr
            keep = lo if g == 0 else jnp.logical_not(lo)
            qc_out[0, :, h * LANE:(h + 1) * LANE] = jnp.where(keep, src, 0.0).astype(BF16)
    kc_out[0] = head_norm(zc[:, BW:BW + C_KV_W], tck_ref).astype(BF16)
    vv = zc[:, BW + C_KV_W:BW + 2 * C_KV_W]
    vr = pltpu.roll(vv, C_HEAD, 1)
    vc_out[0, :, 0:LANE] = jnp.where(lo, vv, vr).astype(BF16)
    vc_out[0, :, LANE:2 * LANE] = jnp.where(lo, vr, vv).astype(BF16)


def _bcproj_call(zb, zc, lp):
    B, S, _ = zb.shape
    tt = TOK_TILE
    full = lambda shape: pl.BlockSpec(shape, lambda t, b: (0,) * len(shape))
    tok = lambda w: pl.BlockSpec((1, tt, w), lambda t, b: (b, t, 0))
    tab = pl.BlockSpec((3, tt, LANE), lambda t, b: (0, t, 0))
    shp = lambda w: jax.ShapeDtypeStruct((B, S, w), BF16)
    return pl.pallas_call(
        _bcproj_kernel,
        out_shape=(shp(B_HEADS * LANE), shp(B_HEADS * LANE), shp(BW),
                   shp(C_HEADS * LANE), shp(C_KV_W), shp(2 * LANE)),
        grid=(S // tt, B),
        in_specs=[tok(B_COLS_P), tok(C_COLS), tab, tab, tab, tab,
                  full((1, B_Q_LORA)), full((1, B_KV_LORA)),
                  full((B_Q_LORA, B_HEADS * LANE)), full((B_KV_LORA, B_HEADS * LANE)),
                  full((B_KV_LORA, BW)), full((LANE, LANE)), full((LANE, LANE))],
        out_specs=(tok(B_HEADS * LANE), tok(B_HEADS * LANE), tok(BW),
                   tok(C_HEADS * LANE), tok(C_KV_W), tok(2 * LANE)),
        compiler_params=_params(("parallel", "arbitrary")),
        name="bcproj",
    )(zb, zc, lp["tab_bq"], lp["tab_bk"], lp["tab_cq"], lp["tab_ck"],
      lp["q_ln"], lp["kv_ln"], lp["w_uq"], lp["w_uk"], lp["w_uv"],
      _block_ones(LANE, LANE), _block_ones(LANE, C_HEAD))


def _attb_kernel(q_ref, k_ref, v_ref, o_ref):
    lane = lax.broadcasted_iota(jnp.int32, (1, LANE), 1)
    lo = lane < B_V
    for j in range(B_HEADS // 2):
        outs = []
        for h in (2 * j, 2 * j + 1):
            sl = slice(h * LANE, (h + 1) * LANE)
            s = _dot(q_ref[:, sl], k_ref[0, :, sl], NT)
            m = jnp.max(s, axis=-1, keepdims=True)
            p = jnp.exp2(s - m)
            den = jnp.sum(p, axis=-1, keepdims=True)
            o = _dot(p.astype(BF16), v_ref[0, :, j * LANE:(j + 1) * LANE])
            outs.append(o / den)
        o_ref[:, j * LANE:(j + 1) * LANE] = jnp.where(lo, outs[0], outs[1])


def _row_block(n_rows, width, seq_len, row0):
    return pl.BlockSpec(
        (pl.Element(n_rows), pl.Element(width)),
        lambda b, t: (pl.multiple_of(b * seq_len + row0 + t * n_rows, TOK_TILE), 0))


def _attb_call(qb, kb, vb, row0, n_rows, tq, n_keys):
    B, S, _ = qb.shape
    width = B_HEADS * LANE
    tiles = n_rows // tq
    return pl.pallas_call(
        _attb_kernel,
        out_shape=jax.ShapeDtypeStruct((B * n_rows, BW), F32),
        grid=(B, tiles),
        in_specs=[
            _row_block(tq, width, S, row0),
            pl.BlockSpec((1, n_keys, width), lambda b, t: (b, 0, 0)),
            pl.BlockSpec((1, n_keys, BW), lambda b, t: (b, 0, 0)),
        ],
        out_specs=pl.BlockSpec((tq, BW), lambda b, t: (b * tiles + t, 0)),
        compiler_params=_params(("parallel", "arbitrary")),
        name="attb",
    )(qb.reshape(B * S, width), kb, vb).reshape(B, n_rows, BW)


def _attc_kernel(tile0, ctx_len, n_lat, sink_ref, q_ref, k_ref, v_ref, *rest):
    valid_refs, o_ref = rest[:-1], rest[-1]
    W = WINDOW
    ctx_tiles = ctx_len // W
    n_blk = n_lat // W
    lane = lax.broadcasted_iota(jnp.int32, (1, LANE), 1)
    lo = lane < C_HEAD
    rows4 = lax.broadcasted_iota(jnp.int32, (C_GROUP * W, 1), 0)

    def attend(rows, k_cat, v_cat, mask):
        for g in range(C_KV_HEADS):
            q_st = jnp.concatenate(
                [q_ref[0, rows, h * LANE:(h + 1) * LANE] for h in range(g * C_GROUP, (g + 1) * C_GROUP)],
                axis=0)
            s = _dot(q_st, k_cat, NT)
            if mask is not None:
                s = jnp.where(jnp.concatenate([mask] * C_GROUP, axis=0) > 0.5, s, NEG)
            sk = jnp.zeros((C_GROUP * W, 1), F32)
            for i in range(C_GROUP):
                sk = jnp.where(rows4 // W == i, sink_ref[g * C_GROUP + i], sk)
            m = jnp.maximum(jnp.max(s, axis=-1, keepdims=True), sk)
            p = jnp.exp2(s - m)
            den = jnp.sum(p, axis=-1, keepdims=True) + jnp.exp2(sk - m)
            o = _dot(p.astype(BF16), v_cat[:, g * LANE:(g + 1) * LANE]) / den
            for jj in range(C_GROUP // 2):
                o_lo = o[(2 * jj) * W:(2 * jj + 1) * W]
                o_hi = o[(2 * jj + 1) * W:(2 * jj + 2) * W]
                col = (g * C_GROUP // 2 + jj) * LANE
                o_ref[0, rows, col:col + LANE] = jnp.where(lo, o_lo, o_hi)

    def latent(i, rows):
        j = (pl.program_id(1) + tile0) * ATTC_BLOCKS + i - ctx_tiles
        bm = jnp.maximum(j - 1, 0)
        bp = jnp.minimum(j + 1, n_blk - 1)

        def blk(ref, b):
            return ref[0, pl.ds(pl.multiple_of(ctx_len + b * W, W), W), :]

        k_cat = jnp.concatenate([k_ref[0, 0:ctx_len, :], blk(k_ref, bm), blk(k_ref, j), blk(k_ref, bp)], axis=0)
        v_cat = jnp.concatenate([v_ref[0, 0:ctx_len, :], blk(v_ref, bm), blk(v_ref, j), blk(v_ref, bp)], axis=0)
        attend(rows, k_cat, v_cat, valid_refs[i][0])

    def context(rows):
        attend(rows, k_ref[0, 0:ctx_len, :], v_ref[0, 0:ctx_len, :], None)

    t = pl.program_id(1) + tile0
    ctx_steps = ctx_tiles // ATTC_BLOCKS
    blocks = [(i, slice(i * W, (i + 1) * W)) for i in range(ATTC_BLOCKS)]

    def all_context():
        for _, rows in blocks:
            context(rows)

    def all_latent():
        for i, rows in blocks:
            latent(i, rows)

    if tile0 < ctx_steps:
        pl.when(t < ctx_steps)(all_context)
        pl.when(t >= ctx_steps)(all_latent)
    else:
        all_latent()


def _attc_call(qc, kc, vc, sink, ctx_len, with_ctx):
    B, S, _ = qc.shape
    W = WINDOW
    nb = ATTC_BLOCKS
    ctx_steps = ctx_len // (nb * W)
    tile0 = 0 if with_ctx else ctx_steps
    n_steps = S // (nb * W) - tile0
    n_blk = (S - ctx_len) // W
    valid = _band_valid(ctx_len)

    def variant(t, i):
        j = (t + tile0) * nb + i - ctx_len // W
        return jnp.where(j == 0, 1, 0) + jnp.where(j == n_blk - 1, 2, 0)

    return pl.pallas_call(
        functools.partial(_attc_kernel, tile0, ctx_len, S - ctx_len),
        out_shape=jax.ShapeDtypeStruct((B, n_steps * nb * W, BW), F32),
        grid=(B, n_steps),
        in_specs=[
            pl.BlockSpec(memory_space=pltpu.SMEM),
            pl.BlockSpec((1, nb * W, C_HEADS * LANE), lambda b, t: (b, t + tile0, 0)),
            pl.BlockSpec((1, S, C_KV_W), lambda b, t: (b, 0, 0)),
            pl.BlockSpec((1, S, 2 * LANE), lambda b, t: (b, 0, 0)),
        ] + [pl.BlockSpec((1, W, ctx_len + 3 * W), lambda b, t, i=i: (variant(t, i), 0, 0)) for i in range(nb)],
        out_specs=pl.BlockSpec((1, nb * W, BW), lambda b, t: (b, t, 0)),
        compiler_params=_params(("parallel", "arbitrary")),
        name="attc",
    )(sink, qc, kc, vc, *([valid] * nb))


def _band_valid(ctx_len):
    W = WINDOW
    qi = np.arange(W)[:, None]
    cb = np.arange(ctx_len + 3 * W)[None, :] - ctx_len
    base = (cb < 0) | (np.abs(cb - W - qi) <= WINDOW)
    out = []
    for variant in range(4):
        first, last = bool(variant & 1), bool(variant & 2)
        ok = base.copy()
        if first:
            ok &= ~((cb >= 0) & (cb < W))
        if last:
            ok &= ~(cb >= 2 * W)
        out.append(ok)
    return jnp.asarray(np.stack(out), dtype=F32)


def _merge_kernel(tile0, ctx_tiles, y0_ref, y1_ref, bonus_ref, ybc_ref, ybl_ref, yc_ref, x_ref,
                  shift_ref, scale_ref, gate_ref,
                  g_ref, wg_ref, gng_ref, gnb_ref, ones_ref, wbo_ref, wout_ref, o_ref):
    x = x_ref[0]
    hb = _modulated_norm(x, g_ref[...], shift_ref[0], scale_ref[0]).astype(BF16)
    ones = ones_ref[...]
    y = y0_ref[0] + y1_ref[0]
    mu = _head_sums(y, ones) * (1.0 / A_HEAD)
    dlt = y - mu
    var = _head_sums(dlt * dlt, ones) * (1.0 / A_HEAD)
    ya = dlt * lax.rsqrt(var + A_GN_EPS) * gng_ref[...] + gnb_ref[...] + bonus_ref[0]
    m = None
    zg0 = N_BRANCH * BW
    yb = jnp.where(pl.program_id(1) + tile0 < ctx_tiles, ybc_ref[0], ybl_ref[0])
    for n, y_n in enumerate((ya, yb, yc_ref[0])):
        g = _dot(hb, wg_ref[:, n * BW:(n + 1) * BW])
        u = (y_n * _silu(g)).astype(BF16)
        zg = _dot(hb, wg_ref[:, zg0 + n * D_MODEL:zg0 + (n + 1) * D_MODEL])
        term = _sigmoid(zg) * _dot(u, wbo_ref[n])
        m = term if m is None else m + term
    o_ref[0] = x + gate_ref[0] * _dot(m.astype(BF16), wout_ref[...])


def _merge_call(y0, y1, bonus, yb_ctx, yb_lat, yc, xall, modl, norm_g, lp, ones_a, ctx_len, with_ctx):
    B, S, _ = xall.shape
    tt = TOK_TILE
    ctx_tiles = ctx_len // tt
    tile0 = 0 if with_ctx else ctx_tiles
    n_tiles = S // tt - tile0
    row = _mod_row(ctx_tiles, B)
    mod3 = modl.reshape(modl.shape[0], 1, 3 * D_MODEL)
    mod = lambda col: pl.BlockSpec((1, 1, D_MODEL), lambda b, t: (row(b, t + tile0), 0, col))
    once = lambda shape: pl.BlockSpec(shape, lambda b, t: (0,) * len(shape), pipeline_mode=pl.Buffered(1))
    tok = lambda w: pl.BlockSpec((1, tt, w), lambda b, t: (b, t + tile0, 0))
    att = lambda w: pl.BlockSpec((1, tt, w), lambda b, t: (b, t, 0))
    n_gate = sum(w for _, w in IN_SEGS_GATE)
    ybc = pl.BlockSpec((1, tt, BW), lambda b, t: (b, jnp.minimum(t + tile0, ctx_tiles - 1), 0))
    ybl = pl.BlockSpec((1, tt, BW), lambda b, t: (b, jnp.maximum(t + tile0 - ctx_tiles, 0), 0))
    return pl.pallas_call(
        functools.partial(_merge_kernel, tile0, ctx_tiles),
        out_shape=jax.ShapeDtypeStruct((B, n_tiles * tt, D_MODEL), F32),
        grid=(B, n_tiles),
        in_specs=[tok(BW), tok(BW), tok(BW), ybc, ybl, att(BW), tok(D_MODEL),
                  mod(0), mod(1), mod(2),
                  once((1, D_MODEL)), once((D_MODEL, n_gate)),
                  once((1, BW)), once((1, BW)), once((MXU_W, MXU_W)),
                  once((N_BRANCH, BW, D_MODEL)), once((D_MODEL, D_MODEL))],
        out_specs=pl.BlockSpec((1, tt, D_MODEL), lambda b, t: (b, t, 0)),
        compiler_params=_params(("parallel", "arbitrary")),
        name="merge",
    )(y0, y1, bonus, yb_ctx, yb_lat, yc, xall, mod3, mod3, mod3, norm_g.reshape(1, D_MODEL), lp["w_gate"],
      lp["gn_g"], lp["gn_b"], ones_a, lp["w_bo"], lp["w_out"])


def _rope_tables(n_ctx, n_lat, rot_dim, head_w, lane0):
    rows = n_lat // GRID_W
    row = jnp.repeat(jnp.arange(rows), GRID_W).astype(F32)
    colp = jnp.tile(jnp.arange(GRID_W), rows).astype(F32)
    q = rot_dim // 4
    inv = ROPE_THETA ** (-(2.0 * jnp.arange(q, dtype=F32)) / (rot_dim // 2))
    ang = jnp.concatenate([row[:, None] * inv, colp[:, None] * inv], axis=-1)
    cos, sin = jnp.cos(ang), jnp.sin(ang)
    zero = jnp.zeros_like(sin[:, :q])
    cos_g = jnp.concatenate([cos[:, :q], cos[:, :q], cos[:, q:], cos[:, q:]], axis=-1)
    sp_g = jnp.concatenate([zero, sin[:, :q], zero, sin[:, q:]], axis=-1)
    sm_g = jnp.concatenate([-sin[:, :q], zero, -sin[:, q:], zero], axis=-1)

    def place(t, fill):
        pad_l = jnp.full((n_lat, lane0), fill, F32)
        pad_r = jnp.full((n_lat, head_w - lane0 - rot_dim), fill, F32)
        grp = jnp.concatenate([pad_l, t, pad_r], axis=-1)
        lat = jnp.tile(grp, (1, LANE // head_w))
        return jnp.concatenate([jnp.full((n_ctx, LANE), fill, F32), lat], axis=0)

    return jnp.stack([place(cos_g, 1.0), place(sp_g, 0.0), place(sm_g, 0.0)])


def _block_ones(n, blk):
    i = np.arange(n) // blk
    return jnp.asarray(i[:, None] == i[None, :], dtype=BF16)


def _pad_lanes(g, width):
    return jnp.pad(g, (0, width - g.shape[0]))


def _fold_gain(tab, gain, shift, scale):
    g = jnp.stack([gain, jnp.roll(gain, shift), jnp.roll(gain, -shift)]) * scale
    return tab * g[:, None, :]


def _layer_params(i, tab_b, tab_c, w_in, a_mu_prev, a_mu_next, a_w0, a_w_up, a_a0, a_a_up, a_k_k, a_k_a, a_r_k,
                  a_gn_g, a_gn_b, b_q_ln, b_kv_ln, b_w_uq, b_w_ukv, b_qn_g, b_kn_g,
                  c_qn_g, c_kn_g, c_sink, w_branch_out, w_out):
    w = w_in[i].astype(BF16)
    c0 = 0
    segs = {}
    for name, width in (("za", A_COLS), ("ga", BW), ("zb", B_COLS), ("gb", BW),
                        ("zc", C_COLS), ("gc", BW), ("zg", N_BRANCH * D_MODEL)):
        segs[name] = w[:, c0:c0 + width]
        c0 += width
    zb = segs["zb"]
    zpad = lambda n: jnp.zeros((D_MODEL, n), w.dtype)
    segs["zb"] = jnp.concatenate(
        [zb[:, :B_Q_LORA + B_KV_LORA], zpad(B_NOPE), zb[:, B_Q_LORA + B_KV_LORA:], zpad(LANE - B_NOPE - B_ROPE)],
        axis=-1)
    w_mix = jnp.concatenate([segs[n] for n, _ in IN_SEGS_MIX], axis=-1).astype(BF16)
    w_gate = jnp.concatenate([segs[n] for n, _ in IN_SEGS_GATE], axis=-1).astype(BF16)

    uq = b_w_uq[i].reshape(B_Q_LORA, B_HEADS, B_QK)
    uq = jnp.pad(uq, ((0, 0), (0, 0), (0, LANE - B_QK))).reshape(B_Q_LORA, B_HEADS * LANE)
    ukv = b_w_ukv[i].reshape(B_KV_LORA, B_HEADS, B_NOPE + B_V)
    uk = jnp.pad(ukv[:, :, :B_NOPE], ((0, 0), (0, 0), (0, LANE - B_NOPE))).reshape(B_KV_LORA, B_HEADS * LANE)
    uv = ukv[:, :, B_NOPE:].reshape(B_KV_LORA, BW)
    row = lambda t: t.reshape(1, -1)
    return dict(
        w_mix=w_mix, w_gate=w_gate,
        mu_prev=row(a_mu_prev[i]), mu_next=row(a_mu_next[i]),
        w0=a_w0[i], w_up=a_w_up[i], a0=a_a0[i], a_up=a_a_up[i],
        k_k=row(a_k_k[i]), k_a=row(a_k_a[i]), r_k=row(a_r_k[i]),
        gn_g=row(a_gn_g[i]), gn_b=row(a_gn_b[i]),
        q_ln=row(b_q_ln[i]), kv_ln=row(b_kv_ln[i]),
        w_uq=uq.astype(BF16), w_uk=uk.astype(BF16), w_uv=uv.astype(BF16),
        tab_bq=_fold_gain(tab_b, _pad_lanes(b_qn_g[i], LANE), B_ROPE // 4, B_QK ** -0.5 * LOG2E),
        tab_bk=_fold_gain(tab_b, _pad_lanes(b_kn_g[i], LANE), B_ROPE // 4, 1.0),
        tab_cq=_fold_gain(tab_c, jnp.tile(c_qn_g[i], 2), C_HEAD // 4, C_HEAD ** -0.5 * LOG2E),
        tab_ck=_fold_gain(tab_c, jnp.tile(c_kn_g[i], 2), C_HEAD // 4, 1.0),
        sink=c_sink[i] * LOG2E,
        w_bo=w_branch_out[i].astype(BF16), w_out=w_out[i].astype(BF16),
    )


def kernel(x, c, ctx, c_ctx, ada_w, ada_b, norm_g, w_in, a_mu_prev, a_mu_next, a_w0, a_w_up, a_a0, a_a_up, a_k_k, a_k_a, a_r_k, a_gn_g, a_gn_b, b_q_ln, b_kv_ln, b_w_uq, b_w_ukv, b_qn_g, b_kn_g, c_qn_g, c_kn_g, c_sink, w_branch_out, w_out):
    B, T, D = x.shape
    L = ctx.shape[1]
    depth = ada_w.shape[0]
    assert D == D_MODEL and L % TOK_TILE == 0 and T % TOK_TILE == 0 and T % GRID_W == 0
    assert T % ATTB_TQ == 0 and B % SCAN_ROWS == 0 and L % (ATTC_BLOCKS * WINDOW) == 0

    mod_rows = -(-(B + 1) // 8) * 8
    cs = jnp.concatenate([c, c_ctx[None, :], jnp.zeros((mod_rows - B - 1, D), c.dtype)], axis=0)
    mod = _mod_call(cs, ada_w, ada_b)

    tab_b = _rope_tables(L, T, B_ROPE, LANE, B_NOPE)
    tab_c = _rope_tables(L, T, C_HEAD, C_HEAD, 0)
    ones_a = _block_ones(MXU_W, A_HEAD)

    xall = jnp.concatenate([ctx, x], axis=1)
    for i in range(depth):
        with_ctx = i < depth - 1
        lp = _layer_params(i, tab_b, tab_c, w_in, a_mu_prev, a_mu_next, a_w0, a_w_up, a_a0, a_a_up, a_k_k, a_k_a,
                           a_r_k, a_gn_g, a_gn_b, b_q_ln, b_kv_ln, b_w_uq, b_w_ukv, b_qn_g, b_kn_g,
                           c_qn_g, c_kn_g, c_sink, w_branch_out, w_out)
        zb, zc, rvk, dkn, lw, bonus = _inprep_call(xall, mod[i], norm_g[i], lp, ones_a, L)
        y0, y1 = _scan_call(rvk, dkn, lw, L)
        qb, kb, vb, qc, kc, vc = _bcproj_call(zb, zc, lp)
        yb_lat = _attb_call(qb, kb, vb, L, T, ATTB_TQ, L + T)
        yb_ctx = _attb_call(qb, kb, vb, 0, L, TOK_TILE, L) if with_ctx else yb_lat
        yc = _attc_call(qc, kc, vc, lp["sink"], L, with_ctx)
        xall = _merge_call(y0, y1, bonus, yb_ctx, yb_lat, yc, xall, mod[i], norm_g[i], lp, ones_a, L,
                           with_ctx)
    return xall
```

```python
import functools

import numpy as np
import jax
import jax.numpy as jnp
from jax import lax
from jax.experimental import pallas as pl
from jax.experimental.pallas import tpu as pltpu

F32 = jnp.float32
BF16 = jnp.bfloat16

D_MODEL = 1024
GRID_W = 64
ROPE_THETA = 10000.0
NORM_EPS = 1e-6
NEG = -1e30
BW = 512
N_BRANCH = 3

A_HEAD = 64
A_HEADS = BW // A_HEAD
A_LORA = 64
A_GN_EPS = 64e-5
A_COLS = 3 * BW + 4 * A_LORA

B_HEADS = 8
B_NOPE = 64
B_ROPE = 32
B_QK = B_NOPE + B_ROPE
B_V = BW // B_HEADS
B_Q_LORA = 256
B_KV_LORA = 128
B_COLS = B_Q_LORA + B_KV_LORA + B_ROPE
B_COLS_P = 512

C_HEAD = 64
C_HEADS = BW // C_HEAD
C_KV_HEADS = 2
C_GROUP = C_HEADS // C_KV_HEADS
C_KV_W = C_KV_HEADS * C_HEAD
C_COLS = BW + 2 * C_KV_W
WINDOW = 128

LANE = 128
MXU_W = 256
CHUNK = 64
SCAN_ROWS = 8
TOK_TILE = 256
V7X_VMEM_BYTES = 64 * 1024 * 1024
VMEM_LIMIT = V7X_VMEM_BYTES - 8 * 1024 * 1024

ATTB_TQ = 512
ATTC_BLOCKS = 2
LOG2E = 1.4426950408889634

IN_SEGS_MIX = (("za", A_COLS), ("zb", B_COLS_P), ("zc", C_COLS))
IN_SEGS_GATE = (("ga", BW), ("gb", BW), ("gc", BW), ("zg", N_BRANCH * D_MODEL))

NN = (((1,), (0,)), ((), ()))
NT = (((1,), (1,)), ((), ()))


def _dot(a, b, dims=NN):
    return lax.dot_general(a, b, dims, preferred_element_type=F32)


def _split(x):
    hi = x.astype(BF16)
    lo = (x - hi.astype(F32)).astype(BF16)
    return hi, lo


def _mm(a, b, dims=NN, passes=1):
    if passes == 1:
        return _dot(a.astype(BF16), b.astype(BF16), dims)
    ah, al = _split(a)
    bh, bl = _split(b)
    return _dot(ah, bh, dims) + (_dot(ah, bl, dims) + _dot(al, bh, dims))


def _mm_exact_rhs(a, b_bf16, parts=2):
    acc = None
    rem = a
    for _ in range(parts):
        p = rem.astype(BF16)
        t = _dot(p, b_bf16)
        acc = t if acc is None else acc + t
        rem = rem - p.astype(F32)
    return acc


def _head_sums(a, ones_blk):
    w = ones_blk.shape[0]
    return jnp.concatenate(
        [_mm_exact_rhs(a[:, c:c + w], ones_blk) for c in range(0, a.shape[1], w)], axis=1)


def _mm_exact_lhs(a_bf16, b, parts=3):
    acc = None
    rem = b
    for _ in range(parts):
        p = rem.astype(BF16)
        t = _dot(a_bf16, p)
        acc = t if acc is None else acc + t
        rem = rem - p.astype(F32)
    return acc


def _sigmoid(x):
    return 1.0 / (1.0 + jnp.exp(-x))


def _silu(x):
    return x * _sigmoid(x)


def _modulated_norm(x, g, shift, scale):
    ms = jnp.mean(x * x, axis=-1, keepdims=True)
    return (x * lax.rsqrt(ms + NORM_EPS) * g) * (1.0 + scale) + shift


def _params(sem):
    return pltpu.CompilerParams(dimension_semantics=sem, vmem_limit_bytes=VMEM_LIMIT)


def _mod_kernel(c_ref, w_ref, b_ref, o_ref):
    s = _silu(c_ref[...])
    o_ref[0] = _mm(s, w_ref[0], passes=3) + b_ref[0]


def _mod_call(cs, ada_w, ada_b):
    n_layers = ada_w.shape[0]
    rows = cs.shape[0]
    return pl.pallas_call(
        _mod_kernel,
        out_shape=jax.ShapeDtypeStruct((n_layers, rows, 3 * D_MODEL), F32),
        grid=(n_layers, 3),
        in_specs=[
            pl.BlockSpec((rows, D_MODEL), lambda l, n: (0, 0)),
            pl.BlockSpec((1, D_MODEL, D_MODEL), lambda l, n: (l, 0, n)),
            pl.BlockSpec((1, 1, D_MODEL), lambda l, n: (l, 0, n)),
        ],
        out_specs=pl.BlockSpec((1, rows, D_MODEL), lambda l, n: (l, 0, n)),
        compiler_params=_params(("arbitrary", "arbitrary")),
        name="mod",
    )(cs, ada_w, ada_b.reshape(n_layers, 1, 3 * D_MODEL))


def _mod_row(ctx_tiles, n_batch_rows):
    return lambda b, t: jnp.where(t < ctx_tiles, n_batch_rows, b)


def _inprep_kernel(ctx_tiles, n_tiles,
                   x_ref, xp_ref, xn_ref, shift_ref, scale_ref, g_ref, w_ref,
                   mup_ref, mun_ref, w0_ref, wup_ref, a0_ref, aup_ref, kk_ref, ka_ref, rk_ref, ones_ref,
                   zb_out, zc_out, rvk_out, dkn_out, lw_out, bonus_out, hb_ref, za_ref):
    t = pl.program_id(1)
    tt = TOK_TILE
    first = jnp.logical_or(t == 0, t == ctx_tiles)
    last = jnp.logical_or(t == ctx_tiles - 1, t == n_tiles - 1)
    g, shift, scale = g_ref[...], shift_ref[0], scale_ref[0]
    hb_ref[0:tt, :] = _modulated_norm(x_ref[0], g, shift, scale).astype(BF16)
    hb_ref[tt:tt + 8, :] = _modulated_norm(xp_ref[0], g, shift, scale).astype(BF16)
    hb_ref[tt + 8:tt + 16, :] = _modulated_norm(xn_ref[0], g, shift, scale).astype(BF16)
    hb = hb_ref[...]
    for c0 in range(0, A_COLS, MXU_W):
        za_ref[:, c0:c0 + MXU_W] = _dot(hb, w_ref[:, c0:c0 + MXU_W])
    hb_t = hb[0:tt]
    zb_out[0] = _dot(hb_t, w_ref[:, A_COLS:A_COLS + B_COLS_P]).astype(BF16)
    for c0 in range(0, C_COLS, MXU_W):
        col = A_COLS + B_COLS_P + c0
        zc_out[0, :, c0:c0 + MXU_W] = _dot(hb_t, w_ref[:, col:col + MXU_W]).astype(BF16)

    z = za_ref[0:tt, :]
    row = lax.broadcasted_iota(jnp.int32, (tt, 1), 0)
    halo_prev = jnp.where(first, 0.0, za_ref[tt + 7:tt + 8, :])
    halo_next = jnp.where(last, 0.0, za_ref[tt + 8:tt + 9, :])
    prev = jnp.where(row == 0, halo_prev, pltpu.roll(z, 1, 0))
    nxt = jnp.where(row == tt - 1, halo_next, pltpu.roll(z, tt - 1, 0))
    zsh = z + mup_ref[...] * (prev - z) + mun_ref[...] * (nxt - z)

    r = zsh[:, 0:BW]
    k = zsh[:, BW:2 * BW]
    v = zsh[:, 2 * BW:3 * BW]
    ones = ones_ref[...]
    kk = k * kk_ref[...]
    ss = _head_sums(kk * kk, ones)
    kk = kk * lax.rsqrt(jnp.maximum(ss, 1e-24))
    rvk_out[0, :, 0:BW] = r
    rvk_out[0, :, BW:2 * BW] = v
    rvk_out[0, :, 2 * BW:3 * BW] = kk
    kd_sum = None
    for d in range(2):
        wd = zsh[:, 3 * BW + d * A_LORA:3 * BW + (d + 1) * A_LORA]
        ad = zsh[:, 3 * BW + 2 * A_LORA + d * A_LORA:3 * BW + 2 * A_LORA + (d + 1) * A_LORA]
        u = -(w0_ref[d:d + 1, :] + _mm(jnp.tanh(wd), wup_ref[d], passes=3))
        softplus = jnp.maximum(u, 0.0) + jnp.log(1.0 + jnp.exp(-jnp.abs(u)))
        w_log = -softplus - 0.5
        lw = -jnp.exp(w_log)
        a = _sigmoid(a0_ref[d:d + 1, :] + _mm(ad, aup_ref[d], passes=3))
        kd = k * (1.0 + (a - 1.0) * ka_ref[...])
        lw_out[d, 0] = lw
        dkn_out[d, 0, :, 0:BW] = kd
        dkn_out[d, 0, :, BW:2 * BW] = -(kk * a)
        kd_sum = kd if kd_sum is None else kd_sum + kd
    bonus_out[0] = (_head_sums(r * kd_sum * rk_ref[...], ones) * v).astype(BF16)


def _inprep_call(xall, modl, norm_g, lp, ones_a, ctx_len):
    B, S, _ = xall.shape
    tt = TOK_TILE
    n_tiles = S // tt
    hb = tt // 8
    n8 = S // 8
    row = _mod_row(ctx_len // tt, B)
    mod3 = modl.reshape(modl.shape[0], 1, 3 * D_MODEL)
    mod = lambda col: pl.BlockSpec((1, 1, D_MODEL), lambda b, t: (row(b, t), 0, col))
    once = lambda shape: pl.BlockSpec(shape, lambda b, t: (0,) * len(shape), pipeline_mode=pl.Buffered(1))
    s3 = lambda w: pl.BlockSpec((1, tt, w), lambda b, t: (b, t, 0))
    s4 = lambda w: pl.BlockSpec((2, 1, tt, w), lambda b, t: (0, b, t, 0))
    o3 = lambda w, dt: jax.ShapeDtypeStruct((B, S, w), dt)
    o4 = lambda w, dt: jax.ShapeDtypeStruct((2, B, S, w), dt)
    n_mix = sum(w for _, w in IN_SEGS_MIX)
    return pl.pallas_call(
        functools.partial(_inprep_kernel, ctx_len // tt, n_tiles),
        out_shape=(o3(B_COLS_P, BF16), o3(C_COLS, BF16), o3(3 * BW, F32), o4(2 * BW, F32),
                   o4(BW, F32), o3(BW, BF16)),
        grid=(B, n_tiles),
        in_specs=[
            s3(D_MODEL),
            pl.BlockSpec((1, 8, D_MODEL), lambda b, t: (b, jnp.maximum(t * hb - 1, 0), 0)),
            pl.BlockSpec((1, 8, D_MODEL), lambda b, t: (b, jnp.minimum((t + 1) * hb, n8 - 1), 0)),
            mod(0), mod(1), once((1, D_MODEL)), once((D_MODEL, n_mix)),
            once((1, A_COLS)), once((1, A_COLS)),
            once((2, BW)), once((2, A_LORA, BW)), once((2, BW)), once((2, A_LORA, BW)),
            once((1, BW)), once((1, BW)), once((1, BW)), once((MXU_W, MXU_W)),
        ],
        out_specs=(s3(B_COLS_P), s3(C_COLS), s3(3 * BW), s4(2 * BW), s4(BW), s3(BW)),
        scratch_shapes=[pltpu.VMEM((tt + 16, D_MODEL), BF16), pltpu.VMEM((tt + 16, A_COLS), F32)],
        compiler_params=_params(("parallel", "arbitrary")),
        name="inprep",
    )(xall, xall, xall, mod3, mod3, norm_g.reshape(1, D_MODEL), lp["w_mix"],
      lp["mu_prev"], lp["mu_next"], lp["w0"], lp["w_up"], lp["a0"], lp["a_up"],
      lp["k_k"], lp["k_a"], lp["r_k"], ones_a)


def _scan_kernel(f_rvk, f_dkn, f_lw, b_rvk, b_dkn, b_lw, yf_ref, yb_ref, state_ref):
    C = CHUNK

    @pl.when(pl.program_id(1) == 0)
    def _():
        state_ref[...] = jnp.zeros_like(state_ref)

    lane = lax.broadcasted_iota(jnp.int32, (1, LANE), 1)
    lo = lane < A_HEAD
    row2 = lax.broadcasted_iota(jnp.int32, (2 * C, 2 * C), 0)
    col2 = lax.broadcasted_iota(jnp.int32, (2 * C, 2 * C), 1)
    rows = lax.broadcasted_iota(jnp.int32, (C, C), 0)
    cols = lax.broadcasted_iota(jnp.int32, (C, C), 1)
    n_pairs = A_HEADS // 2

    n_rows = f_rvk.shape[0]
    dirs = []
    for row, d in [(row, d) for row in range(n_rows) for d in range(2)]:
        rvk_ref, dkn_ref, lw_ref = (f_rvk, f_dkn, f_lw) if d == 0 else (b_rvk, b_dkn, b_lw)
        order = (rows - cols) if d == 0 else (cols - rows)
        lw = lw_ref[0, row]
        kd = dkn_ref[0, row, :, 0:BW]
        nb = dkn_ref[0, row, :, BW:2 * BW]
        cum = _mm_exact_lhs((order >= 0).astype(BF16), lw)
        tot = cum[C - 1:C, :] if d == 0 else cum[0:1, :]
        p_inv = jnp.exp(-cum)
        p_tot = jnp.exp(tot - cum)
        kt = rvk_ref[row, :, 2 * BW:3 * BW] * jnp.exp(cum - lw)
        order2 = (row2 % C - col2 % C) if d == 0 else (col2 % C - row2 % C)
        keep = order2 >= jnp.where(row2 < C, 0, 1)
        e_rows = jnp.broadcast_to(jnp.exp(tot), (LANE, BW))
        dirs.append(dict(
            keep=keep, kt=kt, v=rvk_ref[row, :, BW:2 * BW], fwd=(d == 0),
            y_ref=yf_ref if d == 0 else yb_ref, row=row,
            lhs=jnp.concatenate([rvk_ref[row, :, 0:BW] * jnp.exp(cum), kt], axis=0),
            rhs_e=jnp.concatenate([nb * p_inv, kd * p_inv], axis=0),
            rhs_o=jnp.concatenate([kd * p_inv, nb * p_inv], axis=0),
            nkt=jnp.concatenate([nb * p_tot, kd * p_tot, e_rows], axis=0).T))

    units = [(d, h) for d in range(len(dirs)) for h in range(A_HEADS)]
    slab = lambda h: slice((h // 2) * LANE, (h // 2 + 1) * LANE)
    own = lambda h: lo if h % 2 == 0 else jnp.logical_not(lo)

    lhs_m, top, bot = [], [], []
    for d, h in units:
        x = dirs[d]
        lm = jnp.where(own(h), x["lhs"][:, slab(h)], 0.0)
        rhs = x["rhs_o" if h % 2 else "rhs_e"][:, slab(h)]
        qk = jnp.where(x["keep"], _mm(lm, rhs, NT), 0.0)
        lhs_m.append(lm)
        top.append(qk[0:C])
        bot.append(qk[C:2 * C])
    pairs = [(d, j) for d in range(len(dirs)) for j in range(n_pairs)]
    r_i = lax.broadcasted_iota(jnp.int32, (C, 2 * C), 0)
    c_i = lax.broadcasted_iota(jnp.int32, (C, 2 * C), 1) % C
    eye2 = (r_i == c_i).astype(F32)

    def couple(b, fwd):
        late, early = (r_i, c_i) if fwd else (c_i, r_i)
        return (r_i // (2 * b) == c_i // (2 * b)) & ((late // b) % 2 == 1) & ((early // b) % 2 == 0)

    bd = lambda m: jnp.concatenate([jnp.where(lo, m, 0.0), jnp.where(lo, 0.0, m)], axis=0)
    n_pair = [jnp.where(lo, bot[2 * p], bot[2 * p + 1]) for p in range(len(pairs))]
    fwd_of = [dirs[d]["fwd"] for d, _ in pairs]
    t_inv = [eye2 + jnp.where(couple(1, f), n, 0.0) for n, f in zip(n_pair, fwd_of)]
    b = 2
    while b < C:
        et = [_mm(jnp.where(couple(b, f), n, 0.0), bd(t), NN)
              for n, t, f in zip(n_pair, t_inv, fwd_of)]
        t_inv = [t + _mm(t, bd(m), NN) for t, m in zip(t_inv, et)]
        b *= 2
    wta = []
    for i, (d, h) in enumerate(units):
        t = t_inv[i // 2]
        kt_s = dirs[d]["kt"][:, slab(h)]
        if h % 2 == 0:
            wta.append(_mm(t[:, 0:C], jnp.where(lo, kt_s, bot[i]), NN))
        else:
            rhs = jnp.where(lo, bot[i], kt_s)
            wta.append(_mm(jnp.where(lo, 0.0, t), jnp.concatenate([jnp.zeros_like(rhs), rhs], axis=0),
                           NN))
    st = [state_ref[i] for i in range(len(pairs))]
    vp = [dirs[d]["v"][:, j * LANE:(j + 1) * LANE] for d, j in pairs]
    sv = [(jnp.concatenate([s, v], axis=0), jnp.concatenate([v, s], axis=0)) for s, v in zip(st, vp)]
    qt_u = [_mm(m, sv[i // 2][i % 2], NN) for i, m in enumerate(wta)]
    qt = [jnp.where(lo, qt_u[2 * i], qt_u[2 * i + 1]) for i in range(len(pairs))]
    qv = [(jnp.concatenate([q, v], axis=0), jnp.concatenate([v, q], axis=0)) for q, v in zip(qt, vp)]
    y_u = []
    for i, (d, h) in enumerate(units):
        p = i // 2
        y_l = jnp.concatenate([top[i], lhs_m[i][0:C]], axis=1)
        y_r = jnp.concatenate([qv[p][h % 2], st[p], st[p]], axis=0)
        y_u.append(_mm(y_l, y_r, NN))
    for p, (d, j) in enumerate(pairs):
        sl = slice(j * LANE, (j + 1) * LANE)
        dirs[d]["y_ref"][dirs[d]["row"], :, sl] = jnp.where(lo, y_u[2 * p], y_u[2 * p + 1])
        nk = dirs[d]["nkt"][sl]
        upd = _mm(nk[:, 0:2 * C], qv[p][0], NN)
        dec = nk[:, 2 * C:] * jnp.concatenate([st[p], st[p]], axis=0)
        state_ref[p] = jnp.where(lo, (dec + upd)[0:A_HEAD], (dec + upd)[A_HEAD:])


def _scan_call(rvk, dkn, lw, ctx_len):
    B, S, _ = rvk.shape
    C = CHUNK
    R = SCAN_ROWS
    nc = S // C
    lc = ctx_len // C
    rev = lambda c: jnp.where(c < lc, lc - 1 - c, nc - 1 - (c - lc))
    f3 = lambda w: pl.BlockSpec((R, C, w), lambda b, c: (b, c, 0))
    b3 = lambda w: pl.BlockSpec((R, C, w), lambda b, c: (b, rev(c), 0))
    f4 = lambda w: pl.BlockSpec((1, R, C, w), lambda b, c: (0, b, c, 0))
    b4 = lambda w: pl.BlockSpec((1, R, C, w), lambda b, c: (1, b, rev(c), 0))
    out = jax.ShapeDtypeStruct((B, S, BW), F32)
    return pl.pallas_call(
        _scan_kernel,
        out_shape=(out, out),
        grid=(B // R, nc),
        in_specs=[f3(3 * BW), f4(2 * BW), f4(BW), b3(3 * BW), b4(2 * BW), b4(BW)],
        out_specs=(f3(BW), b3(BW)),
        scratch_shapes=[pltpu.VMEM((R * A_HEADS, A_HEAD, LANE), F32)],
        compiler_params=_params(("parallel", "arbitrary")),
        name="scan",
    )(rvk, dkn, lw, rvk, dkn, lw)


def _rope(x, tab_ref, shift):
    return (x * tab_ref[0] + pltpu.roll(x, shift, 1) * tab_ref[1]
            + pltpu.roll(x, LANE - shift, 1) * tab_ref[2])


def _bcproj_kernel(zb_ref, zc_ref, tbq_ref, tbk_ref, tcq_ref, tck_ref, qln_ref, kvln_ref,
                   wuq_ref, wuk_ref, wuv_ref, onesb_ref, onesc_ref,
                   qb_out, kb_out, vb_out, qc_out, kc_out, vc_out):
    ones_b = onesb_ref[...]
    ones_c = onesc_ref[...]
    zb = zb_ref[0].astype(F32)
    cq = zb[:, 0:B_Q_LORA]
    ckv = zb[:, B_Q_LORA:B_Q_LORA + B_KV_LORA]
    kr_slab = zb[:, B_Q_LORA + B_KV_LORA:B_COLS_P]
    cqn = cq * lax.rsqrt(jnp.mean(cq * cq, axis=-1, keepdims=True) + NORM_EPS) * qln_ref[...]
    ckvn = ckv * lax.rsqrt(jnp.mean(ckv * ckv, axis=-1, keepdims=True) + NORM_EPS) * kvln_ref[...]
    q_all = _dot(cqn.astype(BF16), wuq_ref[...])
    k_all = _dot(ckvn.astype(BF16), wuk_ref[...])
    vb_out[0] = _dot(ckvn.astype(BF16), wuv_ref[...]).astype(BF16)
    for h in range(B_HEADS):
        sl = slice(h * LANE, (h + 1) * LANE)
        q = q_all[:, sl]
        rs = lax.rsqrt(_mm_exact_rhs(q * q, ones_b) * (1.0 / B_QK) + NORM_EPS)
        qb_out[0, :, sl] = (_rope(q, tbq_ref, B_ROPE // 4) * rs).astype(BF16)
        k = k_all[:, sl] + kr_slab
        rs = lax.rsqrt(_mm_exact_rhs(k * k, ones_b) * (1.0 / B_QK) + NORM_EPS)
        kb_out[0, :, sl] = (_rope(k, tbk_ref, B_ROPE // 4) * rs).astype(BF16)

    zc = zc_ref[0].astype(F32)
    lane = lax.broadcasted_iota(jnp.int32, (1, LANE), 1)
    lo = lane < C_HEAD

    def head_norm(x, tab_ref):
        ms = _mm_exact_rhs(x * x, ones_c) * (1.0 / C_HEAD)
        return _rope(x, tab_ref, C_HEAD // 4) * lax.rsqrt(ms + NORM_EPS)

    for j in range(C_HEADS // 2):
        x = head_norm(zc[:, j * LANE:(j + 1) * LANE], tcq_ref)
        xr = pltpu.roll(x, C_HEAD, 1)
        g = (2 * j) // C_GROUP
        for half in range(2):
            h = 2 * j + half
            src = x if half == g else xr
            keep = lo if g == 0 else jnp.logical_not(lo)
            qc_out[0, :, h * LANE:(h + 1) * LANE] = jnp.where(keep, src, 0.0).astype(BF16)
    kc_out[0] = head_norm(zc[:, BW:BW + C_KV_W], tck_ref).astype(BF16)
    vv = zc[:, BW + C_KV_W:BW + 2 * C_KV_W]
    vr = pltpu.roll(vv, C_HEAD, 1)
    vc_out[0, :, 0:LANE] = jnp.where(lo, vv, vr).astype(BF16)
    vc_out[0, :, LANE:2 * LANE] = jnp.where(lo, vr, vv).astype(BF16)


def _bcproj_call(zb, zc, lp):
    B, S, _ = zb.shape
    tt = TOK_TILE
    full = lambda shape: pl.BlockSpec(shape, lambda t, b: (0,) * len(shape))
    tok = lambda w: pl.BlockSpec((1, tt, w), lambda t, b: (b, t, 0))
    tab = pl.BlockSpec((3, tt, LANE), lambda t, b: (0, t, 0))
    shp = lambda w: jax.ShapeDtypeStruct((B, S, w), BF16)
    return pl.pallas_call(
        _bcproj_kernel,
        out_shape=(shp(B_HEADS * LANE), shp(B_HEADS * LANE), shp(BW),
                   shp(C_HEADS * LANE), shp(C_KV_W), shp(2 * LANE)),
        grid=(S // tt, B),
        in_specs=[tok(B_COLS_P), tok(C_COLS), tab, tab, tab, tab,
                  full((1, B_Q_LORA)), full((1, B_KV_LORA)),
                  full((B_Q_LORA, B_HEADS * LANE)), full((B_KV_LORA, B_HEADS * LANE)),
                  full((B_KV_LORA, BW)), full((LANE, LANE)), full((LANE, LANE))],
        out_specs=(tok(B_HEADS * LANE), tok(B_HEADS * LANE), tok(BW),
                   tok(C_HEADS * LANE), tok(C_KV_W), tok(2 * LANE)),
        compiler_params=_params(("parallel", "arbitrary")),
        name="bcproj",
    )(zb, zc, lp["tab_bq"], lp["tab_bk"], lp["tab_cq"], lp["tab_ck"],
      lp["q_ln"], lp["kv_ln"], lp["w_uq"], lp["w_uk"], lp["w_uv"],
      _block_ones(LANE, LANE), _block_ones(LANE, C_HEAD))


def _attb_kernel(q_ref, k_ref, v_ref, o_ref):
    lane = lax.broadcasted_iota(jnp.int32, (1, LANE), 1)
    lo = lane < B_V
    for j in range(B_HEADS // 2):
        outs = []
        for h in (2 * j, 2 * j + 1):
            sl = slice(h * LANE, (h + 1) * LANE)
            s = _dot(q_ref[:, sl], k_ref[0, :, sl], NT)
            m = jnp.max(s, axis=-1, keepdims=True)
            p = jnp.exp2(s - m)
            den = jnp.sum(p, axis=-1, keepdims=True)
            o = _dot(p.astype(BF16), v_ref[0, :, j * LANE:(j + 1) * LANE])
            outs.append(o / den)
        o_ref[:, j * LANE:(j + 1) * LANE] = jnp.where(lo, outs[0], outs[1])


def _row_block(n_rows, width, seq_len, row0):
    return pl.BlockSpec(
        (pl.Element(n_rows), pl.Element(width)),
        lambda b, t: (pl.multiple_of(b * seq_len + row0 + t * n_rows, TOK_TILE), 0))


def _attb_call(qb, kb, vb, row0, n_rows, tq, n_keys):
    B, S, _ = qb.shape
    width = B_HEADS * LANE
    tiles = n_rows // tq
    return pl.pallas_call(
        _attb_kernel,
        out_shape=jax.ShapeDtypeStruct((B * n_rows, BW), F32),
        grid=(B, tiles),
        in_specs=[
            _row_block(tq, width, S, row0),
            pl.BlockSpec((1, n_keys, width), lambda b, t: (b, 0, 0)),
            pl.BlockSpec((1, n_keys, BW), lambda b, t: (b, 0, 0)),
        ],
        out_specs=pl.BlockSpec((tq, BW), lambda b, t: (b * tiles + t, 0)),
        compiler_params=_params(("parallel", "arbitrary")),
        name="attb",
    )(qb.reshape(B * S, width), kb, vb).reshape(B, n_rows, BW)


def _attc_kernel(tile0, ctx_len, n_lat, sink_ref, q_ref, k_ref, v_ref, *rest):
    valid_refs, o_ref = rest[:-1], rest[-1]
    W = WINDOW
    ctx_tiles = ctx_len // W
    n_blk = n_lat // W
    lane = lax.broadcasted_iota(jnp.int32, (1, LANE), 1)
    lo = lane < C_HEAD
    rows4 = lax.broadcasted_iota(jnp.int32, (C_GROUP * W, 1), 0)

    def attend(rows, k_cat, v_cat, mask):
        for g in range(C_KV_HEADS):
            q_st = jnp.concatenate(
                [q_ref[0, rows, h * LANE:(h + 1) * LANE] for h in range(g * C_GROUP, (g + 1) * C_GROUP)],
                axis=0)
            s = _dot(q_st, k_cat, NT)
            if mask is not None:
                s = jnp.where(jnp.concatenate([mask] * C_GROUP, axis=0) > 0.5, s, NEG)
            sk = jnp.zeros((C_GROUP * W, 1), F32)
            for i in range(C_GROUP):
                sk = jnp.where(rows4 // W == i, sink_ref[g * C_GROUP + i], sk)
            m = jnp.maximum(jnp.max(s, axis=-1, keepdims=True), sk)
            p = jnp.exp2(s - m)
            den = jnp.sum(p, axis=-1, keepdims=True) + jnp.exp2(sk - m)
            o = _dot(p.astype(BF16), v_cat[:, g * LANE:(g + 1) * LANE]) / den
            for jj in range(C_GROUP // 2):
                o_lo = o[(2 * jj) * W:(2 * jj + 1) * W]
                o_hi = o[(2 * jj + 1) * W:(2 * jj + 2) * W]
                col = (g * C_GROUP // 2 + jj) * LANE
                o_ref[0, rows, col:col + LANE] = jnp.where(lo, o_lo, o_hi)

    def latent(i, rows):
        j = (pl.program_id(1) + tile0) * ATTC_BLOCKS + i - ctx_tiles
        bm = jnp.maximum(j - 1, 0)
        bp = jnp.minimum(j + 1, n_blk - 1)

        def blk(ref, b):
            return ref[0, pl.ds(pl.multiple_of(ctx_len + b * W, W), W), :]

        k_cat = jnp.concatenate([k_ref[0, 0:ctx_len, :], blk(k_ref, bm), blk(k_ref, j), blk(k_ref, bp)], axis=0)
        v_cat = jnp.concatenate([v_ref[0, 0:ctx_len, :], blk(v_ref, bm), blk(v_ref, j), blk(v_ref, bp)], axis=0)
        attend(rows, k_cat, v_cat, valid_refs[i][0])

    def context(rows):
        attend(rows, k_ref[0, 0:ctx_len, :], v_ref[0, 0:ctx_len, :], None)

    t = pl.program_id(1) + tile0
    ctx_steps = ctx_tiles // ATTC_BLOCKS
    blocks = [(i, slice(i * W, (i + 1) * W)) for i in range(ATTC_BLOCKS)]

    def all_context():
        for _, rows in blocks:
            context(rows)

    def all_latent():
        for i, rows in blocks:
            latent(i, rows)

    if tile0 < ctx_steps:
        pl.when(t < ctx_steps)(all_context)
        pl.when(t >= ctx_steps)(all_latent)
    else:
        all_latent()


def _attc_call(qc, kc, vc, sink, ctx_len, with_ctx):
    B, S, _ = qc.shape
    W = WINDOW
    nb = ATTC_BLOCKS
    ctx_steps = ctx_len // (nb * W)
    tile0 = 0 if with_ctx else ctx_steps
    n_steps = S // (nb * W) - tile0
    n_blk = (S - ctx_len) // W
    valid = _band_valid(ctx_len)

    def variant(t, i):
        j = (t + tile0) * nb + i - ctx_len // W
        return jnp.where(j == 0, 1, 0) + jnp.where(j == n_blk - 1, 2, 0)

    return pl.pallas_call(
        functools.partial(_attc_kernel, tile0, ctx_len, S - ctx_len),
        out_shape=jax.ShapeDtypeStruct((B, n_steps * nb * W, BW), F32),
        grid=(B, n_steps),
        in_specs=[
            pl.BlockSpec(memory_space=pltpu.SMEM),
            pl.BlockSpec((1, nb * W, C_HEADS * LANE), lambda b, t: (b, t + tile0, 0)),
            pl.BlockSpec((1, S, C_KV_W), lambda b, t: (b, 0, 0)),
            pl.BlockSpec((1, S, 2 * LANE), lambda b, t: (b, 0, 0)),
        ] + [pl.BlockSpec((1, W, ctx_len + 3 * W), lambda b, t, i=i: (variant(t, i), 0, 0)) for i in range(nb)],
        out_specs=pl.BlockSpec((1, nb * W, BW), lambda b, t: (b, t, 0)),
        compiler_params=_params(("parallel", "arbitrary")),
        name="attc",
    )(sink, qc, kc, vc, *([valid] * nb))


def _band_valid(ctx_len):
    W = WINDOW
    qi = np.arange(W)[:, None]
    cb = np.arange(ctx_len + 3 * W)[None, :] - ctx_len
    base = (cb < 0) | (np.abs(cb - W - qi) <= WINDOW)
    out = []
    for variant in range(4):
        first, last = bool(variant & 1), bool(variant & 2)
        ok = base.copy()
        if first:
            ok &= ~((cb >= 0) & (cb < W))
        if last:
            ok &= ~(cb >= 2 * W)
        out.append(ok)
    return jnp.asarray(np.stack(out), dtype=F32)


def _merge_kernel(tile0, ctx_tiles, y0_ref, y1_ref, bonus_ref, ybc_ref, ybl_ref, yc_ref, x_ref,
                  shift_ref, scale_ref, gate_ref,
                  g_ref, wg_ref, gng_ref, gnb_ref, ones_ref, wbo_ref, wout_ref, o_ref):
    x = x_ref[0]
    hb = _modulated_norm(x, g_ref[...], shift_ref[0], scale_ref[0]).astype(BF16)
    ones = ones_ref[...]
    y = y0_ref[0] + y1_ref[0]
    mu = _head_sums(y, ones) * (1.0 / A_HEAD)
    dlt = y - mu
    var = _head_sums(dlt * dlt, ones) * (1.0 / A_HEAD)
    ya = dlt * lax.rsqrt(var + A_GN_EPS) * gng_ref[...] + gnb_ref[...] + bonus_ref[0].astype(F32)
    m = None
    zg0 = N_BRANCH * BW
    yb = jnp.where(pl.program_id(1) + tile0 < ctx_tiles, ybc_ref[0], ybl_ref[0])
    for n, y_n in enumerate((ya, yb, yc_ref[0])):
        g = _dot(hb, wg_ref[:, n * BW:(n + 1) * BW])
        u = (y_n * _silu(g)).astype(BF16)
        zg = _dot(hb, wg_ref[:, zg0 + n * D_MODEL:zg0 + (n + 1) * D_MODEL])
        term = _sigmoid(zg) * _dot(u, wbo_ref[n])
        m = term if m is None else m + term
    o_ref[0] = x + gate_ref[0] * _dot(m.astype(BF16), wout_ref[...])


def _merge_call(y0, y1, bonus, yb_ctx, yb_lat, yc, xall, modl, norm_g, lp, ones_a, ctx_len, with_ctx):
    B, S, _ = xall.shape
    tt = TOK_TILE
    ctx_tiles = ctx_len // tt
    tile0 = 0 if with_ctx else ctx_tiles
    n_tiles = S // tt - tile0
    row = _mod_row(ctx_tiles, B)
    mod3 = modl.reshape(modl.shape[0], 1, 3 * D_MODEL)
    mod = lambda col: pl.BlockSpec((1, 1, D_MODEL), lambda b, t: (row(b, t + tile0), 0, col))
    once = lambda shape: pl.BlockSpec(shape, lambda b, t: (0,) * len(shape), pipeline_mode=pl.Buffered(1))
    tok = lambda w: pl.BlockSpec((1, tt, w), lambda b, t: (b, t + tile0, 0))
    att = lambda w: pl.BlockSpec((1, tt, w), lambda b, t: (b, t, 0))
    n_gate = sum(w for _, w in IN_SEGS_GATE)
    ybc = pl.BlockSpec((1, tt, BW), lambda b, t: (b, jnp.minimum(t + tile0, ctx_tiles - 1), 0))
    ybl = pl.BlockSpec((1, tt, BW), lambda b, t: (b, jnp.maximum(t + tile0 - ctx_tiles, 0), 0))
    return pl.pallas_call(
        functools.partial(_merge_kernel, tile0, ctx_tiles),
        out_shape=jax.ShapeDtypeStruct((B, n_tiles * tt, D_MODEL), F32),
        grid=(B, n_tiles),
        in_specs=[tok(BW), tok(BW), tok(BW), ybc, ybl, att(BW), tok(D_MODEL),
                  mod(0), mod(1), mod(2),
                  once((1, D_MODEL)), once((D_MODEL, n_gate)),
                  once((1, BW)), once((1, BW)), once((MXU_W, MXU_W)),
                  once((N_BRANCH, BW, D_MODEL)), once((D_MODEL, D_MODEL))],
        out_specs=pl.BlockSpec((1, tt, D_MODEL), lambda b, t: (b, t, 0)),
        compiler_params=_params(("parallel", "arbitrary")),
        name="merge",
    )(y0, y1, bonus, yb_ctx, yb_lat, yc, xall, mod3, mod3, mod3, norm_g.reshape(1, D_MODEL), lp["w_gate"],
      lp["gn_g"], lp["gn_b"], ones_a, lp["w_bo"], lp["w_out"])


def _rope_tables(n_ctx, n_lat, rot_dim, head_w, lane0):
    rows = n_lat // GRID_W
    row = jnp.repeat(jnp.arange(rows), GRID_W).astype(F32)
    colp = jnp.tile(jnp.arange(GRID_W), rows).astype(F32)
    q = rot_dim // 4
    inv = ROPE_THETA ** (-(2.0 * jnp.arange(q, dtype=F32)) / (rot_dim // 2))
    ang = jnp.concatenate([row[:, None] * inv, colp[:, None] * inv], axis=-1)
    cos, sin = jnp.cos(ang), jnp.sin(ang)
    zero = jnp.zeros_like(sin[:, :q])
    cos_g = jnp.concatenate([cos[:, :q], cos[:, :q], cos[:, q:], cos[:, q:]], axis=-1)
    sp_g = jnp.concatenate([zero, sin[:, :q], zero, sin[:, q:]], axis=-1)
    sm_g = jnp.concatenate([-sin[:, :q], zero, -sin[:, q:], zero], axis=-1)

    def place(t, fill):
        pad_l = jnp.full((n_lat, lane0), fill, F32)
        pad_r = jnp.full((n_lat, head_w - lane0 - rot_dim), fill, F32)
        grp = jnp.concatenate([pad_l, t, pad_r], axis=-1)
        lat = jnp.tile(grp, (1, LANE // head_w))
        return jnp.concatenate([jnp.full((n_ctx, LANE), fill, F32), lat], axis=0)

    return jnp.stack([place(cos_g, 1.0), place(sp_g, 0.0), place(sm_g, 0.0)])


def _block_ones(n, blk):
    i = np.arange(n) // blk
    return jnp.asarray(i[:, None] == i[None, :], dtype=BF16)


def _pad_lanes(g, width):
    return jnp.pad(g, (0, width - g.shape[0]))


def _fold_gain(tab, gain, shift, scale):
    g = jnp.stack([gain, jnp.roll(gain, shift), jnp.roll(gain, -shift)]) * scale
    return tab * g[:, None, :]


def _layer_params(i, tab_b, tab_c, w_in, a_mu_prev, a_mu_next, a_w0, a_w_up, a_a0, a_a_up, a_k_k, a_k_a, a_r_k,
                  a_gn_g, a_gn_b, b_q_ln, b_kv_ln, b_w_uq, b_w_ukv, b_qn_g, b_kn_g,
                  c_qn_g, c_kn_g, c_sink, w_branch_out, w_out):
    w = w_in[i].astype(BF16)
    c0 = 0
    segs = {}
    for name, width in (("za", A_COLS), ("ga", BW), ("zb", B_COLS), ("gb", BW),
                        ("zc", C_COLS), ("gc", BW), ("zg", N_BRANCH * D_MODEL)):
        segs[name] = w[:, c0:c0 + width]
        c0 += width
    zb = segs["zb"]
    zpad = lambda n: jnp.zeros((D_MODEL, n), w.dtype)
    segs["zb"] = jnp.concatenate(
        [zb[:, :B_Q_LORA + B_KV_LORA], zpad(B_NOPE), zb[:, B_Q_LORA + B_KV_LORA:], zpad(LANE - B_NOPE - B_ROPE)],
        axis=-1)
    w_mix = jnp.concatenate([segs[n] for n, _ in IN_SEGS_MIX], axis=-1).astype(BF16)
    w_gate = jnp.concatenate([segs[n] for n, _ in IN_SEGS_GATE], axis=-1).astype(BF16)

    uq = b_w_uq[i].reshape(B_Q_LORA, B_HEADS, B_QK)
    uq = jnp.pad(uq, ((0, 0), (0, 0), (0, LANE - B_QK))).reshape(B_Q_LORA, B_HEADS * LANE)
    ukv = b_w_ukv[i].reshape(B_KV_LORA, B_HEADS, B_NOPE + B_V)
    uk = jnp.pad(ukv[:, :, :B_NOPE], ((0, 0), (0, 0), (0, LANE - B_NOPE))).reshape(B_KV_LORA, B_HEADS * LANE)
    uv = ukv[:, :, B_NOPE:].reshape(B_KV_LORA, BW)
    row = lambda t: t.reshape(1, -1)
    return dict(
        w_mix=w_mix, w_gate=w_gate,
        mu_prev=row(a_mu_prev[i]), mu_next=row(a_mu_next[i]),
        w0=a_w0[i], w_up=a_w_up[i], a0=a_a0[i], a_up=a_a_up[i],
        k_k=row(a_k_k[i]), k_a=row(a_k_a[i]), r_k=row(a_r_k[i]),
        gn_g=row(a_gn_g[i]), gn_b=row(a_gn_b[i]),
        q_ln=row(b_q_ln[i]), kv_ln=row(b_kv_ln[i]),
        w_uq=uq.astype(BF16), w_uk=uk.astype(BF16), w_uv=uv.astype(BF16),
        tab_bq=_fold_gain(tab_b, _pad_lanes(b_qn_g[i], LANE), B_ROPE // 4, B_QK ** -0.5 * LOG2E),
        tab_bk=_fold_gain(tab_b, _pad_lanes(b_kn_g[i], LANE), B_ROPE // 4, 1.0),
        tab_cq=_fold_gain(tab_c, jnp.tile(c_qn_g[i], 2), C_HEAD // 4, C_HEAD ** -0.5 * LOG2E),
        tab_ck=_fold_gain(tab_c, jnp.tile(c_kn_g[i], 2), C_HEAD // 4, 1.0),
        sink=c_sink[i] * LOG2E,
        w_bo=w_branch_out[i].astype(BF16), w_out=w_out[i].astype(BF16),
    )


def kernel(x, c, ctx, c_ctx, ada_w, ada_b, norm_g, w_in, a_mu_prev, a_mu_next, a_w0, a_w_up, a_a0, a_a_up, a_k_k, a_k_a, a_r_k, a_gn_g, a_gn_b, b_q_ln, b_kv_ln, b_w_uq, b_w_ukv, b_qn_g, b_kn_g, c_qn_g, c_kn_g, c_sink, w_branch_out, w_out):
    B, T, D = x.shape
    L = ctx.shape[1]
    depth = ada_w.shape[0]
    assert D == D_MODEL and L % TOK_TILE == 0 and T % TOK_TILE == 0 and T % GRID_W == 0
    assert T % ATTB_TQ == 0 and B % SCAN_ROWS == 0 and L % (ATTC_BLOCKS * WINDOW) == 0

    mod_rows = -(-(B + 1) // 8) * 8
    cs = jnp.concatenate([c, c_ctx[None, :], jnp.zeros((mod_rows - B - 1, D), c.dtype)], axis=0)
    mod = _mod_call(cs, ada_w, ada_b)

    tab_b = _rope_tables(L, T, B_ROPE, LANE, B_NOPE)
    tab_c = _rope_tables(L, T, C_HEAD, C_HEAD, 0)
    ones_a = _block_ones(MXU_W, A_HEAD)

    xall = jnp.concatenate([ctx, x], axis=1)
    for i in range(depth):
        with_ctx = i < depth - 1
        lp = _layer_params(i, tab_b, tab_c, w_in, a_mu_prev, a_mu_next, a_w0, a_w_up, a_a0, a_a_up, a_k_k, a_k_a,
                           a_r_k, a_gn_g, a_gn_b, b_q_ln, b_kv_ln, b_w_uq, b_w_ukv, b_qn_g, b_kn_g,
                           c_qn_g, c_kn_g, c_sink, w_branch_out, w_out)
        zb, zc, rvk, dkn, lw, bonus = _inprep_call(xall, mod[i], norm_g[i], lp, ones_a, L)
        y0, y1 = _scan_call(rvk, dkn, lw, L)
        qb, kb, vb, qc, kc, vc = _bcproj_call(zb, zc, lp)
        yb_lat = _attb_call(qb, kb, vb, L, T, ATTB_TQ, L + T)
        yb_ctx = _attb_call(qb, kb, vb, 0, L, TOK_TILE, L) if with_ctx else yb_lat
        yc = _attc_call(qc, kc, vc, lp["sink"], L, with_ctx)
        xall = _merge_call(y0, y1, bonus, yb_ctx, yb_lat, yc, xall, mod[i], norm_g[i], lp, ones_a, L,
                           with_ctx)
    return xall
```

```python
import functools

import numpy as np
import jax
import jax.numpy as jnp
from jax import lax
from jax.experimental import pallas as pl
from jax.experimental.pallas import tpu as pltpu

F32 = jnp.float32
BF16 = jnp.bfloat16

D_MODEL = 1024
GRID_W = 64
ROPE_THETA = 10000.0
NORM_EPS = 1e-6
NEG = -1e30
BW = 512
N_BRANCH = 3

A_HEAD = 64
A_HEADS = BW // A_HEAD
A_LORA = 64
A_GN_EPS = 64e-5
A_COLS = 3 * BW + 4 * A_LORA

B_HEADS = 8
B_NOPE = 64
B_ROPE = 32
B_QK = B_NOPE + B_ROPE
B_V = BW // B_HEADS
B_Q_LORA = 256
B_KV_LORA = 128
B_COLS = B_Q_LORA + B_KV_LORA + B_ROPE
B_COLS_P = 512

C_HEAD = 64
C_HEADS = BW // C_HEAD
C_KV_HEADS = 2
C_GROUP = C_HEADS // C_KV_HEADS
C_KV_W = C_KV_HEADS * C_HEAD
C_COLS = BW + 2 * C_KV_W
WINDOW = 128

LANE = 128
MXU_W = 256
CHUNK = 64
SCAN_ROWS = 8
TOK_TILE = 256
V7X_VMEM_BYTES = 64 * 1024 * 1024
VMEM_LIMIT = V7X_VMEM_BYTES - 8 * 1024 * 1024

ATTB_TQ = 512
ATTC_BLOCKS = 2
LOG2E = 1.4426950408889634

IN_SEGS_MIX = (("za", A_COLS), ("zb", B_COLS_P), ("zc", C_COLS))
IN_SEGS_GATE = (("ga", BW), ("gb", BW), ("gc", BW), ("zg", N_BRANCH * D_MODEL))

NN = (((1,), (0,)), ((), ()))
NT = (((1,), (1,)), ((), ()))


def _dot(a, b, dims=NN):
    return lax.dot_general(a, b, dims, preferred_element_type=F32)


def _split(x):
    hi = x.astype(BF16)
    lo = (x - hi.astype(F32)).astype(BF16)
    return hi, lo


def _mm(a, b, dims=NN, passes=1):
    if passes == 1:
        return _dot(a.astype(BF16), b.astype(BF16), dims)
    ah, al = _split(a)
    bh, bl = _split(b)
    return _dot(ah, bh, dims) + (_dot(ah, bl, dims) + _dot(al, bh, dims))


def _mm_exact_rhs(a, b_bf16, parts=2):
    acc = None
    rem = a
    for _ in range(parts):
        p = rem.astype(BF16)
        t = _dot(p, b_bf16)
        acc = t if acc is None else acc + t
        rem = rem - p.astype(F32)
    return acc


def _head_sums(a, ones_blk):
    w = ones_blk.shape[0]
    return jnp.concatenate(
        [_mm_exact_rhs(a[:, c:c + w], ones_blk) for c in range(0, a.shape[1], w)], axis=1)


def _mm_exact_lhs(a_bf16, b, parts=3):
    acc = None
    rem = b
    for _ in range(parts):
        p = rem.astype(BF16)
        t = _dot(a_bf16, p)
        acc = t if acc is None else acc + t
        rem = rem - p.astype(F32)
    return acc


def _sigmoid(x):
    return 1.0 / (1.0 + jnp.exp(-x))


def _silu(x):
    return x * _sigmoid(x)


def _modulated_norm(x, g, shift, scale):
    ms = jnp.mean(x * x, axis=-1, keepdims=True)
    return (x * lax.rsqrt(ms + NORM_EPS) * g) * (1.0 + scale) + shift


def _params(sem):
    return pltpu.CompilerParams(dimension_semantics=sem, vmem_limit_bytes=VMEM_LIMIT)


def _mod_kernel(c_ref, w_ref, b_ref, o_ref):
    s = _silu(c_ref[...])
    o_ref[0] = _mm(s, w_ref[0], passes=3) + b_ref[0]


def _mod_call(cs, ada_w, ada_b):
    n_layers = ada_w.shape[0]
    rows = cs.shape[0]
    return pl.pallas_call(
        _mod_kernel,
        out_shape=jax.ShapeDtypeStruct((n_layers, rows, 3 * D_MODEL), F32),
        grid=(n_layers, 3),
        in_specs=[
            pl.BlockSpec((rows, D_MODEL), lambda l, n: (0, 0)),
            pl.BlockSpec((1, D_MODEL, D_MODEL), lambda l, n: (l, 0, n)),
            pl.BlockSpec((1, 1, D_MODEL), lambda l, n: (l, 0, n)),
        ],
        out_specs=pl.BlockSpec((1, rows, D_MODEL), lambda l, n: (l, 0, n)),
        compiler_params=_params(("arbitrary", "arbitrary")),
        name="mod",
    )(cs, ada_w, ada_b.reshape(n_layers, 1, 3 * D_MODEL))


def _mod_row(ctx_tiles, n_batch_rows):
    return lambda b, t: jnp.where(t < ctx_tiles, n_batch_rows, b)


def _inprep_kernel(ctx_tiles, n_tiles,
                   x_ref, xp_ref, xn_ref, shift_ref, scale_ref, g_ref, w_ref,
                   mup_ref, mun_ref, w0_ref, wup_ref, a0_ref, aup_ref, kk_ref, ka_ref, rk_ref, ones_ref,
                   zb_out, zc_out, rv_out, kk_out, dkn_out, lw_out, bonus_out, hb_ref, za_ref):
    t = pl.program_id(1)
    tt = TOK_TILE
    first = jnp.logical_or(t == 0, t == ctx_tiles)
    last = jnp.logical_or(t == ctx_tiles - 1, t == n_tiles - 1)
    g, shift, scale = g_ref[...], shift_ref[0], scale_ref[0]
    hb_ref[0:tt, :] = _modulated_norm(x_ref[0], g, shift, scale).astype(BF16)
    hb_ref[tt:tt + 8, :] = _modulated_norm(xp_ref[0], g, shift, scale).astype(BF16)
    hb_ref[tt + 8:tt + 16, :] = _modulated_norm(xn_ref[0], g, shift, scale).astype(BF16)
    hb = hb_ref[...]
    for c0 in range(0, A_COLS, MXU_W):
        za_ref[:, c0:c0 + MXU_W] = _dot(hb, w_ref[:, c0:c0 + MXU_W])
    hb_t = hb[0:tt]
    zb_out[0] = _dot(hb_t, w_ref[:, A_COLS:A_COLS + B_COLS_P]).astype(BF16)
    for c0 in range(0, C_COLS, MXU_W):
        col = A_COLS + B_COLS_P + c0
        zc_out[0, :, c0:c0 + MXU_W] = _dot(hb_t, w_ref[:, col:col + MXU_W]).astype(BF16)

    z = za_ref[0:tt, :]
    row = lax.broadcasted_iota(jnp.int32, (tt, 1), 0)
    halo_prev = jnp.where(first, 0.0, za_ref[tt + 7:tt + 8, :])
    halo_next = jnp.where(last, 0.0, za_ref[tt + 8:tt + 9, :])
    prev = jnp.where(row == 0, halo_prev, pltpu.roll(z, 1, 0))
    nxt = jnp.where(row == tt - 1, halo_next, pltpu.roll(z, tt - 1, 0))
    zsh = z + mup_ref[...] * (prev - z) + mun_ref[...] * (nxt - z)

    r = zsh[:, 0:BW]
    k = zsh[:, BW:2 * BW]
    v = zsh[:, 2 * BW:3 * BW]
    ones = ones_ref[...]
    kk = k * kk_ref[...]
    ss = _head_sums(kk * kk, ones)
    kk = kk * lax.rsqrt(jnp.maximum(ss, 1e-24))
    rv_out[0, :, 0:BW] = r.astype(BF16)
    rv_out[0, :, BW:2 * BW] = v.astype(BF16)
    kk_out[0] = kk
    kd_sum = None
    for d in range(2):
        wd = zsh[:, 3 * BW + d * A_LORA:3 * BW + (d + 1) * A_LORA]
        ad = zsh[:, 3 * BW + 2 * A_LORA + d * A_LORA:3 * BW + 2 * A_LORA + (d + 1) * A_LORA]
        u = -(w0_ref[d:d + 1, :] + _mm(jnp.tanh(wd), wup_ref[d], passes=3))
        softplus = jnp.maximum(u, 0.0) + jnp.log(1.0 + jnp.exp(-jnp.abs(u)))
        w_log = -softplus - 0.5
        lw = -jnp.exp(w_log)
        a = _sigmoid(a0_ref[d:d + 1, :] + _mm(ad, aup_ref[d], passes=3))
        kd = k * (1.0 + (a - 1.0) * ka_ref[...])
        lw_out[d, 0] = lw
        dkn_out[d, 0, :, 0:BW] = kd
        dkn_out[d, 0, :, BW:2 * BW] = -(kk * a)
        kd_sum = kd if kd_sum is None else kd_sum + kd
    bonus_out[0] = (_head_sums(r * kd_sum * rk_ref[...], ones) * v).astype(BF16)


def _inprep_call(xall, modl, norm_g, lp, ones_a, ctx_len):
    B, S, _ = xall.shape
    tt = TOK_TILE
    n_tiles = S // tt
    hb = tt // 8
    n8 = S // 8
    row = _mod_row(ctx_len // tt, B)
    mod3 = modl.reshape(modl.shape[0], 1, 3 * D_MODEL)
    mod = lambda col: pl.BlockSpec((1, 1, D_MODEL), lambda b, t: (row(b, t), 0, col))
    once = lambda shape: pl.BlockSpec(shape, lambda b, t: (0,) * len(shape), pipeline_mode=pl.Buffered(1))
    s3 = lambda w: pl.BlockSpec((1, tt, w), lambda b, t: (b, t, 0))
    s4 = lambda w: pl.BlockSpec((2, 1, tt, w), lambda b, t: (0, b, t, 0))
    o3 = lambda w, dt: jax.ShapeDtypeStruct((B, S, w), dt)
    o4 = lambda w, dt: jax.ShapeDtypeStruct((2, B, S, w), dt)
    n_mix = sum(w for _, w in IN_SEGS_MIX)
    return pl.pallas_call(
        functools.partial(_inprep_kernel, ctx_len // tt, n_tiles),
        out_shape=(o3(B_COLS_P, BF16), o3(C_COLS, BF16), o3(2 * BW, BF16), o3(BW, F32), o4(2 * BW, F32),
                   o4(BW, F32), o3(BW, BF16)),
        grid=(B, n_tiles),
        in_specs=[
            s3(D_MODEL),
            pl.BlockSpec((1, 8, D_MODEL), lambda b, t: (b, jnp.maximum(t * hb - 1, 0), 0)),
            pl.BlockSpec((1, 8, D_MODEL), lambda b, t: (b, jnp.minimum((t + 1) * hb, n8 - 1), 0)),
            mod(0), mod(1), once((1, D_MODEL)), once((D_MODEL, n_mix)),
            once((1, A_COLS)), once((1, A_COLS)),
            once((2, BW)), once((2, A_LORA, BW)), once((2, BW)), once((2, A_LORA, BW)),
            once((1, BW)), once((1, BW)), once((1, BW)), once((MXU_W, MXU_W)),
        ],
        out_specs=(s3(B_COLS_P), s3(C_COLS), s3(2 * BW), s3(BW), s4(2 * BW), s4(BW), s3(BW)),
        scratch_shapes=[pltpu.VMEM((tt + 16, D_MODEL), BF16), pltpu.VMEM((tt + 16, A_COLS), F32)],
        compiler_params=_params(("parallel", "arbitrary")),
        name="inprep",
    )(xall, xall, xall, mod3, mod3, norm_g.reshape(1, D_MODEL), lp["w_mix"],
      lp["mu_prev"], lp["mu_next"], lp["w0"], lp["w_up"], lp["a0"], lp["a_up"],
      lp["k_k"], lp["k_a"], lp["r_k"], ones_a)


def _scan_kernel(f_rv, f_kk, f_dkn, f_lw, b_rv, b_kk, b_dkn, b_lw, yf_ref, yb_ref, state_ref):
    C = CHUNK

    @pl.when(pl.program_id(1) == 0)
    def _():
        state_ref[...] = jnp.zeros_like(state_ref)

    lane = lax.broadcasted_iota(jnp.int32, (1, LANE), 1)
    lo = lane < A_HEAD
    row2 = lax.broadcasted_iota(jnp.int32, (2 * C, 2 * C), 0)
    col2 = lax.broadcasted_iota(jnp.int32, (2 * C, 2 * C), 1)
    rows = lax.broadcasted_iota(jnp.int32, (C, C), 0)
    cols = lax.broadcasted_iota(jnp.int32, (C, C), 1)
    n_pairs = A_HEADS // 2

    n_rows = f_rv.shape[0]
    dirs = []
    for row, d in [(row, d) for row in range(n_rows) for d in range(2)]:
        rv_ref, kk_ref, dkn_ref, lw_ref = (f_rv, f_kk, f_dkn, f_lw) if d == 0 else (b_rv, b_kk, b_dkn, b_lw)
        order = (rows - cols) if d == 0 else (cols - rows)
        lw = lw_ref[0, row]
        kd = dkn_ref[0, row, :, 0:BW]
        nb = dkn_ref[0, row, :, BW:2 * BW]
        cum = _mm_exact_lhs((order >= 0).astype(BF16), lw)
        tot = cum[C - 1:C, :] if d == 0 else cum[0:1, :]
        p_inv = jnp.exp(-cum)
        p_tot = jnp.exp(tot - cum)
        kt = kk_ref[row] * jnp.exp(cum - lw)
        order2 = (row2 % C - col2 % C) if d == 0 else (col2 % C - row2 % C)
        keep = order2 >= jnp.where(row2 < C, 0, 1)
        e_rows = jnp.broadcast_to(jnp.exp(tot), (LANE, BW))
        dirs.append(dict(
            keep=keep, kt=kt, v=rv_ref[row, :, BW:2 * BW].astype(F32), fwd=(d == 0),
            y_ref=yf_ref if d == 0 else yb_ref, row=row,
            lhs=jnp.concatenate([rv_ref[row, :, 0:BW].astype(F32) * jnp.exp(cum), kt], axis=0),
            rhs_e=jnp.concatenate([nb * p_inv, kd * p_inv], axis=0),
            rhs_o=jnp.concatenate([kd * p_inv, nb * p_inv], axis=0),
            nkt=jnp.concatenate([nb * p_tot, kd * p_tot, e_rows], axis=0).T))

    units = [(d, h) for d in range(len(dirs)) for h in range(A_HEADS)]
    slab = lambda h: slice((h // 2) * LANE, (h // 2 + 1) * LANE)
    own = lambda h: lo if h % 2 == 0 else jnp.logical_not(lo)

    lhs_m, top, bot = [], [], []
    for d, h in units:
        x = dirs[d]
        lm = jnp.where(own(h), x["lhs"][:, slab(h)], 0.0)
        rhs = x["rhs_o" if h % 2 else "rhs_e"][:, slab(h)]
        qk = jnp.where(x["keep"], _mm(lm, rhs, NT), 0.0)
        lhs_m.append(lm)
        top.append(qk[0:C])
        bot.append(qk[C:2 * C])
    pairs = [(d, j) for d in range(len(dirs)) for j in range(n_pairs)]
    r_i = lax.broadcasted_iota(jnp.int32, (C, 2 * C), 0)
    c_i = lax.broadcasted_iota(jnp.int32, (C, 2 * C), 1) % C
    eye2 = (r_i == c_i).astype(F32)

    def couple(b, fwd):
        late, early = (r_i, c_i) if fwd else (c_i, r_i)
        return (r_i // (2 * b) == c_i // (2 * b)) & ((late // b) % 2 == 1) & ((early // b) % 2 == 0)

    bd = lambda m: jnp.concatenate([jnp.where(lo, m, 0.0), jnp.where(lo, 0.0, m)], axis=0)
    n_pair = [jnp.where(lo, bot[2 * p], bot[2 * p + 1]) for p in range(len(pairs))]
    fwd_of = [dirs[d]["fwd"] for d, _ in pairs]
    t_inv = [eye2 + jnp.where(couple(1, f), n, 0.0) for n, f in zip(n_pair, fwd_of)]
    b = 2
    while b < C:
        et = [_mm(jnp.where(couple(b, f), n, 0.0), bd(t), NN)
              for n, t, f in zip(n_pair, t_inv, fwd_of)]
        t_inv = [t + _mm(t, bd(m), NN) for t, m in zip(t_inv, et)]
        b *= 2
    wta = []
    for i, (d, h) in enumerate(units):
        t = t_inv[i // 2]
        kt_s = dirs[d]["kt"][:, slab(h)]
        if h % 2 == 0:
            wta.append(_mm(t[:, 0:C], jnp.where(lo, kt_s, bot[i]), NN))
        else:
            rhs = jnp.where(lo, bot[i], kt_s)
            wta.append(_mm(jnp.where(lo, 0.0, t), jnp.concatenate([jnp.zeros_like(rhs), rhs], axis=0),
                           NN))
    st = [state_ref[i] for i in range(len(pairs))]
    vp = [dirs[d]["v"][:, j * LANE:(j + 1) * LANE] for d, j in pairs]
    sv = [(jnp.concatenate([s, v], axis=0), jnp.concatenate([v, s], axis=0)) for s, v in zip(st, vp)]
    qt_u = [_mm(m, sv[i // 2][i % 2], NN) for i, m in enumerate(wta)]
    qt = [jnp.where(lo, qt_u[2 * i], qt_u[2 * i + 1]) for i in range(len(pairs))]
    qv = [(jnp.concatenate([q, v], axis=0), jnp.concatenate([v, q], axis=0)) for q, v in zip(qt, vp)]
    y_u = []
    for i, (d, h) in enumerate(units):
        p = i // 2
        y_l = jnp.concatenate([top[i], lhs_m[i][0:C]], axis=1)
        y_r = jnp.concatenate([qv[p][h % 2], st[p], st[p]], axis=0)
        y_u.append(_mm(y_l, y_r, NN))
    for p, (d, j) in enumerate(pairs):
        sl = slice(j * LANE, (j + 1) * LANE)
        dirs[d]["y_ref"][dirs[d]["row"], :, sl] = jnp.where(lo, y_u[2 * p], y_u[2 * p + 1])
        nk = dirs[d]["nkt"][sl]
        upd = _mm(nk[:, 0:2 * C], qv[p][0], NN)
        dec = nk[:, 2 * C:] * jnp.concatenate([st[p], st[p]], axis=0)
        state_ref[p] = jnp.where(lo, (dec + upd)[0:A_HEAD], (dec + upd)[A_HEAD:])


def _scan_call(rv, kk, dkn, lw, ctx_len):
    B, S, _ = kk.shape
    C = CHUNK
    R = SCAN_ROWS
    nc = S // C
    lc = ctx_len // C
    rev = lambda c: jnp.where(c < lc, lc - 1 - c, nc - 1 - (c - lc))
    f3 = lambda w: pl.BlockSpec((R, C, w), lambda b, c: (b, c, 0))
    b3 = lambda w: pl.BlockSpec((R, C, w), lambda b, c: (b, rev(c), 0))
    f4 = lambda w: pl.BlockSpec((1, R, C, w), lambda b, c: (0, b, c, 0))
    b4 = lambda w: pl.BlockSpec((1, R, C, w), lambda b, c: (1, b, rev(c), 0))
    out = jax.ShapeDtypeStruct((B, S, BW), F32)
    return pl.pallas_call(
        _scan_kernel,
        out_shape=(out, out),
        grid=(B // R, nc),
        in_specs=[f3(2 * BW), f3(BW), f4(2 * BW), f4(BW), b3(2 * BW), b3(BW), b4(2 * BW), b4(BW)],
        out_specs=(f3(BW), b3(BW)),
        scratch_shapes=[pltpu.VMEM((R * A_HEADS, A_HEAD, LANE), F32)],
        compiler_params=_params(("parallel", "arbitrary")),
        name="scan",
    )(rv, kk, dkn, lw, rv, kk, dkn, lw)


def _rope(x, tab_ref, shift):
    return (x * tab_ref[0] + pltpu.roll(x, shift, 1) * tab_ref[1]
            + pltpu.roll(x, LANE - shift, 1) * tab_ref[2])


def _bcproj_kernel(zb_ref, zc_ref, tbq_ref, tbk_ref, tcq_ref, tck_ref, qln_ref, kvln_ref,
                   wuq_ref, wuk_ref, wuv_ref, onesb_ref, onesc_ref,
                   qb_out, kb_out, vb_out, qc_out, kc_out, vc_out):
    ones_b = onesb_ref[...]
    ones_c = onesc_ref[...]
    zb = zb_ref[0].astype(F32)
    cq = zb[:, 0:B_Q_LORA]
    ckv = zb[:, B_Q_LORA:B_Q_LORA + B_KV_LORA]
    kr_slab = zb[:, B_Q_LORA + B_KV_LORA:B_COLS_P]
    cqn = cq * lax.rsqrt(jnp.mean(cq * cq, axis=-1, keepdims=True) + NORM_EPS) * qln_ref[...]
    ckvn = ckv * lax.rsqrt(jnp.mean(ckv * ckv, axis=-1, keepdims=True) + NORM_EPS) * kvln_ref[...]
    q_all = _dot(cqn.astype(BF16), wuq_ref[...])
    k_all = _dot(ckvn.astype(BF16), wuk_ref[...])
    vb_out[0] = _dot(ckvn.astype(BF16), wuv_ref[...]).astype(BF16)
    for h in range(B_HEADS):
        sl = slice(h * LANE, (h + 1) * LANE)
        q = q_all[:, sl]
        rs = lax.rsqrt(_mm_exact_rhs(q * q, ones_b) * (1.0 / B_QK) + NORM_EPS)
        qb_out[0, :, sl] = (_rope(q, tbq_ref, B_ROPE // 4) * rs).astype(BF16)
        k = k_all[:, sl] + kr_slab
        rs = lax.rsqrt(_mm_exact_rhs(k * k, ones_b) * (1.0 / B_QK) + NORM_EPS)
        kb_out[0, :, sl] = (_rope(k, tbk_ref, B_ROPE // 4) * rs).astype(BF16)

    zc = zc_ref[0].astype(F32)
    lane = lax.broadcasted_iota(jnp.int32, (1, LANE), 1)
    lo = lane < C_HEAD

    def head_norm(x, tab_ref):
        ms = _mm_exact_rhs(x * x, ones_c) * (1.0 / C_HEAD)
        return _rope(x, tab_ref, C_HEAD // 4) * lax.rsqrt(ms + NORM_EPS)

    for j in range(C_HEADS // 2):
        x = head_norm(zc[:, j * LANE:(j + 1) * LANE], tcq_ref)
        xr = pltpu.roll(x, C_HEAD, 1)
        g = (2 * j) // C_GROUP
        for half in range(2):
            h = 2 * j + half
            src = x if half == g else xr
            keep = lo if g == 0 else jnp.logical_not(lo)
            qc_out[0, :, h * LANE:(h + 1) * LANE] = jnp.where(keep, src, 0.0).astype(BF16)
    kc_out[0] = head_norm(zc[:, BW:BW + C_KV_W], tck_ref).astype(BF16)
    vv = zc[:, BW + C_KV_W:BW + 2 * C_KV_W]
    vr = pltpu.roll(vv, C_HEAD, 1)
    vc_out[0, :, 0:LANE] = jnp.where(lo, vv, vr).astype(BF16)
    vc_out[0, :, LANE:2 * LANE] = jnp.where(lo, vr, vv).astype(BF16)


def _bcproj_call(zb, zc, lp):
    B, S, _ = zb.shape
    tt = TOK_TILE
    full = lambda shape: pl.BlockSpec(shape, lambda t, b: (0,) * len(shape))
    tok = lambda w: pl.BlockSpec((1, tt, w), lambda t, b: (b, t, 0))
    tab = pl.BlockSpec((3, tt, LANE), lambda t, b: (0, t, 0))
    shp = lambda w: jax.ShapeDtypeStruct((B, S, w), BF16)
    return pl.pallas_call(
        _bcproj_kernel,
        out_shape=(shp(B_HEADS * LANE), shp(B_HEADS * LANE), shp(BW),
                   shp(C_HEADS * LANE), shp(C_KV_W), shp(2 * LANE)),
        grid=(S // tt, B),
        in_specs=[tok(B_COLS_P), tok(C_COLS), tab, tab, tab, tab,
                  full((1, B_Q_LORA)), full((1, B_KV_LORA)),
                  full((B_Q_LORA, B_HEADS * LANE)), full((B_KV_LORA, B_HEADS * LANE)),
                  full((B_KV_LORA, BW)), full((LANE, LANE)), full((LANE, LANE))],
        out_specs=(tok(B_HEADS * LANE), tok(B_HEADS * LANE), tok(BW),
                   tok(C_HEADS * LANE), tok(C_KV_W), tok(2 * LANE)),
        compiler_params=_params(("parallel", "arbitrary")),
        name="bcproj",
    )(zb, zc, lp["tab_bq"], lp["tab_bk"], lp["tab_cq"], lp["tab_ck"],
      lp["q_ln"], lp["kv_ln"], lp["w_uq"], lp["w_uk"], lp["w_uv"],
      _block_ones(LANE, LANE), _block_ones(LANE, C_HEAD))


def _attb_kernel(q_ref, k_ref, v_ref, o_ref):
    lane = lax.broadcasted_iota(jnp.int32, (1, LANE), 1)
    lo = lane < B_V
    for j in range(B_HEADS // 2):
        outs = []
        for h in (2 * j, 2 * j + 1):
            sl = slice(h * LANE, (h + 1) * LANE)
            s = _dot(q_ref[:, sl], k_ref[0, :, sl], NT)
            m = jnp.max(s, axis=-1, keepdims=True)
            p = jnp.exp2(s - m)
            den = jnp.sum(p, axis=-1, keepdims=True)
            o = _dot(p.astype(BF16), v_ref[0, :, j * LANE:(j + 1) * LANE])
            outs.append(o / den)
        o_ref[:, j * LANE:(j + 1) * LANE] = jnp.where(lo, outs[0], outs[1])


def _row_block(n_rows, width, seq_len, row0):
    return pl.BlockSpec(
        (pl.Element(n_rows), pl.Element(width)),
        lambda b, t: (pl.multiple_of(b * seq_len + row0 + t * n_rows, TOK_TILE), 0))


def _attb_call(qb, kb, vb, row0, n_rows, tq, n_keys):
    B, S, _ = qb.shape
    width = B_HEADS * LANE
    tiles = n_rows // tq
    return pl.pallas_call(
        _attb_kernel,
        out_shape=jax.ShapeDtypeStruct((B * n_rows, BW), F32),
        grid=(B, tiles),
        in_specs=[
            _row_block(tq, width, S, row0),
            pl.BlockSpec((1, n_keys, width), lambda b, t: (b, 0, 0)),
            pl.BlockSpec((1, n_keys, BW), lambda b, t: (b, 0, 0)),
        ],
        out_specs=pl.BlockSpec((tq, BW), lambda b, t: (b * tiles + t, 0)),
        compiler_params=_params(("parallel", "arbitrary")),
        name="attb",
    )(qb.reshape(B * S, width), kb, vb).reshape(B, n_rows, BW)


def _attc_kernel(tile0, ctx_len, n_lat, sink_ref, q_ref, k_ref, v_ref, *rest):
    valid_refs, o_ref = rest[:-1], rest[-1]
    W = WINDOW
    ctx_tiles = ctx_len // W
    n_blk = n_lat // W
    lane = lax.broadcasted_iota(jnp.int32, (1, LANE), 1)
    lo = lane < C_HEAD
    rows4 = lax.broadcasted_iota(jnp.int32, (C_GROUP * W, 1), 0)

    def attend(rows, k_cat, v_cat, mask):
        for g in range(C_KV_HEADS):
            q_st = jnp.concatenate(
                [q_ref[0, rows, h * LANE:(h + 1) * LANE] for h in range(g * C_GROUP, (g + 1) * C_GROUP)],
                axis=0)
            s = _dot(q_st, k_cat, NT)
            if mask is not None:
                s = jnp.where(jnp.concatenate([mask] * C_GROUP, axis=0) > 0.5, s, NEG)
            sk = jnp.zeros((C_GROUP * W, 1), F32)
            for i in range(C_GROUP):
                sk = jnp.where(rows4 // W == i, sink_ref[g * C_GROUP + i], sk)
            m = jnp.maximum(jnp.max(s, axis=-1, keepdims=True), sk)
            p = jnp.exp2(s - m)
            den = jnp.sum(p, axis=-1, keepdims=True) + jnp.exp2(sk - m)
            o = _dot(p.astype(BF16), v_cat[:, g * LANE:(g + 1) * LANE]) / den
            for jj in range(C_GROUP // 2):
                o_lo = o[(2 * jj) * W:(2 * jj + 1) * W]
                o_hi = o[(2 * jj + 1) * W:(2 * jj + 2) * W]
                col = (g * C_GROUP // 2 + jj) * LANE
                o_ref[0, rows, col:col + LANE] = jnp.where(lo, o_lo, o_hi)

    def latent(i, rows):
        j = (pl.program_id(1) + tile0) * ATTC_BLOCKS + i - ctx_tiles
        bm = jnp.maximum(j - 1, 0)
        bp = jnp.minimum(j + 1, n_blk - 1)

        def blk(ref, b):
            return ref[0, pl.ds(pl.multiple_of(ctx_len + b * W, W), W), :]

        k_cat = jnp.concatenate([k_ref[0, 0:ctx_len, :], blk(k_ref, bm), blk(k_ref, j), blk(k_ref, bp)], axis=0)
        v_cat = jnp.concatenate([v_ref[0, 0:ctx_len, :], blk(v_ref, bm), blk(v_ref, j), blk(v_ref, bp)], axis=0)
        attend(rows, k_cat, v_cat, valid_refs[i][0])

    def context(rows):
        attend(rows, k_ref[0, 0:ctx_len, :], v_ref[0, 0:ctx_len, :], None)

    t = pl.program_id(1) + tile0
    ctx_steps = ctx_tiles // ATTC_BLOCKS
    blocks = [(i, slice(i * W, (i + 1) * W)) for i in range(ATTC_BLOCKS)]

    def all_context():
        for _, rows in blocks:
            context(rows)

    def all_latent():
        for i, rows in blocks:
            latent(i, rows)

    if tile0 < ctx_steps:
        pl.when(t < ctx_steps)(all_context)
        pl.when(t >= ctx_steps)(all_latent)
    else:
        all_latent()


def _attc_call(qc, kc, vc, sink, ctx_len, with_ctx):
    B, S, _ = qc.shape
    W = WINDOW
    nb = ATTC_BLOCKS
    ctx_steps = ctx_len // (nb * W)
    tile0 = 0 if with_ctx else ctx_steps
    n_steps = S // (nb * W) - tile0
    n_blk = (S - ctx_len) // W
    valid = _band_valid(ctx_len)

    def variant(t, i):
        j = (t + tile0) * nb + i - ctx_len // W
        return jnp.where(j == 0, 1, 0) + jnp.where(j == n_blk - 1, 2, 0)

    return pl.pallas_call(
        functools.partial(_attc_kernel, tile0, ctx_len, S - ctx_len),
        out_shape=jax.ShapeDtypeStruct((B, n_steps * nb * W, BW), F32),
        grid=(B, n_steps),
        in_specs=[
            pl.BlockSpec(memory_space=pltpu.SMEM),
            pl.BlockSpec((1, nb * W, C_HEADS * LANE), lambda b, t: (b, t + tile0, 0)),
            pl.BlockSpec((1, S, C_KV_W), lambda b, t: (b, 0, 0)),
            pl.BlockSpec((1, S, 2 * LANE), lambda b, t: (b, 0, 0)),
        ] + [pl.BlockSpec((1, W, ctx_len + 3 * W), lambda b, t, i=i: (variant(t, i), 0, 0)) for i in range(nb)],
        out_specs=pl.BlockSpec((1, nb * W, BW), lambda b, t: (b, t, 0)),
        compiler_params=_params(("parallel", "arbitrary")),
        name="attc",
    )(sink, qc, kc, vc, *([valid] * nb))


def _band_valid(ctx_len):
    W = WINDOW
    qi = np.arange(W)[:, None]
    cb = np.arange(ctx_len + 3 * W)[None, :] - ctx_len
    base = (cb < 0) | (np.abs(cb - W - qi) <= WINDOW)
    out = []
    for variant in range(4):
        first, last = bool(variant & 1), bool(variant & 2)
        ok = base.copy()
        if first:
            ok &= ~((cb >= 0) & (cb < W))
        if last:
            ok &= ~(cb >= 2 * W)
        out.append(ok)
    return jnp.asarray(np.stack(out), dtype=F32)


def _merge_kernel(tile0, ctx_tiles, y0_ref, y1_ref, bonus_ref, ybc_ref, ybl_ref, yc_ref, x_ref,
                  shift_ref, scale_ref, gate_ref,
                  g_ref, wg_ref, gng_ref, gnb_ref, ones_ref, wbo_ref, wout_ref, o_ref):
    x = x_ref[0]
    hb = _modulated_norm(x, g_ref[...], shift_ref[0], scale_ref[0]).astype(BF16)
    ones = ones_ref[...]
    y = y0_ref[0] + y1_ref[0]
    mu = _head_sums(y, ones) * (1.0 / A_HEAD)
    dlt = y - mu
    var = _head_sums(dlt * dlt, ones) * (1.0 / A_HEAD)
    ya = dlt * lax.rsqrt(var + A_GN_EPS) * gng_ref[...] + gnb_ref[...] + bonus_ref[0].astype(F32)
    m = None
    zg0 = N_BRANCH * BW
    yb = jnp.where(pl.program_id(1) + tile0 < ctx_tiles, ybc_ref[0], ybl_ref[0])
    for n, y_n in enumerate((ya, yb, yc_ref[0])):
        g = _dot(hb, wg_ref[:, n * BW:(n + 1) * BW])
        u = (y_n * _silu(g)).astype(BF16)
        zg = _dot(hb, wg_ref[:, zg0 + n * D_MODEL:zg0 + (n + 1) * D_MODEL])
        term = _sigmoid(zg) * _dot(u, wbo_ref[n])
        m = term if m is None else m + term
    o_ref[0] = x + gate_ref[0] * _dot(m.astype(BF16), wout_ref[...])


def _merge_call(y0, y1, bonus, yb_ctx, yb_lat, yc, xall, modl, norm_g, lp, ones_a, ctx_len, with_ctx):
    B, S, _ = xall.shape
    tt = TOK_TILE
    ctx_tiles = ctx_len // tt
    tile0 = 0 if with_ctx else ctx_tiles
    n_tiles = S // tt - tile0
    row = _mod_row(ctx_tiles, B)
    mod3 = modl.reshape(modl.shape[0], 1, 3 * D_MODEL)
    mod = lambda col: pl.BlockSpec((1, 1, D_MODEL), lambda b, t: (row(b, t + tile0), 0, col))
    once = lambda shape: pl.BlockSpec(shape, lambda b, t: (0,) * len(shape), pipeline_mode=pl.Buffered(1))
    tok = lambda w: pl.BlockSpec((1, tt, w), lambda b, t: (b, t + tile0, 0))
    att = lambda w: pl.BlockSpec((1, tt, w), lambda b, t: (b, t, 0))
    n_gate = sum(w for _, w in IN_SEGS_GATE)
    ybc = pl.BlockSpec((1, tt, BW), lambda b, t: (b, jnp.minimum(t + tile0, ctx_tiles - 1), 0))
    ybl = pl.BlockSpec((1, tt, BW), lambda b, t: (b, jnp.maximum(t + tile0 - ctx_tiles, 0), 0))
    return pl.pallas_call(
        functools.partial(_merge_kernel, tile0, ctx_tiles),
        out_shape=jax.ShapeDtypeStruct((B, n_tiles * tt, D_MODEL), F32),
        grid=(B, n_tiles),
        in_specs=[tok(BW), tok(BW), tok(BW), ybc, ybl, att(BW), tok(D_MODEL),
                  mod(0), mod(1), mod(2),
                  once((1, D_MODEL)), once((D_MODEL, n_gate)),
                  once((1, BW)), once((1, BW)), once((MXU_W, MXU_W)),
                  once((N_BRANCH, BW, D_MODEL)), once((D_MODEL, D_MODEL))],
        out_specs=pl.BlockSpec((1, tt, D_MODEL), lambda b, t: (b, t, 0)),
        compiler_params=_params(("parallel", "arbitrary")),
        name="merge",
    )(y0, y1, bonus, yb_ctx, yb_lat, yc, xall, mod3, mod3, mod3, norm_g.reshape(1, D_MODEL), lp["w_gate"],
      lp["gn_g"], lp["gn_b"], ones_a, lp["w_bo"], lp["w_out"])


def _rope_tables(n_ctx, n_lat, rot_dim, head_w, lane0):
    rows = n_lat // GRID_W
    row = jnp.repeat(jnp.arange(rows), GRID_W).astype(F32)
    colp = jnp.tile(jnp.arange(GRID_W), rows).astype(F32)
    q = rot_dim // 4
    inv = ROPE_THETA ** (-(2.0 * jnp.arange(q, dtype=F32)) / (rot_dim // 2))
    ang = jnp.concatenate([row[:, None] * inv, colp[:, None] * inv], axis=-1)
    cos, sin = jnp.cos(ang), jnp.sin(ang)
    zero = jnp.zeros_like(sin[:, :q])
    cos_g = jnp.concatenate([cos[:, :q], cos[:, :q], cos[:, q:], cos[:, q:]], axis=-1)
    sp_g = jnp.concatenate([zero, sin[:, :q], zero, sin[:, q:]], axis=-1)
    sm_g = jnp.concatenate([-sin[:, :q], zero, -sin[:, q:], zero], axis=-1)

    def place(t, fill):
        pad_l = jnp.full((n_lat, lane0), fill, F32)
        pad_r = jnp.full((n_lat, head_w - lane0 - rot_dim), fill, F32)
        grp = jnp.concatenate([pad_l, t, pad_r], axis=-1)
        lat = jnp.tile(grp, (1, LANE // head_w))
        return jnp.concatenate([jnp.full((n_ctx, LANE), fill, F32), lat], axis=0)

    return jnp.stack([place(cos_g, 1.0), place(sp_g, 0.0), place(sm_g, 0.0)])


def _block_ones(n, blk):
    i = np.arange(n) // blk
    return jnp.asarray(i[:, None] == i[None, :], dtype=BF16)


def _pad_lanes(g, width):
    return jnp.pad(g, (0, width - g.shape[0]))


def _fold_gain(tab, gain, shift, scale):
    g = jnp.stack([gain, jnp.roll(gain, shift), jnp.roll(gain, -shift)]) * scale
    return tab * g[:, None, :]


def _layer_params(i, tab_b, tab_c, w_in, a_mu_prev, a_mu_next, a_w0, a_w_up, a_a0, a_a_up, a_k_k, a_k_a, a_r_k,
                  a_gn_g, a_gn_b, b_q_ln, b_kv_ln, b_w_uq, b_w_ukv, b_qn_g, b_kn_g,
                  c_qn_g, c_kn_g, c_sink, w_branch_out, w_out):
    w = w_in[i].astype(BF16)
    c0 = 0
    segs = {}
    for name, width in (("za", A_COLS), ("ga", BW), ("zb", B_COLS), ("gb", BW),
                        ("zc", C_COLS), ("gc", BW), ("zg", N_BRANCH * D_MODEL)):
        segs[name] = w[:, c0:c0 + width]
        c0 += width
    zb = segs["zb"]
    zpad = lambda n: jnp.zeros((D_MODEL, n), w.dtype)
    segs["zb"] = jnp.concatenate(
        [zb[:, :B_Q_LORA + B_KV_LORA], zpad(B_NOPE), zb[:, B_Q_LORA + B_KV_LORA:], zpad(LANE - B_NOPE - B_ROPE)],
        axis=-1)
    w_mix = jnp.concatenate([segs[n] for n, _ in IN_SEGS_MIX], axis=-1).astype(BF16)
    w_gate = jnp.concatenate([segs[n] for n, _ in IN_SEGS_GATE], axis=-1).astype(BF16)

    uq = b_w_uq[i].reshape(B_Q_LORA, B_HEADS, B_QK)
    uq = jnp.pad(uq, ((0, 0), (0, 0), (0, LANE - B_QK))).reshape(B_Q_LORA, B_HEADS * LANE)
    ukv = b_w_ukv[i].reshape(B_KV_LORA, B_HEADS, B_NOPE + B_V)
    uk = jnp.pad(ukv[:, :, :B_NOPE], ((0, 0), (0, 0), (0, LANE - B_NOPE))).reshape(B_KV_LORA, B_HEADS * LANE)
    uv = ukv[:, :, B_NOPE:].reshape(B_KV_LORA, BW)
    row = lambda t: t.reshape(1, -1)
    return dict(
        w_mix=w_mix, w_gate=w_gate,
        mu_prev=row(a_mu_prev[i]), mu_next=row(a_mu_next[i]),
        w0=a_w0[i], w_up=a_w_up[i], a0=a_a0[i], a_up=a_a_up[i],
        k_k=row(a_k_k[i]), k_a=row(a_k_a[i]), r_k=row(a_r_k[i]),
        gn_g=row(a_gn_g[i]), gn_b=row(a_gn_b[i]),
        q_ln=row(b_q_ln[i]), kv_ln=row(b_kv_ln[i]),
        w_uq=uq.astype(BF16), w_uk=uk.astype(BF16), w_uv=uv.astype(BF16),
        tab_bq=_fold_gain(tab_b, _pad_lanes(b_qn_g[i], LANE), B_ROPE // 4, B_QK ** -0.5 * LOG2E),
        tab_bk=_fold_gain(tab_b, _pad_lanes(b_kn_g[i], LANE), B_ROPE // 4, 1.0),
        tab_cq=_fold_gain(tab_c, jnp.tile(c_qn_g[i], 2), C_HEAD // 4, C_HEAD ** -0.5 * LOG2E),
        tab_ck=_fold_gain(tab_c, jnp.tile(c_kn_g[i], 2), C_HEAD // 4, 1.0),
        sink=c_sink[i] * LOG2E,
        w_bo=w_branch_out[i].astype(BF16), w_out=w_out[i].astype(BF16),
    )


def kernel(x, c, ctx, c_ctx, ada_w, ada_b, norm_g, w_in, a_mu_prev, a_mu_next, a_w0, a_w_up, a_a0, a_a_up, a_k_k, a_k_a, a_r_k, a_gn_g, a_gn_b, b_q_ln, b_kv_ln, b_w_uq, b_w_ukv, b_qn_g, b_kn_g, c_qn_g, c_kn_g, c_sink, w_branch_out, w_out):
    B, T, D = x.shape
    L = ctx.shape[1]
    depth = ada_w.shape[0]
    assert D == D_MODEL and L % TOK_TILE == 0 and T % TOK_TILE == 0 and T % GRID_W == 0
    assert T % ATTB_TQ == 0 and B % SCAN_ROWS == 0 and L % (ATTC_BLOCKS * WINDOW) == 0

    mod_rows = -(-(B + 1) // 8) * 8
    cs = jnp.concatenate([c, c_ctx[None, :], jnp.zeros((mod_rows - B - 1, D), c.dtype)], axis=0)
    mod = _mod_call(cs, ada_w, ada_b)

    tab_b = _rope_tables(L, T, B_ROPE, LANE, B_NOPE)
    tab_c = _rope_tables(L, T, C_HEAD, C_HEAD, 0)
    ones_a = _block_ones(MXU_W, A_HEAD)

    xall = jnp.concatenate([ctx, x], axis=1)
    for i in range(depth):
        with_ctx = i < depth - 1
        lp = _layer_params(i, tab_b, tab_c, w_in, a_mu_prev, a_mu_next, a_w0, a_w_up, a_a0, a_a_up, a_k_k, a_k_a,
                           a_r_k, a_gn_g, a_gn_b, b_q_ln, b_kv_ln, b_w_uq, b_w_ukv, b_qn_g, b_kn_g,
                           c_qn_g, c_kn_g, c_sink, w_branch_out, w_out)
        zb, zc, rv, kk, dkn, lw, bonus = _inprep_call(xall, mod[i], norm_g[i], lp, ones_a, L)
        y0, y1 = _scan_call(rv, kk, dkn, lw, L)
        qb, kb, vb, qc, kc, vc = _bcproj_call(zb, zc, lp)
        yb_lat = _attb_call(qb, kb, vb, L, T, ATTB_TQ, L + T)
        yb_ctx = _attb_call(qb, kb, vb, 0, L, TOK_TILE, L) if with_ctx else yb_lat
        yc = _attc_call(qc, kc, vc, lp["sink"], L, with_ctx)
        xall = _merge_call(y0, y1, bonus, yb_ctx, yb_lat, yc, xall, mod[i], norm_g[i], lp, ones_a, L,
                           with_ctx)
    return xall
```
